```python
import numpy as np
import jax
import jax.numpy as jnp
from jax import lax

D_MODEL = 1024
BATCH = 16
SEQ = 2048
DEPTH = 2

CTX_LEN = 256
GRID_W = 64
N_EVEN = (DEPTH + 1) // 2
N_ODD = DEPTH // 2
RMS_EPS = 1e-6
LN_EPS = 1e-5
ADA_CHUNKS = 6

GLA_HEADS = 4
GLA_DK = 64
GLA_DV = 128
GLA_LOWRANK = 16
GLA_GATE_NORM = 16.0
GLA_CHUNK = 64
ROPE_BASE = 10000.0
GLA_QK = GLA_HEADS * GLA_DK
GLA_V = GLA_HEADS * GLA_DV

CONV_CH = 512
CONV_WIDTH = 31

AB_SPLIT = (GLA_QK, GLA_QK, GLA_V, GLA_V, GLA_LOWRANK, GLA_LOWRANK, CONV_CH, CONV_CH)
AB_IN = sum(AB_SPLIT)
AB_MIX = GLA_V + CONV_CH

NA_HEADS = 16
NA_DH = 64
NA_WIN_ROWS = 8
NA_WIN_COLS = 16
NA_QCOLS = 16
NA_KCOLS = 32

N_EXPERTS = 256
TOP_K = 8
N_GROUPS = 8
TOPK_GROUPS = 4
D_EXPERT = 256
D_SHARED = 256
ROUTE_SCALE = 2.5
MOE_BLOCK = 128

kernel_name = 'hybrid_gla_conformer_natten_moe_dit'


def rms_norm(h, g):
    hf = h.astype(jnp.float32)
    hf = hf * lax.rsqrt(jnp.mean(hf * hf, axis=-1, keepdims=True) + RMS_EPS)
    return (hf * g.astype(jnp.float32)).astype(h.dtype)


def layer_norm(h, g, b):
    hf = h.astype(jnp.float32)
    mu = jnp.mean(hf, axis=-1, keepdims=True)
    var = jnp.mean(jnp.square(hf - mu), axis=-1, keepdims=True)
    out = (hf - mu) * lax.rsqrt(var + LN_EPS) * g.astype(jnp.float32) + b.astype(jnp.float32)
    return out.astype(h.dtype)


def axial_rope(t, rows, cols):
    half = t.shape[-1] // 2
    nf = half // 2
    inv = ROPE_BASE ** (-jnp.arange(nf, dtype=jnp.float32) / nf)

    def rot(u, pos):
        ang = pos[:, None] * inv[None, :]
        cos, sin = jnp.cos(ang), jnp.sin(ang)
        u1, u2 = u[..., :nf], u[..., nf:]
        return jnp.concatenate([u1 * cos - u2 * sin, u1 * sin + u2 * cos], axis=-1)

    return jnp.concatenate([rot(t[..., :half], rows), rot(t[..., half:], cols)], axis=-1)


def gla_chunked(q, k, v, logd, s0):
    B, H, T, dk = q.shape
    dv = v.shape[-1]
    L = GLA_CHUNK
    n = T // L
    q, k, logd = (t.reshape(B, H, n, L, dk) for t in (q, k, logd))
    v = v.reshape(B, H, n, L, dv)
    b = jnp.cumsum(logd, axis=3)
    b_mid = b[:, :, :, L // 2 - 1:L // 2, :]
    qe = q * jnp.exp(b - b_mid)
    ke = k * jnp.exp(b_mid - b)
    a = jnp.einsum('bhntk,bhnsk->bhnts', qe, ke)
    lower = jnp.tril(jnp.ones((L, L), dtype=bool))
    a = jnp.where(lower, a, 0.0)
    o_intra = jnp.einsum('bhnts,bhnsv->bhntv', a, v)
    b_last = b[:, :, :, -1:, :]
    q_in = q * jnp.exp(b)
    k_out = k * jnp.exp(b_last - b)
    dec = jnp.exp(b_last[:, :, :, 0, :])

    def step(s, inp):
        qi, ko, vi, di = inp
        o = jnp.einsum('bhtk,bhkv->bhtv', qi, s)
        s = s * di[..., None] + jnp.einsum('bhtk,bhtv->bhkv', ko, vi)
        return s, o

    mv = lambda t: jnp.moveaxis(t, 2, 0)
    s_fin, o_inter = lax.scan(step, s0, (mv(q_in), mv(k_out), mv(v), mv(dec)))
    o = o_intra + jnp.moveaxis(o_inter, 0, 2)
    return o.reshape(B, H, T, dv), s_fin


def depthwise_conv(u, w, b):
    ch = u.shape[-1]
    out = lax.conv_general_dilated(
        u, w[:, None, :].astype(u.dtype), window_strides=(1,),
        padding=[(CONV_WIDTH // 2, CONV_WIDTH // 2)],
        dimension_numbers=('NWC', 'WIO', 'NWC'), feature_group_count=ch)
    return out + b


def gla_conv_project(h, w_in, dw_f, db_f, dw_b, db_b):
    B, T, _ = h.shape
    z = h @ w_in
    q, k, v, g, lr_f, lr_b, glu_a, glu_b = jnp.split(z, list(np.cumsum(AB_SPLIT)[:-1]), axis=-1)
    heads = lambda t, d: t.reshape(B, T, GLA_HEADS, d).transpose(0, 2, 1, 3).astype(jnp.float32)
    q = heads(q, GLA_DK) * (GLA_DK ** -0.5)
    k = heads(k, GLA_DK)
    v = heads(v, GLA_DV)
    logd_f = heads(jax.nn.log_sigmoid((lr_f @ dw_f + db_f).astype(jnp.float32)) / GLA_GATE_NORM, GLA_DK)
    logd_b = heads(jax.nn.log_sigmoid((lr_b @ dw_b + db_b).astype(jnp.float32)) / GLA_GATE_NORM, GLA_DK)
    u = glu_a * jax.nn.sigmoid(glu_b)
    return q, k, v, logd_f, logd_b, g, u


def ab_mixer(hx, hc, w_in, w_out, dw_f, db_f, dw_b, db_b, gla_g, conv_w, conv_b, ln_g, ln_b):
    B, S, _ = hx.shape
    qx, kx, vx, lfx, lbx, gx, ux = gla_conv_project(hx, w_in, dw_f, db_f, dw_b, db_b)
    qc, kc, vc, lfc, lbc, gc, uc = gla_conv_project(hc, w_in, dw_f, db_f, dw_b, db_b)
    t = jnp.arange(S)
    rows = (t // GRID_W).astype(jnp.float32)
    cols = (t % GRID_W).astype(jnp.float32)
    qx = axial_rope(qx, rows, cols)
    kx = axial_rope(kx, rows, cols)
    zero = jnp.zeros((B, GLA_HEADS, GLA_DK, GLA_DV), jnp.float32)
    fl = lambda a: jnp.flip(a, axis=2)
    oc_f, sc_f = gla_chunked(qc, kc, vc, lfc, zero)
    ox_f, _ = gla_chunked(qx, kx, vx, lfx, sc_f)
    oc_b, sc_b = gla_chunked(fl(qc), fl(kc), fl(vc), fl(lbc), zero)
    ox_b, _ = gla_chunked(fl(qx), fl(kx), fl(vx), fl(lbx), sc_b)

    def finish(o, g, u):
        T = o.shape[2]
        o = o * lax.rsqrt(jnp.mean(o * o, axis=-1, keepdims=True) + RMS_EPS)
        o = o.transpose(0, 2, 1, 3) * gla_g.astype(jnp.float32).reshape(GLA_HEADS, GLA_DV)
        a_out = o.reshape(B, T, GLA_V).astype(g.dtype) * jax.nn.silu(g)
        cv = depthwise_conv(u, conv_w, conv_b)
        cv = jax.nn.silu(layer_norm(cv, ln_g, ln_b))
        return jnp.concatenate([a_out, cv], axis=-1) @ w_out

    return finish(ox_f + fl(ox_b), gx, ux), finish(oc_f + fl(oc_b), gc, uc)


def na_mixer(hx, hc, w_in, w_out, rpb, need_ctx):
    B, S, _ = hx.shape
    C = hc.shape[1]
    n_rows = S // GRID_W
    wr = min(NA_WIN_ROWS, n_rows)
    H, dh = NA_HEADS, NA_DH
    scale = dh ** -0.5

    def qkv(h):
        T = h.shape[1]
        z = (h @ w_in).reshape(B, T, 3, H, dh)
        return [z[:, :, j].transpose(0, 2, 1, 3) for j in range(3)]

    q, k, v = qkv(hx)
    qc, kc, vc = qkv(hc)

    n_cb = GRID_W // NA_QCOLS
    qcols = np.arange(GRID_W).reshape(n_cb, NA_QCOLS)
    cstart = np.clip(qcols - NA_WIN_COLS // 2, 0, GRID_W - NA_WIN_COLS)
    band = np.minimum(cstart[:, 0], GRID_W - NA_KCOLS)
    kcols = band[:, None] + np.arange(NA_KCOLS)[None, :]
    col_ok = (kcols[:, None, :] >= cstart[:, :, None]) & (kcols[:, None, :] < cstart[:, :, None] + NA_WIN_COLS)
    col_idx = np.clip(kcols[:, None, :] - qcols[:, :, None] + NA_WIN_COLS - 1, 0, 2 * NA_WIN_COLS - 2)
    rpb_c = rpb[:, :, col_idx]
    nw = wr * NA_KCOLS
    mask = jnp.asarray(np.repeat(col_ok[:, :, None, :], wr, axis=2).reshape(n_cb, NA_QCOLS, nw))

    qg = q.reshape(B, H, n_rows, GRID_W, dh)
    kg = k.reshape(B, H, n_rows, GRID_W, dh)
    vg = v.reshape(B, H, n_rows, GRID_W, dh)

    def row_block(r):
        r0 = jnp.clip(r - wr // 2, 0, n_rows - wr)
        qb = lax.dynamic_index_in_dim(qg, r, axis=2, keepdims=False).reshape(B, H, n_cb, NA_QCOLS, dh)

        def band_of(t):
            t = lax.dynamic_slice_in_dim(t, r0, wr, axis=2)[:, :, :, kcols]
            return t.transpose(0, 1, 3, 2, 4, 5).reshape(B, H, n_cb, nw, dh)

        kb, vb = band_of(kg), band_of(vg)
        ridx = r0 + jnp.arange(wr) - r + NA_WIN_ROWS - 1
        bias = rpb_c[:, ridx].transpose(0, 2, 3, 1, 4).reshape(H, n_cb, NA_QCOLS, nw)
        s_win = jnp.einsum('bhcqd,bhckd->bhcqk', qb, kb).astype(jnp.float32) * scale + bias.astype(jnp.float32)
        s_win = jnp.where(mask, s_win, -jnp.inf)
        s_ctx = jnp.einsum('bhcqd,bhkd->bhcqk', qb, kc).astype(jnp.float32) * scale
        p = jax.nn.softmax(jnp.concatenate([s_win, s_ctx], axis=-1), axis=-1).astype(vb.dtype)
        o = (jnp.einsum('bhcqk,bhckd->bhcqd', p[..., :nw], vb)
             + jnp.einsum('bhcqk,bhkd->bhcqd', p[..., nw:], vc))
        return o.reshape(B, H, GRID_W, dh)

    o = lax.map(row_block, jnp.arange(n_rows))
    yx = o.transpose(1, 0, 3, 2, 4).reshape(B, S, H * dh) @ w_out
    yc = None
    if need_ctx:
        sc = jnp.einsum('bhqd,bhkd->bhqk', qc, kc).astype(jnp.float32) * scale
        oc = jnp.einsum('bhqk,bhkd->bhqd', jax.nn.softmax(sc, axis=-1).astype(vc.dtype), vc)
        yc = oc.transpose(0, 2, 1, 3).reshape(B, C, H * dh) @ w_out
    return yx, yc


def moe_ffn(h, router_w, router_b, w_gate, w_up, w_down, ws_gate, ws_up, ws_down):
    N, D = h.shape
    E = w_gate.shape[0]
    scores = jax.nn.sigmoid(h.astype(jnp.float32) @ router_w.astype(jnp.float32))
    biased = scores + router_b.astype(jnp.float32)
    grp = biased.reshape(N, N_GROUPS, E // N_GROUPS)
    grp_score = jnp.sum(lax.top_k(grp, 2)[0], axis=-1)
    _, grp_idx = lax.top_k(grp_score, TOPK_GROUPS)
    grp_keep = jnp.sum(jax.nn.one_hot(grp_idx, N_GROUPS, dtype=jnp.float32), axis=1) > 0
    masked = jnp.where(grp_keep[:, :, None], grp, -jnp.inf).reshape(N, E)
    _, e_idx = lax.top_k(masked, TOP_K)
    gates = jnp.take_along_axis(scores, e_idx, axis=-1)
    gates = gates / jnp.sum(gates, axis=-1, keepdims=True) * ROUTE_SCALE

    y0 = ((jax.nn.silu(h @ ws_gate) * (h @ ws_up)) @ ws_down).astype(jnp.float32)

    A = N * TOP_K
    flat_e = e_idx.reshape(-1)
    flat_tok = jnp.repeat(jnp.arange(N, dtype=jnp.int32), TOP_K)
    flat_w = gates.reshape(-1)
    order = jnp.argsort(flat_e)
    se, stok, sw = flat_e[order], flat_tok[order], flat_w[order]
    counts = jnp.bincount(flat_e, length=E)
    starts = jnp.cumsum(counts) - counts
    padded = (counts + MOE_BLOCK - 1) // MOE_BLOCK * MOE_BLOCK
    pend = jnp.cumsum(padded)
    pstart = pend - padded
    pos = pstart[se] + jnp.arange(A, dtype=jnp.int32) - starts[se]
    P = -(-(A + E * (MOE_BLOCK - 1)) // MOE_BLOCK) * MOE_BLOCK
    nb = P // MOE_BLOCK
    slot_tok = jnp.zeros((P,), jnp.int32).at[pos].set(stok)
    slot_w = jnp.zeros((P,), jnp.float32).at[pos].set(sw)
    block_e = jnp.minimum(jnp.searchsorted(pend, jnp.arange(nb, dtype=jnp.int32) * MOE_BLOCK, side='right'), E - 1)

    def add_block(y, blk):
        tok, wt, e = blk
        xb = h[tok]
        hid = jax.nn.silu(xb @ w_gate[e]) * (xb @ w_up[e])
        out = (hid @ w_down[e]).astype(jnp.float32) * wt[:, None]
        return y.at[tok].add(out), None

    y, _ = lax.scan(add_block, y0, (slot_tok.reshape(nb, MOE_BLOCK), slot_w.reshape(nb, MOE_BLOCK), block_e))
    return y.astype(h.dtype)


def setup_inputs(seed: int = 0) -> dict:
    key = jax.random.key(seed)
    ks = jax.random.split(key, 40)
    D = D_MODEL

    def nrm(i, shape, std):
        return std * jax.random.normal(ks[i], shape, jnp.float32)

    def gain(i, shape):
        return 1.0 + nrm(i, shape, 0.02)

    return {
        'x': nrm(0, (BATCH, SEQ, D), 1.0),
        'c': nrm(1, (BATCH, D), 1.0),
        'ctx': nrm(2, (BATCH, CTX_LEN, D), 1.0),
        'c_ctx': nrm(3, (D,), 1.0),
        'ada_w': nrm(4, (DEPTH, D, ADA_CHUNKS * D), 0.2 * D ** -0.5),
        'ada_b': nrm(5, (DEPTH, ADA_CHUNKS * D), 0.02),
        'g_pre_mix': gain(6, (DEPTH, D)),
        'g_post_mix': gain(7, (DEPTH, D)),
        'g_pre_ffn': gain(8, (DEPTH, D)),
        'g_post_ffn': gain(9, (DEPTH, D)),
        'ab_w_in': nrm(10, (N_EVEN, D, AB_IN), D ** -0.5),
        'ab_w_out': nrm(11, (N_EVEN, AB_MIX, D), AB_MIX ** -0.5),
        'gla_dw_f': nrm(12, (N_EVEN, GLA_LOWRANK, GLA_QK), GLA_LOWRANK ** -0.5),
        'gla_db_f': nrm(13, (N_EVEN, GLA_QK), 0.1),
        'gla_dw_b': nrm(14, (N_EVEN, GLA_LOWRANK, GLA_QK), GLA_LOWRANK ** -0.5),
        'gla_db_b': nrm(15, (N_EVEN, GLA_QK), 0.1),
        'gla_norm_g': gain(16, (N_EVEN, GLA_V)),
        'conv_w': nrm(17, (N_EVEN, CONV_WIDTH, CONV_CH), CONV_WIDTH ** -0.5),
        'conv_b': nrm(18, (N_EVEN, CONV_CH), 0.02),
        'conv_ln_g': gain(19, (N_EVEN, CONV_CH)),
        'conv_ln_b': nrm(20, (N_EVEN, CONV_CH), 0.02),
        'na_w_in': nrm(21, (N_ODD, D, 3 * NA_HEADS * NA_DH), D ** -0.5),
        'na_w_out': nrm(22, (N_ODD, NA_HEADS * NA_DH, D), (NA_HEADS * NA_DH) ** -0.5),
        'na_rpb': nrm(23, (N_ODD, NA_HEADS, 2 * NA_WIN_ROWS - 1, 2 * NA_WIN_COLS - 1), 0.1),
        'moe_router_w': nrm(24, (DEPTH, D, N_EXPERTS), D ** -0.5),
        'moe_router_b': nrm(25, (DEPTH, N_EXPERTS), 0.01),
        'moe_w_gate': nrm(26, (DEPTH, N_EXPERTS, D, D_EXPERT), D ** -0.5),
        'moe_w_up': nrm(27, (DEPTH, N_EXPERTS, D, D_EXPERT), D ** -0.5),
        'moe_w_down': nrm(28, (DEPTH, N_EXPERTS, D_EXPERT, D), D_EXPERT ** -0.5),
        'moe_ws_gate': nrm(29, (DEPTH, D, D_SHARED), D ** -0.5),
        'moe_ws_up': nrm(30, (DEPTH, D, D_SHARED), D ** -0.5),
        'moe_ws_down': nrm(31, (DEPTH, D_SHARED, D), D_SHARED ** -0.5),
    }


def reference(x, c, ctx, c_ctx, ada_w, ada_b, g_pre_mix, g_post_mix, g_pre_ffn, g_post_ffn,
              ab_w_in, ab_w_out, gla_dw_f, gla_db_f, gla_dw_b, gla_db_b, gla_norm_g,
              conv_w, conv_b, conv_ln_g, conv_ln_b, na_w_in, na_w_out, na_rpb,
              moe_router_w, moe_router_b, moe_w_gate, moe_w_up, moe_w_down,
              moe_ws_gate, moe_ws_up, moe_ws_down):
    B, S, D = x.shape
    C = ctx.shape[1]
    for layer in range(DEPTH):
        last = layer == DEPTH - 1
        i = layer // 2
        mod_x = jax.nn.silu(c) @ ada_w[layer] + ada_b[layer]
        mod_c = jax.nn.silu(c_ctx) @ ada_w[layer] + ada_b[layer]
        shx1, scx1, gx1, shx2, scx2, gx2 = jnp.split(mod_x[:, None, :], ADA_CHUNKS, axis=-1)
        shc1, scc1, gc1, shc2, scc2, gc2 = jnp.split(mod_c[None, None, :], ADA_CHUNKS, axis=-1)

        hx = rms_norm(x, g_pre_mix[layer]) * (1.0 + scx1) + shx1
        hc = rms_norm(ctx, g_pre_mix[layer]) * (1.0 + scc1) + shc1
        if layer % 2 == 0:
            yx, yc = ab_mixer(hx, hc, ab_w_in[i], ab_w_out[i], gla_dw_f[i], gla_db_f[i], gla_dw_b[i], gla_db_b[i],
                              gla_norm_g[i], conv_w[i], conv_b[i], conv_ln_g[i], conv_ln_b[i])
        else:
            yx, yc = na_mixer(hx, hc, na_w_in[i], na_w_out[i], na_rpb[i], not last)
        x = x + gx1 * rms_norm(yx, g_post_mix[layer])
        if not last:
            ctx = ctx + gc1 * rms_norm(yc, g_post_mix[layer])

        hx = (rms_norm(x, g_pre_ffn[layer]) * (1.0 + scx2) + shx2).reshape(B * S, D)
        if last:
            tokens = hx
        else:
            hc = (rms_norm(ctx, g_pre_ffn[layer]) * (1.0 + scc2) + shc2).reshape(B * C, D)
            tokens = jnp.concatenate([hx, hc], axis=0)
        yt = moe_ffn(tokens, moe_router_w[layer], moe_router_b[layer], moe_w_gate[layer], moe_w_up[layer],
                     moe_w_down[layer], moe_ws_gate[layer], moe_ws_up[layer], moe_ws_down[layer])
        x = x + gx2 * rms_norm(yt[:B * S].reshape(B, S, D), g_post_ffn[layer])
        if not last:
            ctx = ctx + gc2 * rms_norm(yt[B * S:].reshape(B, C, D), g_post_ffn[layer])
    return x
```

```python
import functools

import numpy as np
import jax
import jax.numpy as jnp
from jax import lax
from jax.experimental import pallas as pl
from jax.experimental.pallas import tpu as pltpu

F32 = jnp.float32
BF16 = jnp.bfloat16

GRID_W = 64
RMS_EPS = 1e-6
LN_EPS = 1e-5
ADA_CHUNKS = 6
GLA_HEADS = 4
GLA_DK = 64
GLA_DV = 128
GLA_LOWRANK = 16
GLA_GATE_NORM = 16.0
GLA_CHUNK = 64
ROPE_BASE = 10000.0
GLA_QK = GLA_HEADS * GLA_DK
GLA_V = GLA_HEADS * GLA_DV
CONV_CH = 512
CONV_WIDTH = 31
NA_HEADS = 16
NA_DH = 64
NA_WIN_ROWS = 8
NA_WIN_COLS = 16
N_EXPERTS = 256
TOP_K = 8
N_GROUPS = 8
TOPK_GROUPS = 4
ROUTE_SCALE = 2.5

LANES = 128
ROW_TILE = 512
ADA_ROWS = 24
ADA_TN = 1536
CONV_RC = 32
CONV_HALO = 16
MOE_RB = 256
COMB_TT = 32
VMEM_LIMIT = 56 * 1024 * 1024

_NT = (((1,), (1,)), ((), ()))
_TN = (((0,), (0,)), ((), ()))


def _params(sem, **kw):
    return pltpu.CompilerParams(dimension_semantics=sem, vmem_limit_bytes=VMEM_LIMIT, **kw)


def _sigmoid(x):
    return 1.0 / (1.0 + jnp.exp(-x))


def _split(a):
    hi = a.astype(BF16)
    lo = (a - hi.astype(F32)).astype(BF16)
    return hi, lo


def _dot3(a, b_hi, b_lo, dims=(((1,), (0,)), ((), ()))):
    a_hi, a_lo = _split(a)
    d = lambda x, y: lax.dot_general(x, y, dims, preferred_element_type=F32)
    return d(a_hi, b_hi) + (d(a_hi, b_lo) + d(a_lo, b_hi))


def _prenorm_mod(x, g, mod_ref, shift_i, scale_i):
    ms = jnp.mean(x * x, axis=-1, keepdims=True)
    h = x * lax.rsqrt(ms + RMS_EPS) * g
    return h * (1.0 + mod_ref[0, scale_i:scale_i + 1, :]) + mod_ref[0, shift_i:shift_i + 1, :]


def _mod_index(tile_rows, seq, batch):
    per_batch = seq // tile_rows
    return lambda i, *_: (jnp.minimum(i // per_batch, batch), 0, 0)


def _ada_kernel(c_ref, w_ref, b_ref, o_ref):
    c = c_ref[...]
    s = c * _sigmoid(c)
    w_hi, w_lo = _split(w_ref[...])
    o_ref[...] = _dot3(s, w_hi, w_lo) + b_ref[...]


def _ada_mod(cc, ada_w, ada_b):
    depth, d, n = ada_w.shape
    return pl.pallas_call(
        _ada_kernel,
        grid=(depth, n // ADA_TN),
        in_specs=[pl.BlockSpec((ADA_ROWS, d), lambda l, j: (0, 0)),
                  pl.BlockSpec((None, d, ADA_TN), lambda l, j: (l, 0, j)),
                  pl.BlockSpec((None, 1, ADA_TN), lambda l, j: (l, 0, j))],
        out_specs=pl.BlockSpec((None, ADA_ROWS, ADA_TN), lambda l, j: (l, 0, j)),
        out_shape=jax.ShapeDtypeStruct((depth, ADA_ROWS, n), F32),
        compiler_params=_params(("arbitrary", "arbitrary")),
        name="ada_mod",
    )(cc, ada_w, ada_b.reshape(depth, 1, n))


def _prenorm_proj_kernel(x_ref, mod_ref, g_ref, w_ref, *o_refs, shift_i, scale_i):
    h = _prenorm_mod(x_ref[...], g_ref[...], mod_ref, shift_i, scale_i).astype(BF16)
    off = 0
    for o_ref in o_refs:
        n = o_ref.shape[-1]
        o_ref[...] = jnp.dot(h, w_ref[:, off:off + n], preferred_element_type=F32).astype(o_ref.dtype)
        off += n


def _prenorm_proj(xt, mod, g, w, splits, shift_i, scale_i, seq, batch):
    rows, d = xt.shape
    n = w.shape[1]
    return pl.pallas_call(
        functools.partial(_prenorm_proj_kernel, shift_i=shift_i, scale_i=scale_i),
        grid=(rows // ROW_TILE,),
        in_specs=[pl.BlockSpec((ROW_TILE, d), lambda i: (i, 0)),
                  pl.BlockSpec((1, ADA_CHUNKS, d), _mod_index(ROW_TILE, seq, batch)),
                  pl.BlockSpec((1, d), lambda i: (0, 0)),
                  pl.BlockSpec((d, n), lambda i: (0, 0))],
        out_specs=[pl.BlockSpec((ROW_TILE, s), lambda i: (i, 0)) for s in splits],
        out_shape=[jax.ShapeDtypeStruct((rows, s), BF16) for s in splits],
        compiler_params=_params(("arbitrary",)),
        name="prenorm_proj",
    )(xt, mod, g.reshape(1, d), w)


def _gla_kernel(qkx_ref, vx_ref, gx_ref, lrx_ref, qkc_ref, vc_ref, gc_ref, lrc_ref,
                dw_ref, db_ref, cos_ref, sin_ref, gg_ref, ox_ref, oc_ref,
                sf_ref, sb_ref, accx_ref, accc_ref, *, n_x, n_c):
    L = GLA_CHUNK
    row = lax.broadcasted_iota(jnp.int32, (L, L), 0)
    col = lax.broadcasted_iota(jnp.int32, (L, L), 1)
    lower = row >= col
    upper = col >= row
    tri_f = jnp.where(lower, 1.0, 0.0).astype(BF16)
    tri_b = jnp.where(upper, 1.0, 0.0).astype(BF16)
    lane = lax.broadcasted_iota(jnp.int32, (1, GLA_QK), 1)
    head_of_lane = lane // GLA_DK
    first_half = (lane % (GLA_DK // 2)) < (GLA_DK // 4)
    bd_mask = (lax.broadcasted_iota(jnp.int32, (GLA_V, GLA_QK), 0) // GLA_DV
               == lax.broadcasted_iota(jnp.int32, (GLA_V, GLA_QK), 1) // GLA_DK)

    def swap_pairs(t):
        nf = GLA_DK // 4
        return jnp.where(first_half, pltpu.roll(t, GLA_QK - nf, 1), pltpu.roll(t, nf, 1))

    def chunk(fwd, c, qk_ref, v_ref, lr_ref, latent):
        rows = pl.ds(pl.multiple_of(c * L, L), L)
        q = qk_ref[rows, 0:GLA_QK].astype(F32) * (GLA_DK ** -0.5)
        k = qk_ref[rows, GLA_QK:2 * GLA_QK].astype(F32)
        if latent:
            cs = cos_ref[rows, :]
            sn = sin_ref[rows, :]
            q = q * cs + swap_pairs(q) * sn
            k = k * cs + swap_pairs(k) * sn
        dsl = slice(0, GLA_QK) if fwd else slice(GLA_QK, 2 * GLA_QK)
        pre = jnp.dot(lr_ref[rows, :], dw_ref[:, dsl], preferred_element_type=F32) + db_ref[:, dsl]
        logd = (jnp.minimum(pre, 0.0) - jnp.log(1.0 + jnp.exp(-jnp.abs(pre)))) * (1.0 / GLA_GATE_NORM)
        d_hi, d_lo = _split(logd)
        tri = tri_f if fwd else tri_b
        b = (jnp.dot(tri, d_hi, preferred_element_type=F32)
             + jnp.dot(tri, d_lo, preferred_element_type=F32))
        mid = L // 2 - 1 if fwd else L // 2
        end = L - 1 if fwd else 0
        b_mid = b[mid:mid + 1, :]
        b_end = b[end:end + 1, :]
        qe = q * jnp.exp(b - b_mid)
        ke = (k * jnp.exp(b_mid - b)).astype(BF16)
        q_in = (q * jnp.exp(b)).astype(BF16)
        k_out = (k * jnp.exp(b_end - b)).astype(BF16)
        dec = jnp.exp(b_end)
        vv = v_ref[rows, :]
        mask = lower if fwd else upper
        outs = []
        for h in range(GLA_HEADS):
            qm = jnp.where(head_of_lane == h, qe, 0.0).astype(BF16)
            a = lax.dot_general(qm, ke, _NT, preferred_element_type=F32)
            a = jnp.where(mask, a, 0.0).astype(BF16)
            outs.append(jnp.dot(a, vv[:, h * GLA_DV:(h + 1) * GLA_DV], preferred_element_type=F32))
        o_intra = jnp.concatenate(outs, axis=-1)
        s_ref = sf_ref if fwd else sb_ref
        st = s_ref[...]
        o_inter = lax.dot_general(q_in, st.astype(BF16), _NT, preferred_element_type=F32)
        upd = lax.dot_general(vv, k_out, _TN, preferred_element_type=F32)
        s_ref[...] = jnp.where(bd_mask, st * dec + upd, 0.0)
        return o_intra + o_inter

    def finish(o, g_ref, c):
        rows = pl.ds(pl.multiple_of(c * L, L), L)
        g = g_ref[rows, :].astype(F32)
        parts = []
        for h in range(GLA_HEADS):
            oh = o[:, h * GLA_DV:(h + 1) * GLA_DV]
            ms = jnp.mean(oh * oh, axis=-1, keepdims=True)
            parts.append(oh * lax.rsqrt(ms + RMS_EPS))
        on = jnp.concatenate(parts, axis=-1) * gg_ref[...]
        return (on * (g * _sigmoid(g))).astype(BF16)

    def run_seq(n, qk_ref, v_ref, g_ref, lr_ref, acc_ref, o_ref, latent):
        half = n // 2

        def rows_of(c):
            return pl.ds(pl.multiple_of(c * L, L), L)

        def first(i, carry):
            cf, cb = i, n - 1 - i
            acc_ref[rows_of(cf), :] = chunk(True, cf, qk_ref, v_ref, lr_ref, latent)
            acc_ref[rows_of(cb), :] = chunk(False, cb, qk_ref, v_ref, lr_ref, latent)
            return carry

        def second(i, carry):
            cf, cb = i, n - 1 - i
            of = acc_ref[rows_of(cf), :] + chunk(True, cf, qk_ref, v_ref, lr_ref, latent)
            o_ref[rows_of(cf), :] = finish(of, g_ref, cf)
            ob = acc_ref[rows_of(cb), :] + chunk(False, cb, qk_ref, v_ref, lr_ref, latent)
            o_ref[rows_of(cb), :] = finish(ob, g_ref, cb)
            return carry

        lax.fori_loop(0, half, first, 0)
        lax.fori_loop(half, n, second, 0)

    sf_ref[...] = jnp.zeros_like(sf_ref)
    sb_ref[...] = jnp.zeros_like(sb_ref)
    run_seq(n_c, qkc_ref, vc_ref, gc_ref, lrc_ref, accc_ref, oc_ref, False)
    run_seq(n_x, qkx_ref, vx_ref, gx_ref, lrx_ref, accx_ref, ox_ref, True)


def _rope_tables(seq):
    half = GLA_DK // 2
    nf = half // 2
    inv = ROPE_BASE ** (-jnp.arange(nf, dtype=F32) / nf)
    t = jnp.arange(seq)
    rows = (t // GRID_W).astype(F32)
    cols = (t % GRID_W).astype(F32)
    ar = rows[:, None] * inv[None, :]
    ac = cols[:, None] * inv[None, :]
    cos_h = jnp.concatenate([jnp.cos(ar), jnp.cos(ar), jnp.cos(ac), jnp.cos(ac)], axis=-1)
    sin_h = jnp.concatenate([-jnp.sin(ar), jnp.sin(ar), -jnp.sin(ac), jnp.sin(ac)], axis=-1)
    return jnp.tile(cos_h, (1, GLA_HEADS)), jnp.tile(sin_h, (1, GLA_HEADS))


def _gla(qk, v, g, lr, dwcat, dbcat, gla_g, seq, ctx_len, batch):
    rows = qk.shape[0]
    cos_t, sin_t = _rope_tables(seq)
    cb = (batch * seq) // ctx_len
    xs = lambda w: pl.BlockSpec((seq, w), lambda b: (b, 0))
    cs = lambda w: pl.BlockSpec((ctx_len, w), lambda b: (cb + b, 0))
    full = lambda a: pl.BlockSpec(a.shape, lambda b: (0,) * a.ndim)
    out_x, out_c = pl.pallas_call(
        functools.partial(_gla_kernel, n_x=seq // GLA_CHUNK, n_c=ctx_len // GLA_CHUNK),
        grid=(batch,),
        in_specs=[xs(2 * GLA_QK), xs(GLA_V), xs(GLA_V), xs(LANES),
                  cs(2 * GLA_QK), cs(GLA_V), cs(GLA_V), cs(LANES),
                  full(dwcat), full(dbcat), full(cos_t), full(sin_t), full(gla_g)],
        out_specs=[pl.BlockSpec((seq, GLA_V), lambda b: (b, 0)),
                   pl.BlockSpec((ctx_len, GLA_V), lambda b: (b, 0))],
        out_shape=[jax.ShapeDtypeStruct((batch * seq, GLA_V), BF16),
                   jax.ShapeDtypeStruct((batch * ctx_len, GLA_V), BF16)],
        scratch_shapes=[pltpu.VMEM((GLA_V, GLA_QK), F32), pltpu.VMEM((GLA_V, GLA_QK), F32),
                        pltpu.VMEM((seq, GLA_V), F32), pltpu.VMEM((ctx_len, GLA_V), F32)],
        compiler_params=_params(("arbitrary",)),
        name="gla",
    )(qk, v, g, lr, qk, v, g, lr, dwcat, dbcat, cos_t, sin_t, gla_g)
    del rows
    return jnp.concatenate([out_x, out_c], axis=0)


def _conv_kernel(glu_ref, w_ref, cb_ref, lg_ref, lb_ref, o_ref, pad_ref, win_ref, *, t_len):
    ch = CONV_CH
    zeros = jnp.zeros((CONV_HALO, ch), F32)
    pad_ref[0:CONV_HALO, :] = zeros
    pad_ref[CONV_HALO + t_len:2 * CONV_HALO + t_len, :] = zeros
    fill = 128

    def fill_body(i, carry):
        r = pl.multiple_of(i * fill, fill)
        a = glu_ref[pl.ds(r, fill), 0:ch].astype(F32)
        b = glu_ref[pl.ds(r, fill), ch:2 * ch].astype(F32)
        pad_ref[pl.ds(CONV_HALO + r, fill), :] = a * _sigmoid(b)
        return carry

    lax.fori_loop(0, t_len // fill, fill_body, 0)
    first_tap = CONV_HALO - CONV_WIDTH // 2

    def body(i, carry):
        base = pl.multiple_of(i * CONV_RC, CONV_RC)
        win_ref[...] = pad_ref[pl.ds(base, CONV_RC + 2 * CONV_HALO), :]
        acc = jnp.zeros((CONV_RC, ch), F32)
        for j in range(CONV_WIDTH):
            acc = acc + w_ref[j:j + 1, :] * win_ref[first_tap + j:first_tap + j + CONV_RC, :]
        acc = acc + cb_ref[...]
        mu = jnp.mean(acc, axis=-1, keepdims=True)
        cen = acc - mu
        var = jnp.mean(cen * cen, axis=-1, keepdims=True)
        y = cen * lax.rsqrt(var + LN_EPS) * lg_ref[...] + lb_ref[...]
        o_ref[pl.ds(base, CONV_RC), :] = (y * _sigmoid(y)).astype(BF16)
        return carry

    lax.fori_loop(0, t_len // CONV_RC, body, 0)


def _conv_module(glu, conv_w, conv_b, ln_g, ln_b, t_len, first_block, batch):
    full = lambda a: pl.BlockSpec(a.shape, lambda b: (0,) * a.ndim)
    return pl.pallas_call(
        functools.partial(_conv_kernel, t_len=t_len),
        grid=(batch,),
        in_specs=[pl.BlockSpec((t_len, 2 * CONV_CH), lambda b: (first_block + b, 0)),
                  full(conv_w), full(conv_b), full(ln_g), full(ln_b)],
        out_specs=pl.BlockSpec((t_len, CONV_CH), lambda b: (b, 0)),
        out_shape=jax.ShapeDtypeStruct((batch * t_len, CONV_CH), BF16),
        scratch_shapes=[pltpu.VMEM((t_len + 2 * CONV_HALO, CONV_CH), F32),
                        pltpu.VMEM((CONV_RC + 2 * CONV_HALO, CONV_CH), F32)],
        compiler_params=_params(("arbitrary",)),
        name="conv_module",
    )(glu, conv_w, conv_b, ln_g, ln_b)


def _outproj_kernel(*refs, n_lhs, gate_i):
    lhs_refs = refs[:n_lhs]
    w_ref, x_ref, mod_ref, g_ref, o_ref = refs[n_lhs:]
    off = 0
    y = None
    for l_ref in lhs_refs:
        kk = l_ref.shape[-1]
        t = jnp.dot(l_ref[...], w_ref[off:off + kk, :], preferred_element_type=F32)
        y = t if y is None else y + t
        off += kk
    ms = jnp.mean(y * y, axis=-1, keepdims=True)
    yn = y * lax.rsqrt(ms + RMS_EPS) * g_ref[...]
    o_ref[...] = x_ref[...] + mod_ref[0, gate_i:gate_i + 1, :] * yn


def _outproj(lhs_list, w, xt, mod, g, gate_i, n_rows, seq, batch):
    d = xt.shape[1]
    n_lhs = len(lhs_list)
    in_specs = [pl.BlockSpec((ROW_TILE, a.shape[1]), lambda i: (i, 0)) for a in lhs_list]
    in_specs += [pl.BlockSpec(w.shape, lambda i: (0, 0)),
                 pl.BlockSpec((ROW_TILE, d), lambda i: (i, 0)),
                 pl.BlockSpec((1, ADA_CHUNKS, d), _mod_index(ROW_TILE, seq, batch)),
                 pl.BlockSpec((1, d), lambda i: (0, 0))]
    return pl.pallas_call(
        functools.partial(_outproj_kernel, n_lhs=n_lhs, gate_i=gate_i),
        grid=(n_rows // ROW_TILE,),
        in_specs=in_specs,
        out_specs=pl.BlockSpec((ROW_TILE, d), lambda i: (i, 0)),
        out_shape=jax.ShapeDtypeStruct(xt.shape, F32),
        input_output_aliases={n_lhs + 1: 0},
        compiler_params=_params(("arbitrary",)),
        name="outproj",
    )(*lhs_list, w, xt, mod, g.reshape(1, d))


def _na_kernel(q_ref, k_ref, v_ref, kc_ref, vc_ref, bias_ref, o_ref, *, n_rows):
    wr = NA_WIN_ROWS
    lane = lax.broadcasted_iota(jnp.int32, (1, LANES), 1)
    head0 = lane < NA_DH
    kc = kc_ref[...]
    vc = vc_ref[...]

    def body(r, carry):
        r0 = jnp.clip(r - wr // 2, 0, n_rows - wr)
        delta = r - r0
        q2 = q_ref[pl.ds(pl.multiple_of(r * GRID_W, GRID_W), GRID_W), :] * (NA_DH ** -0.5)
        kw = k_ref[pl.ds(pl.multiple_of(r0 * GRID_W, GRID_W), wr * GRID_W), :]
        vw = v_ref[pl.ds(pl.multiple_of(r0 * GRID_W, GRID_W), wr * GRID_W), :]
        outs = []
        for hh in range(2):
            qm = jnp.where(head0 if hh == 0 else jnp.logical_not(head0), q2, jnp.zeros_like(q2))
            s_w = lax.dot_general(qm, kw, _NT, preferred_element_type=F32) + bias_ref[hh, delta]
            s_c = lax.dot_general(qm, kc, _NT, preferred_element_type=F32)
            m = jnp.maximum(jnp.max(s_w, axis=-1, keepdims=True), jnp.max(s_c, axis=-1, keepdims=True))
            p_w = jnp.exp(s_w - m)
            p_c = jnp.exp(s_c - m)
            den = jnp.sum(p_w, axis=-1, keepdims=True) + jnp.sum(p_c, axis=-1, keepdims=True)
            o = (jnp.dot(p_w.astype(BF16), vw, preferred_element_type=F32)
                 + jnp.dot(p_c.astype(BF16), vc, preferred_element_type=F32))
            outs.append(o / den)
        o_ref[pl.ds(pl.multiple_of(r * GRID_W, GRID_W), GRID_W), :] = jnp.where(head0, outs[0], outs[1]).astype(BF16)
        return carry

    lax.fori_loop(0, n_rows, body, 0)


def _na_bias_table(rpb, n_rows):
    wr = min(NA_WIN_ROWS, n_rows)
    qc = np.arange(GRID_W)
    cstart = np.clip(qc - NA_WIN_COLS // 2, 0, GRID_W - NA_WIN_COLS)
    kcol = np.arange(GRID_W)
    ok = (kcol[None, :] >= cstart[:, None]) & (kcol[None, :] < cstart[:, None] + NA_WIN_COLS)
    cidx = np.clip(kcol[None, :] - qc[:, None] + NA_WIN_COLS - 1, 0, 2 * NA_WIN_COLS - 2)
    delta = np.arange(wr)
    j = np.arange(wr)
    ridx = j[None, :] - delta[:, None] + NA_WIN_ROWS - 1
    t = rpb[:, ridx][:, :, :, cidx]
    t = jnp.where(jnp.asarray(ok)[None, None, None], t.astype(F32), -jnp.inf)
    t = t.transpose(0, 1, 3, 2, 4)
    return t.reshape(rpb.shape[0], wr, GRID_W, wr * GRID_W)


def _na_attention(z, bias, seq, ctx_len, batch):
    n_pairs = NA_HEADS * NA_DH // LANES
    n_rows = seq // GRID_W
    cb = (batch * seq) // ctx_len
    return pl.pallas_call(
        functools.partial(_na_kernel, n_rows=n_rows),
        grid=(n_pairs, batch),
        in_specs=[pl.BlockSpec((seq, LANES), lambda p, b: (b, p)),
                  pl.BlockSpec((seq, LANES), lambda p, b: (b, n_pairs + p)),
                  pl.BlockSpec((seq, LANES), lambda p, b: (b, 2 * n_pairs + p)),
                  pl.BlockSpec((ctx_len, LANES), lambda p, b: (cb + b, n_pairs + p)),
                  pl.BlockSpec((ctx_len, LANES), lambda p, b: (cb + b, 2 * n_pairs + p)),
                  pl.BlockSpec((2,) + bias.shape[1:], lambda p, b: (p, 0, 0, 0))],
        out_specs=pl.BlockSpec((seq, LANES), lambda p, b: (b, p)),
        out_shape=jax.ShapeDtypeStruct((batch * seq, NA_HEADS * NA_DH), BF16),
        compiler_params=_params(("arbitrary", "arbitrary")),
        name="na_attention",
    )(z, z, z, z, z, bias)


def _store_token_tiles(ref, val, row0=0):
    n, d = val.shape
    nc = d // LANES
    for k in range(nc):
        ref[pl.ds(row0 * nc + k, n, stride=nc), :] = val[:, k * LANES:(k + 1) * LANES]


def _load_token_tiles(ref, row0, n, nc, lead=None):
    chunks = []
    for k in range(nc):
        idx = pl.ds(row0 * nc + k, n, stride=nc)
        chunks.append(ref[idx, :] if lead is None else ref[lead, idx, :])
    return jnp.concatenate(chunks, axis=-1)


def _moe_pre_kernel(x_ref, mod_ref, g_ref, rwh_ref, rwl_ref, wsg_ref, wsu_ref, wsd_ref,
                    h_ref, sc_ref, y0_ref):
    h = _prenorm_mod(x_ref[...], g_ref[...], mod_ref, 3, 4)
    _store_token_tiles(h_ref, h)
    sc_ref[...] = _sigmoid(_dot3(h, rwh_ref[...], rwl_ref[...]))
    hb = h.astype(BF16)
    gate = jnp.dot(hb, wsg_ref[...], preferred_element_type=F32)
    up = jnp.dot(hb, wsu_ref[...], preferred_element_type=F32)
    hid = (gate * _sigmoid(gate) * up).astype(BF16)
    y0_ref[...] = jnp.dot(hid, wsd_ref[...], preferred_element_type=F32)


def _moe_pre(xt, mod, g, rw_hi, rw_lo, wsg, wsu, wsd, n_tok, seq, batch):
    d = xt.shape[1]
    e = rw_hi.shape[1]
    full = lambda a: pl.BlockSpec(a.shape, lambda i: (0,) * a.ndim)
    return pl.pallas_call(
        _moe_pre_kernel,
        grid=(n_tok // ROW_TILE,),
        in_specs=[pl.BlockSpec((ROW_TILE, d), lambda i: (i, 0)),
                  pl.BlockSpec((1, ADA_CHUNKS, d), _mod_index(ROW_TILE, seq, batch)),
                  pl.BlockSpec((1, d), lambda i: (0, 0)),
                  full(rw_hi), full(rw_lo), full(wsg), full(wsu), full(wsd)],
        out_specs=[pl.BlockSpec((ROW_TILE * (d // LANES), LANES), lambda i: (i, 0)),
                   pl.BlockSpec((ROW_TILE, e), lambda i: (i, 0)),
                   pl.BlockSpec((ROW_TILE, d), lambda i: (i, 0))],
        out_shape=[jax.ShapeDtypeStruct((n_tok * (d // LANES), LANES), F32),
                   jax.ShapeDtypeStruct((n_tok, e), F32),
                   jax.ShapeDtypeStruct((n_tok, d), F32)],
        compiler_params=_params(("arbitrary",)),
        name="moe_pre",
    )(xt, mod, g.reshape(1, d), rw_hi, rw_lo, wsg, wsu, wsd)


def _tile_gather_copy(src_hbm, idx_smem, dst_vmem, sem, i, nc):
    tok = idx_smem[i // LANES, i % LANES]
    src = src_hbm.at[pl.ds(pl.multiple_of(tok * nc, nc), nc), :]
    dst = dst_vmem.at[pl.ds(pl.multiple_of(i * nc, nc), nc), :]
    return pltpu.make_async_copy(src, dst, sem)


def _moe_expert_kernel(be_ref, nused_ref, tok_hbm, h_hbm, wg_ref, wu_ref, wd_ref, o_ref,
                       idx_smem, xbuf, idx_sem, row_sem):
    i = pl.program_id(0)
    nused = nused_ref[0]
    slot = i % 2
    nxt = 1 - slot
    nc = xbuf.shape[1] // MOE_RB

    def idx_copy(blk, s):
        return pltpu.make_async_copy(tok_hbm.at[blk], idx_smem.at[s], idx_sem.at[s])

    def issue_rows(s):
        def body(r, carry):
            _tile_gather_copy(h_hbm, idx_smem.at[s], xbuf.at[s], row_sem.at[s], r, nc).start()
            return carry
        lax.fori_loop(0, MOE_RB, body, 0, unroll=8)

    def wait_rows(s):
        pltpu.make_async_copy(h_hbm.at[pl.ds(0, MOE_RB * nc), :], xbuf.at[s], row_sem.at[s]).wait()

    @pl.when(i == 0)
    def _():
        idx_copy(0, 0).start()
        idx_copy(0, 0).wait()
        issue_rows(0)

        @pl.when(nused > 1)
        def _():
            idx_copy(1, 1).start()

    @pl.when(i + 1 < nused)
    def _():
        idx_copy(i + 1, nxt).wait()
        issue_rows(nxt)

    @pl.when(i < nused)
    def _():
        wait_rows(slot)

        @pl.when(i + 2 < nused)
        def _():
            idx_copy(i + 2, slot).start()

        xb = _load_token_tiles(xbuf, 0, MOE_RB, nc, lead=slot).astype(BF16)
        gate = jnp.dot(xb, wg_ref[...].astype(BF16), preferred_element_type=F32)
        up = jnp.dot(xb, wu_ref[...].astype(BF16), preferred_element_type=F32)
        hid = (gate * _sigmoid(gate) * up).astype(BF16)
        _store_token_tiles(o_ref, jnp.dot(hid, wd_ref[...].astype(BF16), preferred_element_type=F32))

    @pl.when(i >= nused)
    def _():
        o_ref[...] = jnp.zeros_like(o_ref)


def _index_tiles(idx):
    n, m = idx.shape
    return jnp.pad(idx, ((0, 0), (0, 8 * LANES - m))).reshape(n, 8, LANES)


def _moe_experts(block_e, nused, slot_tok, h, wg, wu, wd):
    nb = slot_tok.shape[0]
    d = wg.shape[1]
    nc = d // LANES
    de = wg.shape[2]
    grid_spec = pltpu.PrefetchScalarGridSpec(
        num_scalar_prefetch=2,
        grid=(nb,),
        in_specs=[pl.BlockSpec(memory_space=pl.ANY),
                  pl.BlockSpec(memory_space=pl.ANY),
                  pl.BlockSpec((None, d, de), lambda i, be, nu: (be[i], 0, 0)),
                  pl.BlockSpec((None, d, de), lambda i, be, nu: (be[i], 0, 0)),
                  pl.BlockSpec((None, de, d), lambda i, be, nu: (be[i], 0, 0))],
        out_specs=pl.BlockSpec((MOE_RB * nc, LANES), lambda i, be, nu: (i, 0)),
        scratch_shapes=[pltpu.SMEM((2, 8, LANES), jnp.int32),
                        pltpu.VMEM((2, MOE_RB * nc, LANES), F32),
                        pltpu.SemaphoreType.DMA((2,)),
                        pltpu.SemaphoreType.DMA((2,))],
    )
    return pl.pallas_call(
        _moe_expert_kernel,
        grid_spec=grid_spec,
        out_shape=jax.ShapeDtypeStruct((nb * MOE_RB * nc, LANES), F32),
        compiler_params=_params(("arbitrary",)),
        name="moe_experts",
    )(block_e, nused, _index_tiles(slot_tok), h, wg, wu, wd)


def _moe_combine_kernel(pos_hbm, og_hbm, gates_ref, y0_ref, x_ref, mod_ref, g_ref, o_ref,
                        idx_smem, buf, idx_sem, row_sem, *, n_tiles):
    i = pl.program_id(0)
    slot = i % 2
    nxt = 1 - slot
    n_rows = TOP_K * COMB_TT
    nc = buf.shape[1] // n_rows

    def idx_copy(t, s):
        return pltpu.make_async_copy(pos_hbm.at[t], idx_smem.at[s], idx_sem.at[s])

    def issue_rows(s):
        def body(r, carry):
            _tile_gather_copy(og_hbm, idx_smem.at[s], buf.at[s], row_sem.at[s], r, nc).start()
            return carry
        lax.fori_loop(0, n_rows, body, 0, unroll=8)

    def wait_rows(s):
        pltpu.make_async_copy(og_hbm.at[pl.ds(0, n_rows * nc), :], buf.at[s], row_sem.at[s]).wait()

    @pl.when(i == 0)
    def _():
        idx_copy(0, 0).start()
        idx_copy(0, 0).wait()
        issue_rows(0)

        @pl.when(n_tiles > 1)
        def _():
            idx_copy(1, 1).start()

    @pl.when(i + 1 < n_tiles)
    def _():
        idx_copy(i + 1, nxt).wait()
        issue_rows(nxt)

    wait_rows(slot)

    @pl.when(i + 2 < n_tiles)
    def _():
        idx_copy(i + 2, slot).start()

    gates = gates_ref[...]
    y = y0_ref[...]
    for k in range(TOP_K):
        y = y + gates[:, k:k + 1] * _load_token_tiles(buf, k * COMB_TT, COMB_TT, nc, lead=slot)
    ms = jnp.mean(y * y, axis=-1, keepdims=True)
    yn = y * lax.rsqrt(ms + RMS_EPS) * g_ref[...]
    o_ref[...] = x_ref[...] + mod_ref[0, 5:6, :] * yn


def _moe_combine(pos_t, og, gates, y0, xt, mod, g, n_tok, seq, batch):
    d = xt.shape[1]
    n_tiles = n_tok // COMB_TT
    row_spec = lambda w: pl.BlockSpec((COMB_TT, w), lambda i: (i, 0))
    return pl.pallas_call(
        functools.partial(_moe_combine_kernel, n_tiles=n_tiles),
        grid=(n_tiles,),
        in_specs=[pl.BlockSpec(memory_space=pl.ANY),
                  pl.BlockSpec(memory_space=pl.ANY),
                  row_spec(TOP_K), row_spec(d), row_spec(d),
                  pl.BlockSpec((1, ADA_CHUNKS, d), _mod_index(COMB_TT, seq, batch)),
                  pl.BlockSpec((1, d), lambda i: (0, 0))],
        out_specs=row_spec(d),
        out_shape=jax.ShapeDtypeStruct(xt.shape, F32),
        input_output_aliases={4: 0},
        scratch_shapes=[pltpu.SMEM((2, 8, LANES), jnp.int32),
                        pltpu.VMEM((2, TOP_K * COMB_TT * (d // LANES), LANES), F32),
                        pltpu.SemaphoreType.DMA((2,)),
                        pltpu.SemaphoreType.DMA((2,))],
        compiler_params=_params(("arbitrary",)),
        name="moe_combine",
    )(_index_tiles(pos_t), og, gates, y0, xt, mod, g.reshape(1, d))


def _route(scores, router_b):
    n, e = scores.shape
    biased = scores + router_b.astype(F32)
    grp = biased.reshape(n, N_GROUPS, e // N_GROUPS)
    grp_score = jnp.sum(lax.top_k(grp, 2)[0], axis=-1)
    _, grp_idx = lax.top_k(grp_score, TOPK_GROUPS)
    grp_keep = jnp.sum(jax.nn.one_hot(grp_idx, N_GROUPS, dtype=F32), axis=1) > 0
    masked = jnp.where(grp_keep[:, :, None], grp, -jnp.inf).reshape(n, e)
    _, e_idx = lax.top_k(masked, TOP_K)
    gates = jnp.take_along_axis(scores, e_idx, axis=-1)
    gates = gates / jnp.sum(gates, axis=-1, keepdims=True) * ROUTE_SCALE

    a = n * TOP_K
    flat_e = e_idx.reshape(-1)
    flat_tok = jnp.repeat(jnp.arange(n, dtype=jnp.int32), TOP_K)
    order = jnp.argsort(flat_e)
    se, stok = flat_e[order], flat_tok[order]
    counts = jnp.bincount(flat_e, length=e)
    starts = jnp.cumsum(counts) - counts
    padded = (counts + MOE_RB - 1) // MOE_RB * MOE_RB
    pend = jnp.cumsum(padded)
    pstart = pend - padded
    pos_sorted = (pstart[se] + jnp.arange(a, dtype=jnp.int32) - starts[se]).astype(jnp.int32)
    nb = -(-(a + e * (MOE_RB - 1)) // MOE_RB)
    slot_tok = jnp.zeros((nb * MOE_RB,), jnp.int32).at[pos_sorted].set(stok)
    pos = jnp.zeros((a,), jnp.int32).at[order].set(pos_sorted).reshape(n, TOP_K)
    block_e = jnp.minimum(jnp.searchsorted(pend, jnp.arange(nb, dtype=jnp.int32) * MOE_RB, side='right'),
                          e - 1).astype(jnp.int32)
    nused = (pend[-1] // MOE_RB).astype(jnp.int32).reshape(1)
    pos_t = pos.reshape(n // COMB_TT, COMB_TT, TOP_K).transpose(0, 2, 1).reshape(n // COMB_TT, TOP_K * COMB_TT)
    return gates, slot_tok.reshape(nb, MOE_RB), pos_t, block_e, nused


def _moe_layer(xt, mod, g_pre, g_post, router_w, router_b, w_gate, w_up, w_down, ws_gate, ws_up, ws_down,
               n_tok, seq, batch):
    rw_hi, rw_lo = _split(router_w.astype(F32))
    h, scores, y0 = _moe_pre(xt, mod, g_pre, rw_hi, rw_lo, ws_gate.astype(BF16), ws_up.astype(BF16),
                             ws_down.astype(BF16), n_tok, seq, batch)
    gates, slot_tok2, pos_t, block_e, nused = _route(scores, router_b)
    og = _moe_experts(block_e, nused, slot_tok2, h, w_gate, w_up, w_down)
    return _moe_combine(pos_t, og, gates, y0, xt, mod, g_post, n_tok, seq, batch)


def kernel(x, c, ctx, c_ctx, ada_w, ada_b, g_pre_mix, g_post_mix, g_pre_ffn, g_post_ffn, ab_w_in, ab_w_out, gla_dw_f, gla_db_f, gla_dw_b, gla_db_b, gla_norm_g, conv_w, conv_b, conv_ln_g, conv_ln_b, na_w_in, na_w_out, na_rpb, moe_router_w, moe_router_b, moe_w_gate, moe_w_up, moe_w_down, moe_ws_gate, moe_ws_up, moe_ws_down):
    batch, seq, d = x.shape
    ctx_len = ctx.shape[1]
    depth = ada_w.shape[0]
    n_x = batch * seq
    n_all = n_x + batch * ctx_len
    assert seq % ROW_TILE == 0 and (batch * ctx_len) % ROW_TILE == 0 and n_x % ctx_len == 0

    xt = jnp.concatenate([x.reshape(n_x, d), ctx.reshape(batch * ctx_len, d)], axis=0)

    cc = jnp.zeros((ADA_ROWS, d), F32).at[:batch].set(c).at[batch].set(c_ctx)
    mods = _ada_mod(cc, ada_w, ada_b)[:, :batch + 1].reshape(depth, batch + 1, ADA_CHUNKS, d)

    for layer in range(depth):
        last = layer == depth - 1
        i = layer // 2
        mod = mods[layer]
        if layer % 2 == 0:
            w_in = ab_w_in[i]
            cuts = np.cumsum([0, GLA_QK, GLA_QK, GLA_V, GLA_V, GLA_LOWRANK, GLA_LOWRANK, CONV_CH, CONV_CH])
            seg = lambda j: w_in[:, cuts[j]:cuts[j + 1]]
            lr_pad = jnp.zeros((d, LANES - 2 * GLA_LOWRANK), w_in.dtype)
            w_cat = jnp.concatenate([seg(0), seg(1), seg(2), seg(3), seg(6), seg(7), seg(4), seg(5), lr_pad],
                                    axis=1).astype(BF16)
            qk, v, g, glu, lr = _prenorm_proj(xt, mod, g_pre_mix[layer], w_cat,
                                              (2 * GLA_QK, GLA_V, GLA_V, 2 * CONV_CH, LANES), 0, 1, seq, batch)
            dwcat = jnp.zeros((LANES, 2 * GLA_QK), F32)
            dwcat = dwcat.at[:GLA_LOWRANK, :GLA_QK].set(gla_dw_f[i])
            dwcat = dwcat.at[GLA_LOWRANK:2 * GLA_LOWRANK, GLA_QK:].set(gla_dw_b[i]).astype(BF16)
            dbcat = jnp.concatenate([gla_db_f[i], gla_db_b[i]]).reshape(1, 2 * GLA_QK).astype(F32)
            a_out = _gla(qk, v, g, lr, dwcat, dbcat, gla_norm_g[i].reshape(1, GLA_V).astype(F32),
                         seq, ctx_len, batch)
            cw = conv_w[i].astype(F32)
            cvec = lambda t: t.reshape(1, CONV_CH).astype(F32)
            cv_x = _conv_module(glu, cw, cvec(conv_b[i]), cvec(conv_ln_g[i]), cvec(conv_ln_b[i]), seq, 0, batch)
            cv_c = _conv_module(glu, cw, cvec(conv_b[i]), cvec(conv_ln_g[i]), cvec(conv_ln_b[i]), ctx_len,
                                n_x // ctx_len, batch)
            cv = jnp.concatenate([cv_x, cv_c], axis=0)
            n_rows = n_x if last else n_all
            xt = _outproj([a_out, cv], ab_w_out[i].astype(BF16), xt, mod, g_post_mix[layer], 2, n_rows, seq, batch)
        else:
            (z,) = _prenorm_proj(xt, mod, g_pre_mix[layer], na_w_in[i].astype(BF16),
                                 (3 * NA_HEADS * NA_DH,), 0, 1, seq, batch)
            bias = _na_bias_table(na_rpb[i], seq // GRID_W)
            o = _na_attention(z, bias, seq, ctx_len, batch)
            if last:
                xt = _outproj([o], na_w_out[i].astype(BF16), xt, mod, g_post_mix[layer], 2, n_x, seq, batch)
            else:
                raise NotImplementedError("context update after a neighbourhood-attention layer")
        n_tok = n_x if last else n_all
        xt = _moe_layer(xt, mod, g_pre_ffn[layer], g_post_ffn[layer], moe_router_w[layer], moe_router_b[layer],
                        moe_w_gate[layer], moe_w_up[layer], moe_w_down[layer], moe_ws_gate[layer],
                        moe_ws_up[layer], moe_ws_down[layer], n_tok, seq, batch)
    return xt[:n_x].reshape(batch, seq, d)
```

```python
import functools

import numpy as np
import jax
import jax.numpy as jnp
from jax import lax
from jax.experimental import pallas as pl
from jax.experimental.pallas import tpu as pltpu

F32 = jnp.float32
BF16 = jnp.bfloat16

GRID_W = 64
RMS_EPS = 1e-6
LN_EPS = 1e-5
ADA_CHUNKS = 6
GLA_HEADS = 4
GLA_DK = 64
GLA_DV = 128
GLA_LOWRANK = 16
GLA_GATE_NORM = 16.0
GLA_CHUNK = 64
ROPE_BASE = 10000.0
GLA_QK = GLA_HEADS * GLA_DK
GLA_V = GLA_HEADS * GLA_DV
CONV_CH = 512
CONV_WIDTH = 31
NA_HEADS = 16
NA_DH = 64
NA_WIN_ROWS = 8
NA_WIN_COLS = 16
N_EXPERTS = 256
TOP_K = 8
N_GROUPS = 8
TOPK_GROUPS = 4
ROUTE_SCALE = 2.5

LANES = 128
ROW_TILE = 512
ADA_ROWS = 24
ADA_TN = 1536
CONV_RC = 32
CONV_HALO = 16
MOE_RB = 256
DISP_TT = 128
VMEM_LIMIT = 56 * 1024 * 1024

_NT = (((1,), (1,)), ((), ()))
_TN = (((0,), (0,)), ((), ()))


def _params(sem, **kw):
    return pltpu.CompilerParams(dimension_semantics=sem, vmem_limit_bytes=VMEM_LIMIT, **kw)


def _sigmoid(x):
    return 1.0 / (1.0 + jnp.exp(-x))


def _split(a):
    hi = a.astype(BF16)
    lo = (a - hi.astype(F32)).astype(BF16)
    return hi, lo


def _dot3(a, b_hi, b_lo, dims=(((1,), (0,)), ((), ()))):
    a_hi, a_lo = _split(a)
    d = lambda x, y: lax.dot_general(x, y, dims, preferred_element_type=F32)
    return d(a_hi, b_hi) + (d(a_hi, b_lo) + d(a_lo, b_hi))


def _prenorm_mod(x, g, mod_ref, shift_i, scale_i):
    ms = jnp.mean(x * x, axis=-1, keepdims=True)
    h = x * lax.rsqrt(ms + RMS_EPS) * g
    return h * (1.0 + mod_ref[0, scale_i:scale_i + 1, :]) + mod_ref[0, shift_i:shift_i + 1, :]


def _mod_index(tile_rows, seq, batch):
    per_batch = seq // tile_rows
    return lambda i, *_: (jnp.minimum(i // per_batch, batch), 0, 0)


def _ada_kernel(c_ref, w_ref, b_ref, o_ref):
    c = c_ref[...]
    s = c * _sigmoid(c)
    w_hi, w_lo = _split(w_ref[...])
    o_ref[...] = _dot3(s, w_hi, w_lo) + b_ref[...]


def _ada_mod(cc, ada_w, ada_b):
    depth, d, n = ada_w.shape
    return pl.pallas_call(
        _ada_kernel,
        grid=(depth, n // ADA_TN),
        in_specs=[pl.BlockSpec((ADA_ROWS, d), lambda l, j: (0, 0)),
                  pl.BlockSpec((None, d, ADA_TN), lambda l, j: (l, 0, j)),
                  pl.BlockSpec((None, 1, ADA_TN), lambda l, j: (l, 0, j))],
        out_specs=pl.BlockSpec((None, ADA_ROWS, ADA_TN), lambda l, j: (l, 0, j)),
        out_shape=jax.ShapeDtypeStruct((depth, ADA_ROWS, n), F32),
        compiler_params=_params(("arbitrary", "arbitrary")),
        name="ada_mod",
    )(cc, ada_w, ada_b.reshape(depth, 1, n))


def _prenorm_proj_kernel(x_ref, mod_ref, g_ref, w_ref, *o_refs, shift_i, scale_i):
    h = _prenorm_mod(x_ref[...], g_ref[...], mod_ref, shift_i, scale_i).astype(BF16)
    off = 0
    for o_ref in o_refs:
        n = o_ref.shape[-1]
        o_ref[...] = jnp.dot(h, w_ref[:, off:off + n], preferred_element_type=F32).astype(o_ref.dtype)
        off += n


def _prenorm_proj(xt, mod, g, w, splits, shift_i, scale_i, seq, batch):
    rows, d = xt.shape
    n = w.shape[1]
    return pl.pallas_call(
        functools.partial(_prenorm_proj_kernel, shift_i=shift_i, scale_i=scale_i),
        grid=(rows // ROW_TILE,),
        in_specs=[pl.BlockSpec((ROW_TILE, d), lambda i: (i, 0)),
                  pl.BlockSpec((1, ADA_CHUNKS, d), _mod_index(ROW_TILE, seq, batch)),
                  pl.BlockSpec((1, d), lambda i: (0, 0)),
                  pl.BlockSpec((d, n), lambda i: (0, 0))],
        out_specs=[pl.BlockSpec((ROW_TILE, s), lambda i: (i, 0)) for s in splits],
        out_shape=[jax.ShapeDtypeStruct((rows, s), BF16) for s in splits],
        compiler_params=_params(("arbitrary",)),
        name="prenorm_proj",
    )(xt, mod, g.reshape(1, d), w)


def _gla_kernel(qkx_ref, vx_ref, gx_ref, lrx_ref, qkc_ref, vc_ref, gc_ref, lrc_ref,
                dw_ref, db_ref, cos_ref, sin_ref, gg_ref, ox_ref, oc_ref,
                sf_ref, sb_ref, accx_ref, accc_ref, *, n_x, n_c):
    L = GLA_CHUNK
    row = lax.broadcasted_iota(jnp.int32, (L, L), 0)
    col = lax.broadcasted_iota(jnp.int32, (L, L), 1)
    lower = row >= col
    upper = col >= row
    tri_f = jnp.where(lower, 1.0, 0.0).astype(BF16)
    tri_b = jnp.where(upper, 1.0, 0.0).astype(BF16)
    lane = lax.broadcasted_iota(jnp.int32, (1, GLA_QK), 1)
    head_of_lane = lane // GLA_DK
    first_half = (lane % (GLA_DK // 2)) < (GLA_DK // 4)
    bd_mask = (lax.broadcasted_iota(jnp.int32, (GLA_V, GLA_QK), 0) // GLA_DV
               == lax.broadcasted_iota(jnp.int32, (GLA_V, GLA_QK), 1) // GLA_DK)

    def swap_pairs(t):
        nf = GLA_DK // 4
        return jnp.where(first_half, pltpu.roll(t, GLA_QK - nf, 1), pltpu.roll(t, nf, 1))

    def chunk(fwd, c, qk_ref, v_ref, lr_ref, latent):
        rows = pl.ds(pl.multiple_of(c * L, L), L)
        q = qk_ref[rows, 0:GLA_QK].astype(F32) * (GLA_DK ** -0.5)
        k = qk_ref[rows, GLA_QK:2 * GLA_QK].astype(F32)
        if latent:
            cs = cos_ref[rows, :]
            sn = sin_ref[rows, :]
            q = q * cs + swap_pairs(q) * sn
            k = k * cs + swap_pairs(k) * sn
        dsl = slice(0, GLA_QK) if fwd else slice(GLA_QK, 2 * GLA_QK)
        pre = jnp.dot(lr_ref[rows, :], dw_ref[:, dsl], preferred_element_type=F32) + db_ref[:, dsl]
        logd = (jnp.minimum(pre, 0.0) - jnp.log(1.0 + jnp.exp(-jnp.abs(pre)))) * (1.0 / GLA_GATE_NORM)
        d_hi, d_lo = _split(logd)
        tri = tri_f if fwd else tri_b
        b = (jnp.dot(tri, d_hi, preferred_element_type=F32)
             + jnp.dot(tri, d_lo, preferred_element_type=F32))
        mid = L // 2 - 1 if fwd else L // 2
        end = L - 1 if fwd else 0
        b_mid = b[mid:mid + 1, :]
        b_end = b[end:end + 1, :]
        qe = q * jnp.exp(b - b_mid)
        ke = (k * jnp.exp(b_mid - b)).astype(BF16)
        q_in = (q * jnp.exp(b)).astype(BF16)
        k_out = (k * jnp.exp(b_end - b)).astype(BF16)
        dec = jnp.exp(b_end)
        vv = v_ref[rows, :]
        mask = lower if fwd else upper
        outs = []
        for h in range(GLA_HEADS):
            qm = jnp.where(head_of_lane == h, qe, 0.0).astype(BF16)
            a = lax.dot_general(qm, ke, _NT, preferred_element_type=F32)
            a = jnp.where(mask, a, 0.0).astype(BF16)
            outs.append(jnp.dot(a, vv[:, h * GLA_DV:(h + 1) * GLA_DV], preferred_element_type=F32))
        o_intra = jnp.concatenate(outs, axis=-1)
        s_ref = sf_ref if fwd else sb_ref
        st = s_ref[...]
        o_inter = lax.dot_general(q_in, st.astype(BF16), _NT, preferred_element_type=F32)
        upd = lax.dot_general(vv, k_out, _TN, preferred_element_type=F32)
        s_ref[...] = jnp.where(bd_mask, st * dec + upd, 0.0)
        return o_intra + o_inter

    def finish(o, g_ref, c):
        rows = pl.ds(pl.multiple_of(c * L, L), L)
        g = g_ref[rows, :].astype(F32)
        parts = []
        for h in range(GLA_HEADS):
            oh = o[:, h * GLA_DV:(h + 1) * GLA_DV]
            ms = jnp.mean(oh * oh, axis=-1, keepdims=True)
            parts.append(oh * lax.rsqrt(ms + RMS_EPS))
        on = jnp.concatenate(parts, axis=-1) * gg_ref[...]
        return (on * (g * _sigmoid(g))).astype(BF16)

    def run_seq(n, qk_ref, v_ref, g_ref, lr_ref, acc_ref, o_ref, latent):
        half = n // 2

        def rows_of(c):
            return pl.ds(pl.multiple_of(c * L, L), L)

        def first(i, carry):
            cf, cb = i, n - 1 - i
            acc_ref[rows_of(cf), :] = chunk(True, cf, qk_ref, v_ref, lr_ref, latent)
            acc_ref[rows_of(cb), :] = chunk(False, cb, qk_ref, v_ref, lr_ref, latent)
            return carry

        def second(i, carry):
            cf, cb = i, n - 1 - i
            of = acc_ref[rows_of(cf), :] + chunk(True, cf, qk_ref, v_ref, lr_ref, latent)
            o_ref[rows_of(cf), :] = finish(of, g_ref, cf)
            ob = acc_ref[rows_of(cb), :] + chunk(False, cb, qk_ref, v_ref, lr_ref, latent)
            o_ref[rows_of(cb), :] = finish(ob, g_ref, cb)
            return carry

        lax.fori_loop(0, half, first, 0)
        lax.fori_loop(half, n, second, 0)

    sf_ref[...] = jnp.zeros_like(sf_ref)
    sb_ref[...] = jnp.zeros_like(sb_ref)
    run_seq(n_c, qkc_ref, vc_ref, gc_ref, lrc_ref, accc_ref, oc_ref, False)
    run_seq(n_x, qkx_ref, vx_ref, gx_ref, lrx_ref, accx_ref, ox_ref, True)


def _rope_tables(seq):
    half = GLA_DK // 2
    nf = half // 2
    inv = ROPE_BASE ** (-jnp.arange(nf, dtype=F32) / nf)
    t = jnp.arange(seq)
    rows = (t // GRID_W).astype(F32)
    cols = (t % GRID_W).astype(F32)
    ar = rows[:, None] * inv[None, :]
    ac = cols[:, None] * inv[None, :]
    cos_h = jnp.concatenate([jnp.cos(ar), jnp.cos(ar), jnp.cos(ac), jnp.cos(ac)], axis=-1)
    sin_h = jnp.concatenate([-jnp.sin(ar), jnp.sin(ar), -jnp.sin(ac), jnp.sin(ac)], axis=-1)
    return jnp.tile(cos_h, (1, GLA_HEADS)), jnp.tile(sin_h, (1, GLA_HEADS))


def _gla(qk, v, g, lr, dwcat, dbcat, gla_g, seq, ctx_len, batch):
    rows = qk.shape[0]
    cos_t, sin_t = _rope_tables(seq)
    cb = (batch * seq) // ctx_len
    xs = lambda w: pl.BlockSpec((seq, w), lambda b: (b, 0))
    cs = lambda w: pl.BlockSpec((ctx_len, w), lambda b: (cb + b, 0))
    full = lambda a: pl.BlockSpec(a.shape, lambda b: (0,) * a.ndim)
    out_x, out_c = pl.pallas_call(
        functools.partial(_gla_kernel, n_x=seq // GLA_CHUNK, n_c=ctx_len // GLA_CHUNK),
        grid=(batch,),
        in_specs=[xs(2 * GLA_QK), xs(GLA_V), xs(GLA_V), xs(LANES),
                  cs(2 * GLA_QK), cs(GLA_V), cs(GLA_V), cs(LANES),
                  full(dwcat), full(dbcat), full(cos_t), full(sin_t), full(gla_g)],
        out_specs=[pl.BlockSpec((seq, GLA_V), lambda b: (b, 0)),
                   pl.BlockSpec((ctx_len, GLA_V), lambda b: (b, 0))],
        out_shape=[jax.ShapeDtypeStruct((batch * seq, GLA_V), BF16),
                   jax.ShapeDtypeStruct((batch * ctx_len, GLA_V), BF16)],
        scratch_shapes=[pltpu.VMEM((GLA_V, GLA_QK), F32), pltpu.VMEM((GLA_V, GLA_QK), F32),
                        pltpu.VMEM((seq, GLA_V), F32), pltpu.VMEM((ctx_len, GLA_V), F32)],
        compiler_params=_params(("arbitrary",)),
        name="gla",
    )(qk, v, g, lr, qk, v, g, lr, dwcat, dbcat, cos_t, sin_t, gla_g)
    del rows
    return jnp.concatenate([out_x, out_c], axis=0)


def _conv_kernel(glu_ref, w_ref, cb_ref, lg_ref, lb_ref, o_ref, pad_ref, win_ref, *, t_len):
    ch = CONV_CH
    zeros = jnp.zeros((CONV_HALO, ch), F32)
    pad_ref[0:CONV_HALO, :] = zeros
    pad_ref[CONV_HALO + t_len:2 * CONV_HALO + t_len, :] = zeros
    fill = 128

    def fill_body(i, carry):
        r = pl.multiple_of(i * fill, fill)
        a = glu_ref[pl.ds(r, fill), 0:ch].astype(F32)
        b = glu_ref[pl.ds(r, fill), ch:2 * ch].astype(F32)
        pad_ref[pl.ds(CONV_HALO + r, fill), :] = a * _sigmoid(b)
        return carry

    lax.fori_loop(0, t_len // fill, fill_body, 0)
    first_tap = CONV_HALO - CONV_WIDTH // 2

    def body(i, carry):
        base = pl.multiple_of(i * CONV_RC, CONV_RC)
        win_ref[...] = pad_ref[pl.ds(base, CONV_RC + 2 * CONV_HALO), :]
        acc = jnp.zeros((CONV_RC, ch), F32)
        for j in range(CONV_WIDTH):
            acc = acc + w_ref[j:j + 1, :] * win_ref[first_tap + j:first_tap + j + CONV_RC, :]
        acc = acc + cb_ref[...]
        mu = jnp.mean(acc, axis=-1, keepdims=True)
        cen = acc - mu
        var = jnp.mean(cen * cen, axis=-1, keepdims=True)
        y = cen * lax.rsqrt(var + LN_EPS) * lg_ref[...] + lb_ref[...]
        o_ref[pl.ds(base, CONV_RC), :] = (y * _sigmoid(y)).astype(BF16)
        return carry

    lax.fori_loop(0, t_len // CONV_RC, body, 0)


def _conv_module(glu, conv_w, conv_b, ln_g, ln_b, t_len, first_block, batch):
    full = lambda a: pl.BlockSpec(a.shape, lambda b: (0,) * a.ndim)
    return pl.pallas_call(
        functools.partial(_conv_kernel, t_len=t_len),
        grid=(batch,),
        in_specs=[pl.BlockSpec((t_len, 2 * CONV_CH), lambda b: (first_block + b, 0)),
                  full(conv_w), full(conv_b), full(ln_g), full(ln_b)],
        out_specs=pl.BlockSpec((t_len, CONV_CH), lambda b: (b, 0)),
        out_shape=jax.ShapeDtypeStruct((batch * t_len, CONV_CH), BF16),
        scratch_shapes=[pltpu.VMEM((t_len + 2 * CONV_HALO, CONV_CH), F32),
                        pltpu.VMEM((CONV_RC + 2 * CONV_HALO, CONV_CH), F32)],
        compiler_params=_params(("arbitrary",)),
        name="conv_module",
    )(glu, conv_w, conv_b, ln_g, ln_b)


def _outproj_kernel(*refs, n_lhs, gate_i):
    lhs_refs = refs[:n_lhs]
    w_ref, x_ref, mod_ref, g_ref, o_ref = refs[n_lhs:]
    off = 0
    y = None
    for l_ref in lhs_refs:
        kk = l_ref.shape[-1]
        t = jnp.dot(l_ref[...], w_ref[off:off + kk, :], preferred_element_type=F32)
        y = t if y is None else y + t
        off += kk
    ms = jnp.mean(y * y, axis=-1, keepdims=True)
    yn = y * lax.rsqrt(ms + RMS_EPS) * g_ref[...]
    o_ref[...] = x_ref[...] + mod_ref[0, gate_i:gate_i + 1, :] * yn


def _outproj(lhs_list, w, xt, mod, g, gate_i, n_rows, seq, batch):
    d = xt.shape[1]
    n_lhs = len(lhs_list)
    in_specs = [pl.BlockSpec((ROW_TILE, a.shape[1]), lambda i: (i, 0)) for a in lhs_list]
    in_specs += [pl.BlockSpec(w.shape, lambda i: (0, 0)),
                 pl.BlockSpec((ROW_TILE, d), lambda i: (i, 0)),
                 pl.BlockSpec((1, ADA_CHUNKS, d), _mod_index(ROW_TILE, seq, batch)),
                 pl.BlockSpec((1, d), lambda i: (0, 0))]
    return pl.pallas_call(
        functools.partial(_outproj_kernel, n_lhs=n_lhs, gate_i=gate_i),
        grid=(n_rows // ROW_TILE,),
        in_specs=in_specs,
        out_specs=pl.BlockSpec((ROW_TILE, d), lambda i: (i, 0)),
        out_shape=jax.ShapeDtypeStruct(xt.shape, F32),
        input_output_aliases={n_lhs + 1: 0},
        compiler_params=_params(("arbitrary",)),
        name="outproj",
    )(*lhs_list, w, xt, mod, g.reshape(1, d))


def _na_kernel(q_ref, k_ref, v_ref, kc_ref, vc_ref, bias_ref, o_ref, *, n_rows):
    wr = NA_WIN_ROWS
    lane = lax.broadcasted_iota(jnp.int32, (1, LANES), 1)
    head0 = lane < NA_DH
    kc = kc_ref[...]
    vc = vc_ref[...]

    def body(r, carry):
        r0 = jnp.clip(r - wr // 2, 0, n_rows - wr)
        delta = r - r0
        q2 = q_ref[pl.ds(pl.multiple_of(r * GRID_W, GRID_W), GRID_W), :] * (NA_DH ** -0.5)
        kw = k_ref[pl.ds(pl.multiple_of(r0 * GRID_W, GRID_W), wr * GRID_W), :]
        vw = v_ref[pl.ds(pl.multiple_of(r0 * GRID_W, GRID_W), wr * GRID_W), :]
        outs = []
        for hh in range(2):
            qm = jnp.where(head0 if hh == 0 else jnp.logical_not(head0), q2, jnp.zeros_like(q2))
            s_w = lax.dot_general(qm, kw, _NT, preferred_element_type=F32) + bias_ref[hh, delta]
            s_c = lax.dot_general(qm, kc, _NT, preferred_element_type=F32)
            m = jnp.maximum(jnp.max(s_w, axis=-1, keepdims=True), jnp.max(s_c, axis=-1, keepdims=True))
            p_w = jnp.exp(s_w - m)
            p_c = jnp.exp(s_c - m)
            den = jnp.sum(p_w, axis=-1, keepdims=True) + jnp.sum(p_c, axis=-1, keepdims=True)
            o = (jnp.dot(p_w.astype(BF16), vw, preferred_element_type=F32)
                 + jnp.dot(p_c.astype(BF16), vc, preferred_element_type=F32))
            outs.append(o / den)
        o_ref[pl.ds(pl.multiple_of(r * GRID_W, GRID_W), GRID_W), :] = jnp.where(head0, outs[0], outs[1]).astype(BF16)
        return carry

    lax.fori_loop(0, n_rows, body, 0)


def _na_bias_table(rpb, n_rows):
    wr = min(NA_WIN_ROWS, n_rows)
    qc = np.arange(GRID_W)
    cstart = np.clip(qc - NA_WIN_COLS // 2, 0, GRID_W - NA_WIN_COLS)
    kcol = np.arange(GRID_W)
    ok = (kcol[None, :] >= cstart[:, None]) & (kcol[None, :] < cstart[:, None] + NA_WIN_COLS)
    cidx = np.clip(kcol[None, :] - qc[:, None] + NA_WIN_COLS - 1, 0, 2 * NA_WIN_COLS - 2)
    delta = np.arange(wr)
    j = np.arange(wr)
    ridx = j[None, :] - delta[:, None] + NA_WIN_ROWS - 1
    t = rpb[:, ridx][:, :, :, cidx]
    t = jnp.where(jnp.asarray(ok)[None, None, None], t.astype(F32), -jnp.inf)
    t = t.transpose(0, 1, 3, 2, 4)
    return t.reshape(rpb.shape[0], wr, GRID_W, wr * GRID_W)


def _na_attention(z, bias, seq, ctx_len, batch):
    n_pairs = NA_HEADS * NA_DH // LANES
    n_rows = seq // GRID_W
    cb = (batch * seq) // ctx_len
    return pl.pallas_call(
        functools.partial(_na_kernel, n_rows=n_rows),
        grid=(n_pairs, batch),
        in_specs=[pl.BlockSpec((seq, LANES), lambda p, b: (b, p)),
                  pl.BlockSpec((seq, LANES), lambda p, b: (b, n_pairs + p)),
                  pl.BlockSpec((seq, LANES), lambda p, b: (b, 2 * n_pairs + p)),
                  pl.BlockSpec((ctx_len, LANES), lambda p, b: (cb + b, n_pairs + p)),
                  pl.BlockSpec((ctx_len, LANES), lambda p, b: (cb + b, 2 * n_pairs + p)),
                  pl.BlockSpec((2,) + bias.shape[1:], lambda p, b: (p, 0, 0, 0))],
        out_specs=pl.BlockSpec((seq, LANES), lambda p, b: (b, p)),
        out_shape=jax.ShapeDtypeStruct((batch * seq, NA_HEADS * NA_DH), BF16),
        compiler_params=_params(("arbitrary", "arbitrary")),
        name="na_attention",
    )(z, z, z, z, z, bias)


def _store_token_tiles(ref, val, row0=0):
    n, d = val.shape
    nc = d // LANES
    for k in range(nc):
        ref[pl.ds(row0 * nc + k, n, stride=nc), :] = val[:, k * LANES:(k + 1) * LANES]


def _load_token_tiles(ref, row0, n, nc, lead=None):
    chunks = []
    for k in range(nc):
        idx = pl.ds(row0 * nc + k, n, stride=nc)
        chunks.append(ref[idx, :] if lead is None else ref[lead, idx, :])
    return jnp.concatenate(chunks, axis=-1)


def _first_argmax(x, iota_f, n):
    m = jnp.max(x, axis=0, keepdims=True)
    idx = jnp.min(jnp.where(x == m, iota_f, float(n)), axis=0, keepdims=True)
    return m, idx


def _moe_pre_kernel(x_ref, mod_ref, g_ref, rwh_ref, rwl_ref, rb_ref, wsg_ref, wsu_ref, wsd_ref,
                    h_ref, y0_ref, eidx_ref, gates_ref, rank_ref, cnt_ref, base_ref):
    tm = x_ref.shape[0]
    n_e = rwh_ref.shape[0]

    @pl.when(pl.program_id(0) == 0)
    def _():
        base_ref[...] = jnp.zeros_like(base_ref)

    h = _prenorm_mod(x_ref[...], g_ref[...], mod_ref, 3, 4)
    _store_token_tiles(h_ref, h)
    hb = h.astype(BF16)
    gate = jnp.dot(hb, wsg_ref[...], preferred_element_type=F32)
    up = jnp.dot(hb, wsu_ref[...], preferred_element_type=F32)
    hid = (gate * _sigmoid(gate) * up).astype(BF16)
    y0_ref[...] = jnp.dot(hid, wsd_ref[...], preferred_element_type=F32)

    h_lo = (h - hb.astype(F32)).astype(BF16)
    d = lambda a, b: lax.dot_general(a, b, _NT, preferred_element_type=F32)
    logits = d(rwh_ref[...], hb) + (d(rwh_ref[...], h_lo) + d(rwl_ref[...], hb))
    scores = _sigmoid(logits)
    biased = scores + rb_ref[...]
    neg = -jnp.inf
    gsz = n_e // N_GROUPS
    io_g = lax.broadcasted_iota(jnp.int32, (gsz, tm), 0).astype(F32)
    grp_rows = []
    for gi in range(N_GROUPS):
        xg = biased[gi * gsz:(gi + 1) * gsz]
        m1, i1 = _first_argmax(xg, io_g, gsz)
        m2 = jnp.max(jnp.where(io_g == i1, neg, xg), axis=0, keepdims=True)
        grp_rows.append(m1 + m2)
    cur = jnp.concatenate(grp_rows, axis=0)
    io_n = lax.broadcasted_iota(jnp.int32, (N_GROUPS, tm), 0).astype(F32)
    keep = jnp.zeros((N_GROUPS, tm), F32)
    for _ in range(TOPK_GROUPS):
        _, ii = _first_argmax(cur, io_n, N_GROUPS)
        sel = io_n == ii
        keep = jnp.where(sel, 1.0, keep)
        cur = jnp.where(sel, neg, cur)
    cur = jnp.concatenate([jnp.where(keep[gi:gi + 1] > 0.0, biased[gi * gsz:(gi + 1) * gsz], neg)
                           for gi in range(N_GROUPS)], axis=0)
    io_e = lax.broadcasted_iota(jnp.int32, (n_e, tm), 0).astype(F32)
    idxs, gvals = [], []
    chosen = jnp.zeros((n_e, tm), F32)
    for _ in range(TOP_K):
        _, ii = _first_argmax(cur, io_e, n_e)
        sel = io_e == ii
        idxs.append(ii)
        gvals.append(jnp.sum(jnp.where(sel, scores, 0.0), axis=0, keepdims=True))
        chosen = jnp.where(sel, 1.0, chosen)
        cur = jnp.where(sel, neg, cur)
    gv = jnp.concatenate(gvals, axis=0)
    gates_ref[...] = gv / jnp.sum(gv, axis=0, keepdims=True) * ROUTE_SCALE
    eidx_ref[...] = jnp.concatenate(idxs, axis=0).astype(jnp.int32)

    earlier = (lax.broadcasted_iota(jnp.int32, (tm, tm), 0) < lax.broadcasted_iota(jnp.int32, (tm, tm), 1))
    before = jnp.dot(chosen.astype(BF16), jnp.where(earlier, 1.0, 0.0).astype(BF16), preferred_element_type=F32)
    before = before + base_ref[:, 0:1]
    ranks = [jnp.sum(jnp.where(io_e == ii, before, 0.0), axis=0, keepdims=True) for ii in idxs]
    rank_ref[...] = jnp.concatenate(ranks, axis=0).astype(jnp.int32)
    base_ref[...] = base_ref[...] + jnp.sum(chosen, axis=1, keepdims=True)
    cnt_ref[...] = base_ref[...].astype(jnp.int32)


def _moe_pre(xt, mod, g, rwt_hi, rwt_lo, rb, wsg, wsu, wsd, n_tok, seq, batch):
    d = xt.shape[1]
    e = rwt_hi.shape[0]
    full = lambda a: pl.BlockSpec(a.shape, lambda i: (0,) * a.ndim)
    tok_major = lambda: pl.BlockSpec((TOP_K, ROW_TILE), lambda i: (0, i))
    return pl.pallas_call(
        _moe_pre_kernel,
        grid=(n_tok // ROW_TILE,),
        in_specs=[pl.BlockSpec((ROW_TILE, d), lambda i: (i, 0)),
                  pl.BlockSpec((1, ADA_CHUNKS, d), _mod_index(ROW_TILE, seq, batch)),
                  pl.BlockSpec((1, d), lambda i: (0, 0)),
                  full(rwt_hi), full(rwt_lo), full(rb), full(wsg), full(wsu), full(wsd)],
        out_specs=[pl.BlockSpec((ROW_TILE * (d // LANES), LANES), lambda i: (i, 0)),
                   pl.BlockSpec((ROW_TILE, d), lambda i: (i, 0)),
                   tok_major(), tok_major(), tok_major(),
                   pl.BlockSpec((e, LANES), lambda i: (0, 0))],
        out_shape=[jax.ShapeDtypeStruct((n_tok * (d // LANES), LANES), F32),
                   jax.ShapeDtypeStruct((n_tok, d), F32),
                   jax.ShapeDtypeStruct((TOP_K, n_tok), jnp.int32),
                   jax.ShapeDtypeStruct((TOP_K, n_tok), F32),
                   jax.ShapeDtypeStruct((TOP_K, n_tok), jnp.int32),
                   jax.ShapeDtypeStruct((e, LANES), jnp.int32)],
        scratch_shapes=[pltpu.VMEM((e, LANES), F32)],
        compiler_params=_params(("arbitrary",)),
        name="moe_pre",
    )(xt, mod, g.reshape(1, d), rwt_hi, rwt_lo, rb, wsg, wsu, wsd)


def _moe_pos_kernel(eidx_ref, rank_ref, pstart_ref, pos_ref):
    n_e = pstart_ref.shape[0]
    tp = eidx_ref.shape[1]
    io_e = lax.broadcasted_iota(jnp.int32, (n_e, tp), 0)
    ps = pstart_ref[...]
    e = eidx_ref[...]
    rows = [jnp.sum(jnp.where(io_e == e[k:k + 1, :], ps, 0.0), axis=0, keepdims=True) for k in range(TOP_K)]
    pos_ref[...] = rank_ref[...] + jnp.concatenate(rows, axis=0).astype(jnp.int32)


def _moe_pos(eidx, rank, pstart):
    n_tok = eidx.shape[1]
    blk = lambda: pl.BlockSpec((TOP_K, ROW_TILE), lambda i: (0, i))
    return pl.pallas_call(
        _moe_pos_kernel,
        grid=(n_tok // ROW_TILE,),
        in_specs=[blk(), blk(), pl.BlockSpec(pstart.shape, lambda i: (0, 0))],
        out_specs=blk(),
        out_shape=jax.ShapeDtypeStruct(eidx.shape, jnp.int32),
        compiler_params=_params(("arbitrary",)),
        name="moe_pos",
    )(eidx, rank, pstart)


def _moe_dispatch_kernel(lastblk_ref, nused_ref, pos_hbm, h_hbm, xg_hbm, idx_smem, zbuf, idx_sem, row_sem, z_sem, *,
                         n_tiles, nc, n_blocks):
    i = pl.program_id(0)
    slot = i % 2
    n_e = lastblk_ref.shape[0]
    rows_per_step = DISP_TT * TOP_K

    def idx_copy(t, s):
        return pltpu.make_async_copy(pos_hbm.at[:, pl.ds(pl.multiple_of(t * DISP_TT, DISP_TT), DISP_TT)],
                                     idx_smem.at[s], idx_sem.at[s])

    def block_copy(b):
        return pltpu.make_async_copy(zbuf, xg_hbm.at[pl.ds(pl.multiple_of(b * (MOE_RB * nc), MOE_RB * nc),
                                                           MOE_RB * nc), :], z_sem)

    def wait_rows(s):
        pltpu.make_async_copy(h_hbm.at[pl.ds(0, rows_per_step * nc), :], xg_hbm.at[pl.ds(0, rows_per_step * nc), :],
                              row_sem.at[s]).wait()

    @pl.when(i == 0)
    def _():
        idx_copy(0, 0).start()
        zbuf[...] = jnp.zeros_like(zbuf)

        def for_zero_blocks(fn):
            def per_expert(e, carry):
                @pl.when(lastblk_ref[e] >= 0)
                def _():
                    fn(block_copy(lastblk_ref[e]))
                return carry

            def per_block(b, carry):
                fn(block_copy(b))
                return carry

            lax.fori_loop(0, n_e, per_expert, 0)
            lax.fori_loop(nused_ref[0], n_blocks, per_block, 0)

        for_zero_blocks(lambda cp: cp.start())
        for_zero_blocks(lambda cp: cp.wait())

    idx_copy(i, slot).wait()

    @pl.when(i + 1 < n_tiles)
    def _():
        idx_copy(i + 1, 1 - slot).start()

    def body(r, carry):
        tok = i * DISP_TT + r
        src = h_hbm.at[pl.ds(pl.multiple_of(tok * nc, nc), nc), :]
        for k in range(TOP_K):
            p = idx_smem[slot, k, r]
            pltpu.make_async_copy(src, xg_hbm.at[pl.ds(pl.multiple_of(p * nc, nc), nc), :], row_sem.at[slot]).start()
        return carry

    lax.fori_loop(0, DISP_TT, body, 0, unroll=2)

    @pl.when(i > 0)
    def _():
        wait_rows(1 - slot)

    @pl.when(i == n_tiles - 1)
    def _():
        wait_rows(slot)


def _moe_dispatch(lastblk, nused, pos, h, n_blocks):
    n_tok = pos.shape[1]
    nc = h.shape[0] // n_tok
    n_tiles = n_tok // DISP_TT
    grid_spec = pltpu.PrefetchScalarGridSpec(
        num_scalar_prefetch=2,
        grid=(n_tiles,),
        in_specs=[pl.BlockSpec(memory_space=pl.ANY), pl.BlockSpec(memory_space=pl.ANY)],
        out_specs=pl.BlockSpec(memory_space=pl.ANY),
        scratch_shapes=[pltpu.SMEM((2, TOP_K, DISP_TT), jnp.int32),
                        pltpu.VMEM((MOE_RB * nc, LANES), F32),
                        pltpu.SemaphoreType.DMA((2,)),
                        pltpu.SemaphoreType.DMA((2,)),
                        pltpu.SemaphoreType.DMA(())],
    )
    return pl.pallas_call(
        functools.partial(_moe_dispatch_kernel, n_tiles=n_tiles, nc=nc, n_blocks=n_blocks),
        grid_spec=grid_spec,
        out_shape=jax.ShapeDtypeStruct((n_blocks * MOE_RB * nc, LANES), F32),
        compiler_params=_params(("arbitrary",)),
        name="moe_dispatch",
    )(lastblk, nused, pos, h)


def _moe_expert_kernel(be_ref, nused_ref, x_ref, wg_ref, wu_ref, wd_ref, o_ref):
    i = pl.program_id(0)
    nused = nused_ref[0]
    nc = x_ref.shape[0] // MOE_RB

    @pl.when(i < nused)
    def _():
        xb = _load_token_tiles(x_ref, 0, MOE_RB, nc).astype(BF16)
        gate = jnp.dot(xb, wg_ref[...].astype(BF16), preferred_element_type=F32)
        up = jnp.dot(xb, wu_ref[...].astype(BF16), preferred_element_type=F32)
        hid = (gate * _sigmoid(gate) * up).astype(BF16)
        _store_token_tiles(o_ref, jnp.dot(hid, wd_ref[...].astype(BF16), preferred_element_type=F32))

    @pl.when(i >= nused)
    def _():
        o_ref[...] = jnp.zeros_like(o_ref)


def _moe_experts(block_e, nused, xg, wg, wu, wd):
    nb = block_e.shape[0]
    d = wg.shape[1]
    nc = d // LANES
    de = wg.shape[2]
    grid_spec = pltpu.PrefetchScalarGridSpec(
        num_scalar_prefetch=2,
        grid=(nb,),
        in_specs=[pl.BlockSpec((MOE_RB * nc, LANES), lambda i, be, nu: (jnp.minimum(i, nu[0] - 1), 0)),
                  pl.BlockSpec((None, d, de), lambda i, be, nu: (be[i], 0, 0)),
                  pl.BlockSpec((None, d, de), lambda i, be, nu: (be[i], 0, 0)),
                  pl.BlockSpec((None, de, d), lambda i, be, nu: (be[i], 0, 0))],
        out_specs=pl.BlockSpec((MOE_RB * nc, LANES), lambda i, be, nu: (i, 0)),
    )
    return pl.pallas_call(
        _moe_expert_kernel,
        grid_spec=grid_spec,
        out_shape=jax.ShapeDtypeStruct((nb * MOE_RB * nc, LANES), F32),
        compiler_params=_params(("arbitrary",)),
        name="moe_experts",
    )(block_e, nused, xg, wg, wu, wd)


def _moe_combine_kernel(pos_hbm, og_hbm, gates_ref, y0_ref, x_ref, mod_ref, g_ref, o_ref,
                        idx_smem, buf, idx_sem, row_sem, *, n_tiles):
    i = pl.program_id(0)
    slot = i % 2
    nxt = 1 - slot
    n_rows = TOP_K * DISP_TT
    nc = buf.shape[1] // n_rows

    def idx_copy(t, s):
        return pltpu.make_async_copy(pos_hbm.at[:, pl.ds(pl.multiple_of(t * DISP_TT, DISP_TT), DISP_TT)],
                                     idx_smem.at[s], idx_sem.at[s])

    def issue_rows(s):
        def body(r, carry):
            for k in range(TOP_K):
                p = idx_smem[s, k, r]
                pltpu.make_async_copy(og_hbm.at[pl.ds(pl.multiple_of(p * nc, nc), nc), :],
                                      buf.at[s, pl.ds(pl.multiple_of((k * DISP_TT + r) * nc, nc), nc), :],
                                      row_sem.at[s]).start()
            return carry
        lax.fori_loop(0, DISP_TT, body, 0, unroll=2)

    def wait_rows(s):
        pltpu.make_async_copy(og_hbm.at[pl.ds(0, n_rows * nc), :], buf.at[s], row_sem.at[s]).wait()

    @pl.when(i == 0)
    def _():
        idx_copy(0, 0).start()
        idx_copy(0, 0).wait()
        issue_rows(0)

        @pl.when(n_tiles > 1)
        def _():
            idx_copy(1, 1).start()

    @pl.when(i + 1 < n_tiles)
    def _():
        idx_copy(i + 1, nxt).wait()
        issue_rows(nxt)

    wait_rows(slot)

    @pl.when(i + 2 < n_tiles)
    def _():
        idx_copy(i + 2, slot).start()

    gates = gates_ref[...].T
    y = y0_ref[...]
    for k in range(TOP_K):
        y = y + gates[:, k:k + 1] * _load_token_tiles(buf, k * DISP_TT, DISP_TT, nc, lead=slot)
    ms = jnp.mean(y * y, axis=-1, keepdims=True)
    yn = y * lax.rsqrt(ms + RMS_EPS) * g_ref[...]
    o_ref[...] = x_ref[...] + mod_ref[0, 5:6, :] * yn


def _moe_combine(pos, og, gates, y0, xt, mod, g, n_tok, seq, batch):
    d = xt.shape[1]
    n_tiles = n_tok // DISP_TT
    row_spec = lambda w: pl.BlockSpec((DISP_TT, w), lambda i: (i, 0))
    return pl.pallas_call(
        functools.partial(_moe_combine_kernel, n_tiles=n_tiles),
        grid=(n_tiles,),
        in_specs=[pl.BlockSpec(memory_space=pl.ANY),
                  pl.BlockSpec(memory_space=pl.ANY),
                  pl.BlockSpec((TOP_K, DISP_TT), lambda i: (0, i)), row_spec(d), row_spec(d),
                  pl.BlockSpec((1, ADA_CHUNKS, d), _mod_index(DISP_TT, seq, batch)),
                  pl.BlockSpec((1, d), lambda i: (0, 0))],
        out_specs=row_spec(d),
        out_shape=jax.ShapeDtypeStruct(xt.shape, F32),
        input_output_aliases={4: 0},
        scratch_shapes=[pltpu.SMEM((2, TOP_K, DISP_TT), jnp.int32),
                        pltpu.VMEM((2, TOP_K * DISP_TT * (d // LANES), LANES), F32),
                        pltpu.SemaphoreType.DMA((2,)),
                        pltpu.SemaphoreType.DMA((2,))],
        compiler_params=_params(("arbitrary",)),
        name="moe_combine",
    )(pos, og, gates, y0, xt, mod, g.reshape(1, d))


def _dispatch_plan(counts, n_assign):
    e = counts.shape[0]
    padded = (counts + MOE_RB - 1) // MOE_RB * MOE_RB
    pend = jnp.cumsum(padded)
    pstart = pend - padded
    nb = -(-(n_assign + e * (MOE_RB - 1)) // MOE_RB)
    first_row = jnp.arange(nb, dtype=jnp.int32) * MOE_RB
    block_e = jnp.minimum(jnp.sum((pend[None, :] <= first_row[:, None]).astype(jnp.int32), axis=1), e - 1)
    nused = (pend[-1] // MOE_RB).astype(jnp.int32).reshape(1)
    lastblk = jnp.where(counts > 0, pend // MOE_RB - 1, -1)
    return pstart.astype(jnp.int32), lastblk.astype(jnp.int32), block_e.astype(jnp.int32), nused, nb


def _moe_layer(xt, mod, g_pre, g_post, router_w, router_b, w_gate, w_up, w_down, ws_gate, ws_up, ws_down,
               n_tok, seq, batch):
    e = router_w.shape[1]
    rwt = router_w.astype(F32).T
    rwt_hi, rwt_lo = _split(rwt)
    h, y0, eidx, gates, rank, cnt = _moe_pre(xt, mod, g_pre, rwt_hi, rwt_lo, router_b.astype(F32).reshape(e, 1),
                                             ws_gate.astype(BF16), ws_up.astype(BF16), ws_down.astype(BF16),
                                             n_tok, seq, batch)
    pstart, lastblk, block_e, nused, nb = _dispatch_plan(cnt[:, 0], n_tok * TOP_K)
    pos = _moe_pos(eidx, rank, pstart.astype(F32).reshape(e, 1))
    xg = _moe_dispatch(lastblk, nused, pos, h, nb)
    og = _moe_experts(block_e, nused, xg, w_gate, w_up, w_down)
    return _moe_combine(pos, og, gates, y0, xt, mod, g_post, n_tok, seq, batch)


def kernel(x, c, ctx, c_ctx, ada_w, ada_b, g_pre_mix, g_post_mix, g_pre_ffn, g_post_ffn, ab_w_in, ab_w_out, gla_dw_f, gla_db_f, gla_dw_b, gla_db_b, gla_norm_g, conv_w, conv_b, conv_ln_g, conv_ln_b, na_w_in, na_w_out, na_rpb, moe_router_w, moe_router_b, moe_w_gate, moe_w_up, moe_w_down, moe_ws_gate, moe_ws_up, moe_ws_down):
    batch, seq, d = x.shape
    ctx_len = ctx.shape[1]
    depth = ada_w.shape[0]
    n_x = batch * seq
    n_all = n_x + batch * ctx_len
    assert seq % ROW_TILE == 0 and (batch * ctx_len) % ROW_TILE == 0 and n_x % ctx_len == 0

    xt = jnp.concatenate([x.reshape(n_x, d), ctx.reshape(batch * ctx_len, d)], axis=0)

    cc = jnp.zeros((ADA_ROWS, d), F32).at[:batch].set(c).at[batch].set(c_ctx)
    mods = _ada_mod(cc, ada_w, ada_b)[:, :batch + 1].reshape(depth, batch + 1, ADA_CHUNKS, d)

    for layer in range(depth):
        last = layer == depth - 1
        i = layer // 2
        mod = mods[layer]
        if layer % 2 == 0:
            w_in = ab_w_in[i]
            cuts = np.cumsum([0, GLA_QK, GLA_QK, GLA_V, GLA_V, GLA_LOWRANK, GLA_LOWRANK, CONV_CH, CONV_CH])
            seg = lambda j: w_in[:, cuts[j]:cuts[j + 1]]
            lr_pad = jnp.zeros((d, LANES - 2 * GLA_LOWRANK), w_in.dtype)
            w_cat = jnp.concatenate([seg(0), seg(1), seg(2), seg(3), seg(6), seg(7), seg(4), seg(5), lr_pad],
                                    axis=1).astype(BF16)
            qk, v, g, glu, lr = _prenorm_proj(xt, mod, g_pre_mix[layer], w_cat,
                                              (2 * GLA_QK, GLA_V, GLA_V, 2 * CONV_CH, LANES), 0, 1, seq, batch)
            dwcat = jnp.zeros((LANES, 2 * GLA_QK), F32)
            dwcat = dwcat.at[:GLA_LOWRANK, :GLA_QK].set(gla_dw_f[i])
            dwcat = dwcat.at[GLA_LOWRANK:2 * GLA_LOWRANK, GLA_QK:].set(gla_dw_b[i]).astype(BF16)
            dbcat = jnp.concatenate([gla_db_f[i], gla_db_b[i]]).reshape(1, 2 * GLA_QK).astype(F32)
            a_out = _gla(qk, v, g, lr, dwcat, dbcat, gla_norm_g[i].reshape(1, GLA_V).astype(F32),
                         seq, ctx_len, batch)
            cw = conv_w[i].astype(F32)
            cvec = lambda t: t.reshape(1, CONV_CH).astype(F32)
            cv_x = _conv_module(glu, cw, cvec(conv_b[i]), cvec(conv_ln_g[i]), cvec(conv_ln_b[i]), seq, 0, batch)
            cv_c = _conv_module(glu, cw, cvec(conv_b[i]), cvec(conv_ln_g[i]), cvec(conv_ln_b[i]), ctx_len,
                                n_x // ctx_len, batch)
            cv = jnp.concatenate([cv_x, cv_c], axis=0)
            n_rows = n_x if last else n_all
            xt = _outproj([a_out, cv], ab_w_out[i].astype(BF16), xt, mod, g_post_mix[layer], 2, n_rows, seq, batch)
        else:
            (z,) = _prenorm_proj(xt, mod, g_pre_mix[layer], na_w_in[i].astype(BF16),
                                 (3 * NA_HEADS * NA_DH,), 0, 1, seq, batch)
            bias = _na_bias_table(na_rpb[i], seq // GRID_W)
            o = _na_attention(z, bias, seq, ctx_len, batch)
            if last:
                xt = _outproj([o], na_w_out[i].astype(BF16), xt, mod, g_post_mix[layer], 2, n_x, seq, batch)
            else:
                raise NotImplementedError("context update after a neighbourhood-attention layer")
        n_tok = n_x if last else n_all
        xt = _moe_layer(xt, mod, g_pre_ffn[layer], g_post_ffn[layer], moe_router_w[layer], moe_router_b[layer],
                        moe_w_gate[layer], moe_w_up[layer], moe_w_down[layer], moe_ws_gate[layer],
                        moe_ws_up[layer], moe_ws_down[layer], n_tok, seq, batch)
    return xt[:n_x].reshape(batch, seq, d)
```

```python
import functools

import numpy as np
import jax
import jax.numpy as jnp
from jax import lax
from jax.experimental import pallas as pl
from jax.experimental.pallas import tpu as pltpu

F32 = jnp.float32
BF16 = jnp.bfloat16

GRID_W = 64
RMS_EPS = 1e-6
LN_EPS = 1e-5
ADA_CHUNKS = 6
GLA_HEADS = 4
GLA_DK = 64
GLA_DV = 128
GLA_LOWRANK = 16
GLA_GATE_NORM = 16.0
GLA_CHUNK = 64
ROPE_BASE = 10000.0
GLA_QK = GLA_HEADS * GLA_DK
GLA_V = GLA_HEADS * GLA_DV
CONV_CH = 512
CONV_WIDTH = 31
NA_HEADS = 16
NA_DH = 64
NA_WIN_ROWS = 8
NA_WIN_COLS = 16
N_EXPERTS = 256
TOP_K = 8
N_GROUPS = 8
TOPK_GROUPS = 4
ROUTE_SCALE = 2.5

LANES = 128
ROW_TILE = 512
ADA_ROWS = 24
ADA_TN = 1536
CONV_RC = 32
CONV_HALO = 16
NA_ROW_UNROLL = 4
MOE_RB = 256
DISP_TT = 128
VMEM_LIMIT = 56 * 1024 * 1024

_NT = (((1,), (1,)), ((), ()))
_TN = (((0,), (0,)), ((), ()))


def _params(sem, **kw):
    return pltpu.CompilerParams(dimension_semantics=sem, vmem_limit_bytes=VMEM_LIMIT, **kw)


def _sigmoid(x):
    return 1.0 / (1.0 + jnp.exp(-x))


def _split(a):
    hi = a.astype(BF16)
    lo = (a - hi.astype(F32)).astype(BF16)
    return hi, lo


def _dot3(a, b_hi, b_lo, dims=(((1,), (0,)), ((), ()))):
    a_hi, a_lo = _split(a)
    d = lambda x, y: lax.dot_general(x, y, dims, preferred_element_type=F32)
    return d(a_hi, b_hi) + (d(a_hi, b_lo) + d(a_lo, b_hi))


def _prenorm_mod(x, g, mod_ref, shift_i, scale_i):
    ms = jnp.mean(x * x, axis=-1, keepdims=True)
    h = x * lax.rsqrt(ms + RMS_EPS) * g
    return h * (1.0 + mod_ref[0, scale_i:scale_i + 1, :]) + mod_ref[0, shift_i:shift_i + 1, :]


def _mod_index(tile_rows, seq, batch):
    per_batch = seq // tile_rows
    return lambda i, *_: (jnp.minimum(i // per_batch, batch), 0, 0)


def _ada_kernel(c_ref, w_ref, b_ref, o_ref):
    c = c_ref[...]
    s = c * _sigmoid(c)
    w_hi, w_lo = _split(w_ref[...])
    o_ref[...] = _dot3(s, w_hi, w_lo) + b_ref[...]


def _ada_mod(cc, ada_w, ada_b):
    depth, d, n = ada_w.shape
    return pl.pallas_call(
        _ada_kernel,
        grid=(depth, n // ADA_TN),
        in_specs=[pl.BlockSpec((ADA_ROWS, d), lambda l, j: (0, 0)),
                  pl.BlockSpec((None, d, ADA_TN), lambda l, j: (l, 0, j)),
                  pl.BlockSpec((None, 1, ADA_TN), lambda l, j: (l, 0, j))],
        out_specs=pl.BlockSpec((None, ADA_ROWS, ADA_TN), lambda l, j: (l, 0, j)),
        out_shape=jax.ShapeDtypeStruct((depth, ADA_ROWS, n), F32),
        compiler_params=_params(("arbitrary", "arbitrary")),
        name="ada_mod",
    )(cc, ada_w, ada_b.reshape(depth, 1, n))


def _prenorm_proj_kernel(x_ref, mod_ref, g_ref, w_ref, *o_refs, shift_i, scale_i):
    h = _prenorm_mod(x_ref[...], g_ref[...], mod_ref, shift_i, scale_i).astype(BF16)
    off = 0
    for o_ref in o_refs:
        n = o_ref.shape[-1]
        o_ref[...] = jnp.dot(h, w_ref[:, off:off + n], preferred_element_type=F32).astype(o_ref.dtype)
        off += n


def _prenorm_proj(xt, mod, g, w, splits, shift_i, scale_i, seq, batch):
    rows, d = xt.shape
    n = w.shape[1]
    return pl.pallas_call(
        functools.partial(_prenorm_proj_kernel, shift_i=shift_i, scale_i=scale_i),
        grid=(rows // ROW_TILE,),
        in_specs=[pl.BlockSpec((ROW_TILE, d), lambda i: (i, 0)),
                  pl.BlockSpec((1, ADA_CHUNKS, d), _mod_index(ROW_TILE, seq, batch)),
                  pl.BlockSpec((1, d), lambda i: (0, 0)),
                  pl.BlockSpec((d, n), lambda i: (0, 0))],
        out_specs=[pl.BlockSpec((ROW_TILE, s), lambda i: (i, 0)) for s in splits],
        out_shape=[jax.ShapeDtypeStruct((rows, s), BF16) for s in splits],
        compiler_params=_params(("arbitrary",)),
        name="prenorm_proj",
    )(xt, mod, g.reshape(1, d), w)


def _gla_kernel(qkx_ref, vx_ref, gx_ref, lrx_ref, qkc_ref, vc_ref, gc_ref, lrc_ref,
                dw_ref, db_ref, cos_ref, sin_ref, gg_ref, ox_ref, oc_ref,
                sf_ref, sb_ref, accx_ref, accc_ref, *, n_x, n_c):
    L = GLA_CHUNK
    row = lax.broadcasted_iota(jnp.int32, (L, L), 0)
    col = lax.broadcasted_iota(jnp.int32, (L, L), 1)
    lower = row >= col
    upper = col >= row
    tri_f = jnp.where(lower, 1.0, 0.0).astype(BF16)
    tri_b = jnp.where(upper, 1.0, 0.0).astype(BF16)
    lane = lax.broadcasted_iota(jnp.int32, (1, GLA_QK), 1)
    head_of_lane = lane // GLA_DK
    first_half = (lane % (GLA_DK // 2)) < (GLA_DK // 4)
    bd_mask = (lax.broadcasted_iota(jnp.int32, (GLA_V, GLA_QK), 0) // GLA_DV
               == lax.broadcasted_iota(jnp.int32, (GLA_V, GLA_QK), 1) // GLA_DK)

    def swap_pairs(t):
        nf = GLA_DK // 4
        return jnp.where(first_half, pltpu.roll(t, GLA_QK - nf, 1), pltpu.roll(t, nf, 1))

    def chunk(fwd, c, qk_ref, v_ref, lr_ref, latent):
        rows = pl.ds(pl.multiple_of(c * L, L), L)
        q = qk_ref[rows, 0:GLA_QK].astype(F32) * (GLA_DK ** -0.5)
        k = qk_ref[rows, GLA_QK:2 * GLA_QK].astype(F32)
        if latent:
            cs = cos_ref[rows, :]
            sn = sin_ref[rows, :]
            q = q * cs + swap_pairs(q) * sn
            k = k * cs + swap_pairs(k) * sn
        dsl = slice(0, GLA_QK) if fwd else slice(GLA_QK, 2 * GLA_QK)
        pre = jnp.dot(lr_ref[rows, :], dw_ref[:, dsl], preferred_element_type=F32) + db_ref[:, dsl]
        logd = (jnp.minimum(pre, 0.0) - jnp.log(1.0 + jnp.exp(-jnp.abs(pre)))) * (1.0 / GLA_GATE_NORM)
        d_hi, d_lo = _split(logd)
        tri = tri_f if fwd else tri_b
        b = (jnp.dot(tri, d_hi, preferred_element_type=F32)
             + jnp.dot(tri, d_lo, preferred_element_type=F32))
        mid = L // 2 - 1 if fwd else L // 2
        end = L - 1 if fwd else 0
        b_mid = b[mid:mid + 1, :]
        b_end = b[end:end + 1, :]
        qe = q * jnp.exp(b - b_mid)
        ke = (k * jnp.exp(b_mid - b)).astype(BF16)
        q_in = (q * jnp.exp(b)).astype(BF16)
        k_out = (k * jnp.exp(b_end - b)).astype(BF16)
        dec = jnp.exp(b_end)
        vv = v_ref[rows, :]
        mask = lower if fwd else upper
        outs = []
        for h in range(GLA_HEADS):
            qm = jnp.where(head_of_lane == h, qe, 0.0).astype(BF16)
            a = lax.dot_general(qm, ke, _NT, preferred_element_type=F32)
            a = jnp.where(mask, a, 0.0).astype(BF16)
            outs.append(jnp.dot(a, vv[:, h * GLA_DV:(h + 1) * GLA_DV], preferred_element_type=F32))
        o_intra = jnp.concatenate(outs, axis=-1)
        s_ref = sf_ref if fwd else sb_ref
        st = s_ref[...]
        o_inter = lax.dot_general(q_in, st.astype(BF16), _NT, preferred_element_type=F32)
        upd = lax.dot_general(vv, k_out, _TN, preferred_element_type=F32)
        s_ref[...] = jnp.where(bd_mask, st * dec + upd, 0.0)
        return o_intra + o_inter

    def finish(o, g_ref, c):
        rows = pl.ds(pl.multiple_of(c * L, L), L)
        g = g_ref[rows, :].astype(F32)
        parts = []
        for h in range(GLA_HEADS):
            oh = o[:, h * GLA_DV:(h + 1) * GLA_DV]
            ms = jnp.mean(oh * oh, axis=-1, keepdims=True)
            parts.append(oh * lax.rsqrt(ms + RMS_EPS))
        on = jnp.concatenate(parts, axis=-1) * gg_ref[...]
        return (on * (g * _sigmoid(g))).astype(BF16)

    def run_seq(n, qk_ref, v_ref, g_ref, lr_ref, acc_ref, o_ref, latent):
        half = n // 2

        def rows_of(c):
            return pl.ds(pl.multiple_of(c * L, L), L)

        def first(i, carry):
            cf, cb = i, n - 1 - i
            acc_ref[rows_of(cf), :] = chunk(True, cf, qk_ref, v_ref, lr_ref, latent)
            acc_ref[rows_of(cb), :] = chunk(False, cb, qk_ref, v_ref, lr_ref, latent)
            return carry

        def second(i, carry):
            cf, cb = i, n - 1 - i
            of = acc_ref[rows_of(cf), :] + chunk(True, cf, qk_ref, v_ref, lr_ref, latent)
            o_ref[rows_of(cf), :] = finish(of, g_ref, cf)
            ob = acc_ref[rows_of(cb), :] + chunk(False, cb, qk_ref, v_ref, lr_ref, latent)
            o_ref[rows_of(cb), :] = finish(ob, g_ref, cb)
            return carry

        lax.fori_loop(0, half, first, 0)
        lax.fori_loop(half, n, second, 0)

    sf_ref[...] = jnp.zeros_like(sf_ref)
    sb_ref[...] = jnp.zeros_like(sb_ref)
    run_seq(n_c, qkc_ref, vc_ref, gc_ref, lrc_ref, accc_ref, oc_ref, False)
    run_seq(n_x, qkx_ref, vx_ref, gx_ref, lrx_ref, accx_ref, ox_ref, True)


def _rope_tables(seq):
    half = GLA_DK // 2
    nf = half // 2
    inv = ROPE_BASE ** (-jnp.arange(nf, dtype=F32) / nf)
    t = jnp.arange(seq)
    rows = (t // GRID_W).astype(F32)
    cols = (t % GRID_W).astype(F32)
    ar = rows[:, None] * inv[None, :]
    ac = cols[:, None] * inv[None, :]
    cos_h = jnp.concatenate([jnp.cos(ar), jnp.cos(ar), jnp.cos(ac), jnp.cos(ac)], axis=-1)
    sin_h = jnp.concatenate([-jnp.sin(ar), jnp.sin(ar), -jnp.sin(ac), jnp.sin(ac)], axis=-1)
    return jnp.tile(cos_h, (1, GLA_HEADS)), jnp.tile(sin_h, (1, GLA_HEADS))


def _gla(qk, v, g, lr, dwcat, dbcat, gla_g, seq, ctx_len, batch):
    rows = qk.shape[0]
    cos_t, sin_t = _rope_tables(seq)
    cb = (batch * seq) // ctx_len
    xs = lambda w: pl.BlockSpec((seq, w), lambda b: (b, 0))
    cs = lambda w: pl.BlockSpec((ctx_len, w), lambda b: (cb + b, 0))
    full = lambda a: pl.BlockSpec(a.shape, lambda b: (0,) * a.ndim)
    out_x, out_c = pl.pallas_call(
        functools.partial(_gla_kernel, n_x=seq // GLA_CHUNK, n_c=ctx_len // GLA_CHUNK),
        grid=(batch,),
        in_specs=[xs(2 * GLA_QK), xs(GLA_V), xs(GLA_V), xs(LANES),
                  cs(2 * GLA_QK), cs(GLA_V), cs(GLA_V), cs(LANES),
                  full(dwcat), full(dbcat), full(cos_t), full(sin_t), full(gla_g)],
        out_specs=[pl.BlockSpec((seq, GLA_V), lambda b: (b, 0)),
                   pl.BlockSpec((ctx_len, GLA_V), lambda b: (b, 0))],
        out_shape=[jax.ShapeDtypeStruct((batch * seq, GLA_V), BF16),
                   jax.ShapeDtypeStruct((batch * ctx_len, GLA_V), BF16)],
        scratch_shapes=[pltpu.VMEM((GLA_V, GLA_QK), F32), pltpu.VMEM((GLA_V, GLA_QK), F32),
                        pltpu.VMEM((seq, GLA_V), F32), pltpu.VMEM((ctx_len, GLA_V), F32)],
        compiler_params=_params(("arbitrary",)),
        name="gla",
    )(qk, v, g, lr, qk, v, g, lr, dwcat, dbcat, cos_t, sin_t, gla_g)
    del rows
    return jnp.concatenate([out_x, out_c], axis=0)


def _conv_kernel(glu_ref, w_ref, cb_ref, lg_ref, lb_ref, o_ref, pad_ref, win_ref, *, t_len):
    ch = CONV_CH
    zeros = jnp.zeros((CONV_HALO, ch), F32)
    pad_ref[0:CONV_HALO, :] = zeros
    pad_ref[CONV_HALO + t_len:2 * CONV_HALO + t_len, :] = zeros
    fill = 128

    def fill_body(i, carry):
        r = pl.multiple_of(i * fill, fill)
        a = glu_ref[pl.ds(r, fill), 0:ch].astype(F32)
        b = glu_ref[pl.ds(r, fill), ch:2 * ch].astype(F32)
        pad_ref[pl.ds(CONV_HALO + r, fill), :] = a * _sigmoid(b)
        return carry

    lax.fori_loop(0, t_len // fill, fill_body, 0)
    first_tap = CONV_HALO - CONV_WIDTH // 2

    def body(i, carry):
        base = pl.multiple_of(i * CONV_RC, CONV_RC)
        win_ref[...] = pad_ref[pl.ds(base, CONV_RC + 2 * CONV_HALO), :]
        acc = jnp.zeros((CONV_RC, ch), F32)
        for j in range(CONV_WIDTH):
            acc = acc + w_ref[j:j + 1, :] * win_ref[first_tap + j:first_tap + j + CONV_RC, :]
        acc = acc + cb_ref[...]
        mu = jnp.mean(acc, axis=-1, keepdims=True)
        cen = acc - mu
        var = jnp.mean(cen * cen, axis=-1, keepdims=True)
        y = cen * lax.rsqrt(var + LN_EPS) * lg_ref[...] + lb_ref[...]
        o_ref[pl.ds(base, CONV_RC), :] = (y * _sigmoid(y)).astype(BF16)
        return carry

    lax.fori_loop(0, t_len // CONV_RC, body, 0)


def _conv_module(glu, conv_w, conv_b, ln_g, ln_b, t_len, first_block, batch):
    full = lambda a: pl.BlockSpec(a.shape, lambda b: (0,) * a.ndim)
    return pl.pallas_call(
        functools.partial(_conv_kernel, t_len=t_len),
        grid=(batch,),
        in_specs=[pl.BlockSpec((t_len, 2 * CONV_CH), lambda b: (first_block + b, 0)),
                  full(conv_w), full(conv_b), full(ln_g), full(ln_b)],
        out_specs=pl.BlockSpec((t_len, CONV_CH), lambda b: (b, 0)),
        out_shape=jax.ShapeDtypeStruct((batch * t_len, CONV_CH), BF16),
        scratch_shapes=[pltpu.VMEM((t_len + 2 * CONV_HALO, CONV_CH), F32),
                        pltpu.VMEM((CONV_RC + 2 * CONV_HALO, CONV_CH), F32)],
        compiler_params=_params(("arbitrary",)),
        name="conv_module",
    )(glu, conv_w, conv_b, ln_g, ln_b)


def _outproj_kernel(*refs, n_lhs, gate_i):
    lhs_refs = refs[:n_lhs]
    w_ref, x_ref, mod_ref, g_ref, o_ref = refs[n_lhs:]
    off = 0
    y = None
    for l_ref in lhs_refs:
        kk = l_ref.shape[-1]
        t = jnp.dot(l_ref[...], w_ref[off:off + kk, :], preferred_element_type=F32)
        y = t if y is None else y + t
        off += kk
    ms = jnp.mean(y * y, axis=-1, keepdims=True)
    yn = y * lax.rsqrt(ms + RMS_EPS) * g_ref[...]
    o_ref[...] = x_ref[...] + mod_ref[0, gate_i:gate_i + 1, :] * yn


def _outproj(lhs_list, w, xt, mod, g, gate_i, n_rows, seq, batch):
    d = xt.shape[1]
    n_lhs = len(lhs_list)
    in_specs = [pl.BlockSpec((ROW_TILE, a.shape[1]), lambda i: (i, 0)) for a in lhs_list]
    in_specs += [pl.BlockSpec(w.shape, lambda i: (0, 0)),
                 pl.BlockSpec((ROW_TILE, d), lambda i: (i, 0)),
                 pl.BlockSpec((1, ADA_CHUNKS, d), _mod_index(ROW_TILE, seq, batch)),
                 pl.BlockSpec((1, d), lambda i: (0, 0))]
    return pl.pallas_call(
        functools.partial(_outproj_kernel, n_lhs=n_lhs, gate_i=gate_i),
        grid=(n_rows // ROW_TILE,),
        in_specs=in_specs,
        out_specs=pl.BlockSpec((ROW_TILE, d), lambda i: (i, 0)),
        out_shape=jax.ShapeDtypeStruct(xt.shape, F32),
        input_output_aliases={n_lhs + 1: 0},
        compiler_params=_params(("arbitrary",)),
        name="outproj",
    )(*lhs_list, w, xt, mod, g.reshape(1, d))


def _na_kernel(q_ref, k_ref, v_ref, kc_ref, vc_ref, bias_ref, o_ref, *, n_rows):
    wr = NA_WIN_ROWS
    lane = lax.broadcasted_iota(jnp.int32, (1, LANES), 1)
    head0 = lane < NA_DH
    kc = kc_ref[...]
    vc = vc_ref[...]

    def body(r, carry):
        r0 = jnp.clip(r - wr // 2, 0, n_rows - wr)
        delta = r - r0
        q2 = q_ref[pl.ds(pl.multiple_of(r * GRID_W, GRID_W), GRID_W), :] * (NA_DH ** -0.5)
        kw = k_ref[pl.ds(pl.multiple_of(r0 * GRID_W, GRID_W), wr * GRID_W), :]
        vw = v_ref[pl.ds(pl.multiple_of(r0 * GRID_W, GRID_W), wr * GRID_W), :]
        outs = []
        for hh in range(2):
            qm = jnp.where(head0 if hh == 0 else jnp.logical_not(head0), q2, jnp.zeros_like(q2))
            s_w = lax.dot_general(qm, kw, _NT, preferred_element_type=F32) + bias_ref[hh, delta]
            s_c = lax.dot_general(qm, kc, _NT, preferred_element_type=F32)
            m = jnp.maximum(jnp.max(s_w, axis=-1, keepdims=True), jnp.max(s_c, axis=-1, keepdims=True))
            p_w = jnp.exp(s_w - m)
            p_c = jnp.exp(s_c - m)
            den = jnp.sum(p_w, axis=-1, keepdims=True) + jnp.sum(p_c, axis=-1, keepdims=True)
            o = (jnp.dot(p_w.astype(BF16), vw, preferred_element_type=F32)
                 + jnp.dot(p_c.astype(BF16), vc, preferred_element_type=F32))
            outs.append(o / den)
        o_ref[pl.ds(pl.multiple_of(r * GRID_W, GRID_W), GRID_W), :] = jnp.where(head0, outs[0], outs[1]).astype(BF16)
        return carry

    lax.fori_loop(0, n_rows, body, 0, unroll=NA_ROW_UNROLL)


def _na_bias_table(rpb, n_rows):
    wr = min(NA_WIN_ROWS, n_rows)
    qc = np.arange(GRID_W)
    cstart = np.clip(qc - NA_WIN_COLS // 2, 0, GRID_W - NA_WIN_COLS)
    kcol = np.arange(GRID_W)
    ok = (kcol[None, :] >= cstart[:, None]) & (kcol[None, :] < cstart[:, None] + NA_WIN_COLS)
    cidx = np.clip(kcol[None, :] - qc[:, None] + NA_WIN_COLS - 1, 0, 2 * NA_WIN_COLS - 2)
    delta = np.arange(wr)
    j = np.arange(wr)
    ridx = j[None, :] - delta[:, None] + NA_WIN_ROWS - 1
    t = rpb[:, ridx][:, :, :, cidx]
    t = jnp.where(jnp.asarray(ok)[None, None, None], t.astype(F32), -jnp.inf)
    t = t.transpose(0, 1, 3, 2, 4)
    return t.reshape(rpb.shape[0], wr, GRID_W, wr * GRID_W)


def _na_attention(z, bias, seq, ctx_len, batch):
    n_pairs = NA_HEADS * NA_DH // LANES
    n_rows = seq // GRID_W
    cb = (batch * seq) // ctx_len
    return pl.pallas_call(
        functools.partial(_na_kernel, n_rows=n_rows),
        grid=(n_pairs, batch),
        in_specs=[pl.BlockSpec((seq, LANES), lambda p, b: (b, p)),
                  pl.BlockSpec((seq, LANES), lambda p, b: (b, n_pairs + p)),
                  pl.BlockSpec((seq, LANES), lambda p, b: (b, 2 * n_pairs + p)),
                  pl.BlockSpec((ctx_len, LANES), lambda p, b: (cb + b, n_pairs + p)),
                  pl.BlockSpec((ctx_len, LANES), lambda p, b: (cb + b, 2 * n_pairs + p)),
                  pl.BlockSpec((2,) + bias.shape[1:], lambda p, b: (p, 0, 0, 0))],
        out_specs=pl.BlockSpec((seq, LANES), lambda p, b: (b, p)),
        out_shape=jax.ShapeDtypeStruct((batch * seq, NA_HEADS * NA_DH), BF16),
        compiler_params=_params(("arbitrary", "arbitrary")),
        name="na_attention",
    )(z, z, z, z, z, bias)


def _store_token_tiles(ref, val, row0=0):
    n, d = val.shape
    nc = d // LANES
    for k in range(nc):
        ref[pl.ds(row0 * nc + k, n, stride=nc), :] = val[:, k * LANES:(k + 1) * LANES]


def _load_token_tiles(ref, row0, n, nc, lead=None):
    chunks = []
    for k in range(nc):
        idx = pl.ds(row0 * nc + k, n, stride=nc)
        chunks.append(ref[idx, :] if lead is None else ref[lead, idx, :])
    return jnp.concatenate(chunks, axis=-1)


def _first_argmax(x, iota_f, n):
    m = jnp.max(x, axis=0, keepdims=True)
    idx = jnp.min(jnp.where(x == m, iota_f, float(n)), axis=0, keepdims=True)
    return m, idx


def _moe_pre_kernel(x_ref, mod_ref, g_ref, rwh_ref, rwl_ref, rb_ref, wsg_ref, wsu_ref, wsd_ref,
                    h_ref, y0_ref, eidx_ref, gates_ref, rank_ref, cnt_ref, base_ref):
    tm = x_ref.shape[0]
    n_e = rwh_ref.shape[0]

    @pl.when(pl.program_id(0) == 0)
    def _():
        base_ref[...] = jnp.zeros_like(base_ref)

    h = _prenorm_mod(x_ref[...], g_ref[...], mod_ref, 3, 4)
    _store_token_tiles(h_ref, h)
    hb = h.astype(BF16)
    gate = jnp.dot(hb, wsg_ref[...], preferred_element_type=F32)
    up = jnp.dot(hb, wsu_ref[...], preferred_element_type=F32)
    hid = (gate * _sigmoid(gate) * up).astype(BF16)
    y0_ref[...] = jnp.dot(hid, wsd_ref[...], preferred_element_type=F32)

    h_lo = (h - hb.astype(F32)).astype(BF16)
    d = lambda a, b: lax.dot_general(a, b, _NT, preferred_element_type=F32)
    logits = d(rwh_ref[...], hb) + (d(rwh_ref[...], h_lo) + d(rwl_ref[...], hb))
    scores = _sigmoid(logits)
    biased = scores + rb_ref[...]
    neg = -jnp.inf
    gsz = n_e // N_GROUPS
    io_g = lax.broadcasted_iota(jnp.int32, (gsz, tm), 0).astype(F32)
    grp_rows = []
    for gi in range(N_GROUPS):
        xg = biased[gi * gsz:(gi + 1) * gsz]
        m1, i1 = _first_argmax(xg, io_g, gsz)
        m2 = jnp.max(jnp.where(io_g == i1, neg, xg), axis=0, keepdims=True)
        grp_rows.append(m1 + m2)
    cur = jnp.concatenate(grp_rows, axis=0)
    io_n = lax.broadcasted_iota(jnp.int32, (N_GROUPS, tm), 0).astype(F32)
    keep = jnp.zeros((N_GROUPS, tm), F32)
    for _ in range(TOPK_GROUPS):
        _, ii = _first_argmax(cur, io_n, N_GROUPS)
        sel = io_n == ii
        keep = jnp.where(sel, 1.0, keep)
        cur = jnp.where(sel, neg, cur)
    cur = jnp.concatenate([jnp.where(keep[gi:gi + 1] > 0.0, biased[gi * gsz:(gi + 1) * gsz], neg)
                           for gi in range(N_GROUPS)], axis=0)
    io_e = lax.broadcasted_iota(jnp.int32, (n_e, tm), 0).astype(F32)
    idxs, gvals = [], []
    chosen = jnp.zeros((n_e, tm), F32)
    for _ in range(TOP_K):
        _, ii = _first_argmax(cur, io_e, n_e)
        sel = io_e == ii
        idxs.append(ii)
        gvals.append(jnp.sum(jnp.where(sel, scores, 0.0), axis=0, keepdims=True))
        chosen = jnp.where(sel, 1.0, chosen)
        cur = jnp.where(sel, neg, cur)
    gv = jnp.concatenate(gvals, axis=0)
    gates_ref[...] = gv / jnp.sum(gv, axis=0, keepdims=True) * ROUTE_SCALE
    eidx_ref[...] = jnp.concatenate(idxs, axis=0).astype(jnp.int32)

    earlier = (lax.broadcasted_iota(jnp.int32, (tm, tm), 0) < lax.broadcasted_iota(jnp.int32, (tm, tm), 1))
    before = jnp.dot(chosen.astype(BF16), jnp.where(earlier, 1.0, 0.0).astype(BF16), preferred_element_type=F32)
    before = before + base_ref[:, 0:1]
    ranks = [jnp.sum(jnp.where(io_e == ii, before, 0.0), axis=0, keepdims=True) for ii in idxs]
    rank_ref[...] = jnp.concatenate(ranks, axis=0).astype(jnp.int32)
    base_ref[...] = base_ref[...] + jnp.sum(chosen, axis=1, keepdims=True)
    cnt_ref[...] = base_ref[...].astype(jnp.int32)


def _moe_pre(xt, mod, g, rwt_hi, rwt_lo, rb, wsg, wsu, wsd, n_tok, seq, batch):
    d = xt.shape[1]
    e = rwt_hi.shape[0]
    full = lambda a: pl.BlockSpec(a.shape, lambda i: (0,) * a.ndim)
    tok_major = lambda: pl.BlockSpec((TOP_K, ROW_TILE), lambda i: (0, i))
    return pl.pallas_call(
        _moe_pre_kernel,
        grid=(n_tok // ROW_TILE,),
        in_specs=[pl.BlockSpec((ROW_TILE, d), lambda i: (i, 0)),
                  pl.BlockSpec((1, ADA_CHUNKS, d), _mod_index(ROW_TILE, seq, batch)),
                  pl.BlockSpec((1, d), lambda i: (0, 0)),
                  full(rwt_hi), full(rwt_lo), full(rb), full(wsg), full(wsu), full(wsd)],
        out_specs=[pl.BlockSpec((ROW_TILE * (d // LANES), LANES), lambda i: (i, 0)),
                   pl.BlockSpec((ROW_TILE, d), lambda i: (i, 0)),
                   tok_major(), tok_major(), tok_major(),
                   pl.BlockSpec((e, LANES), lambda i: (0, 0))],
        out_shape=[jax.ShapeDtypeStruct((n_tok * (d // LANES), LANES), F32),
                   jax.ShapeDtypeStruct((n_tok, d), F32),
                   jax.ShapeDtypeStruct((TOP_K, n_tok), jnp.int32),
                   jax.ShapeDtypeStruct((TOP_K, n_tok), F32),
                   jax.ShapeDtypeStruct((TOP_K, n_tok), jnp.int32),
                   jax.ShapeDtypeStruct((e, LANES), jnp.int32)],
        scratch_shapes=[pltpu.VMEM((e, LANES), F32)],
        compiler_params=_params(("arbitrary",)),
        name="moe_pre",
    )(xt, mod, g.reshape(1, d), rwt_hi, rwt_lo, rb, wsg, wsu, wsd)


def _moe_pos_kernel(eidx_ref, rank_ref, pstart_ref, pos_ref):
    n_e = pstart_ref.shape[0]
    tp = eidx_ref.shape[1]
    io_e = lax.broadcasted_iota(jnp.int32, (n_e, tp), 0)
    ps = pstart_ref[...]
    e = eidx_ref[...]
    rows = [jnp.sum(jnp.where(io_e == e[k:k + 1, :], ps, 0.0), axis=0, keepdims=True) for k in range(TOP_K)]
    pos_ref[...] = rank_ref[...] + jnp.concatenate(rows, axis=0).astype(jnp.int32)


def _moe_pos(eidx, rank, pstart):
    n_tok = eidx.shape[1]
    blk = lambda: pl.BlockSpec((TOP_K, ROW_TILE), lambda i: (0, i))
    return pl.pallas_call(
        _moe_pos_kernel,
        grid=(n_tok // ROW_TILE,),
        in_specs=[blk(), blk(), pl.BlockSpec(pstart.shape, lambda i: (0, 0))],
        out_specs=blk(),
        out_shape=jax.ShapeDtypeStruct(eidx.shape, jnp.int32),
        compiler_params=_params(("arbitrary",)),
        name="moe_pos",
    )(eidx, rank, pstart)


def _moe_dispatch_kernel(lastblk_ref, nused_ref, pos_hbm, h_hbm, xg_hbm, idx_smem, zbuf, hbuf, idx_sem, row_sem, z_sem,
                         h_sem, *, n_tiles, nc, n_blocks):
    i = pl.program_id(0)
    slot = i % 2
    n_e = lastblk_ref.shape[0]
    rows_per_step = DISP_TT * TOP_K

    def h_copy(t, s):
        return pltpu.make_async_copy(h_hbm.at[pl.ds(pl.multiple_of(t * (DISP_TT * nc), DISP_TT * nc), DISP_TT * nc), :],
                                     hbuf.at[s], h_sem.at[s])

    def idx_copy(t, s):
        return pltpu.make_async_copy(pos_hbm.at[:, pl.ds(pl.multiple_of(t * DISP_TT, DISP_TT), DISP_TT)],
                                     idx_smem.at[s], idx_sem.at[s])

    def block_copy(b):
        return pltpu.make_async_copy(zbuf, xg_hbm.at[pl.ds(pl.multiple_of(b * (MOE_RB * nc), MOE_RB * nc),
                                                           MOE_RB * nc), :], z_sem)

    def wait_rows(s):
        pltpu.make_async_copy(h_hbm.at[pl.ds(0, rows_per_step * nc), :], xg_hbm.at[pl.ds(0, rows_per_step * nc), :],
                              row_sem.at[s]).wait()

    @pl.when(i == 0)
    def _():
        idx_copy(0, 0).start()
        zbuf[...] = jnp.zeros_like(zbuf)

        def for_zero_blocks(fn):
            def per_expert(e, carry):
                @pl.when(lastblk_ref[e] >= 0)
                def _():
                    fn(block_copy(lastblk_ref[e]))
                return carry

            def per_block(b, carry):
                fn(block_copy(b))
                return carry

            lax.fori_loop(0, n_e, per_expert, 0)
            lax.fori_loop(nused_ref[0], n_blocks, per_block, 0)

        for_zero_blocks(lambda cp: cp.start())
        for_zero_blocks(lambda cp: cp.wait())

        h_copy(0, 0).start()

    idx_copy(i, slot).wait()
    hslot = i % 3
    h_copy(i, hslot).wait()

    @pl.when(i + 1 < n_tiles)
    def _():
        idx_copy(i + 1, 1 - slot).start()
        h_copy(i + 1, (i + 1) % 3).start()

    def body(r, carry):
        src = hbuf.at[hslot, pl.ds(pl.multiple_of(r * nc, nc), nc), :]
        for k in range(TOP_K):
            p = idx_smem[slot, k, r]
            pltpu.make_async_copy(src, xg_hbm.at[pl.ds(pl.multiple_of(p * nc, nc), nc), :], row_sem.at[slot]).start()
        return carry

    lax.fori_loop(0, DISP_TT, body, 0, unroll=2)

    @pl.when(i > 0)
    def _():
        wait_rows(1 - slot)

    @pl.when(i == n_tiles - 1)
    def _():
        wait_rows(slot)


def _moe_dispatch(lastblk, nused, pos, h, n_blocks):
    n_tok = pos.shape[1]
    nc = h.shape[0] // n_tok
    n_tiles = n_tok // DISP_TT
    grid_spec = pltpu.PrefetchScalarGridSpec(
        num_scalar_prefetch=2,
        grid=(n_tiles,),
        in_specs=[pl.BlockSpec(memory_space=pl.ANY), pl.BlockSpec(memory_space=pl.ANY)],
        out_specs=pl.BlockSpec(memory_space=pl.ANY),
        scratch_shapes=[pltpu.SMEM((2, TOP_K, DISP_TT), jnp.int32),
                        pltpu.VMEM((MOE_RB * nc, LANES), F32),
                        pltpu.VMEM((3, DISP_TT * nc, LANES), F32),
                        pltpu.SemaphoreType.DMA((2,)),
                        pltpu.SemaphoreType.DMA((2,)),
                        pltpu.SemaphoreType.DMA(()),
                        pltpu.SemaphoreType.DMA((3,))],
    )
    return pl.pallas_call(
        functools.partial(_moe_dispatch_kernel, n_tiles=n_tiles, nc=nc, n_blocks=n_blocks),
        grid_spec=grid_spec,
        out_shape=jax.ShapeDtypeStruct((n_blocks * MOE_RB * nc, LANES), F32),
        compiler_params=_params(("arbitrary",)),
        name="moe_dispatch",
    )(lastblk, nused, pos, h)


def _moe_expert_kernel(be_ref, nused_ref, x_ref, wg_ref, wu_ref, wd_ref, o_ref, wgb, wub, wdb):
    i = pl.program_id(0)
    nused = nused_ref[0]
    nc = x_ref.shape[0] // MOE_RB
    new_expert = jnp.logical_or(i == 0, be_ref[i] != be_ref[jnp.maximum(i - 1, 0)])

    @pl.when(jnp.logical_and(i < nused, new_expert))
    def _():
        wgb[...] = wg_ref[...].astype(BF16)
        wub[...] = wu_ref[...].astype(BF16)
        wdb[...] = wd_ref[...].astype(BF16)

    @pl.when(i < nused)
    def _():
        xb = _load_token_tiles(x_ref, 0, MOE_RB, nc).astype(BF16)
        gate = jnp.dot(xb, wgb[...], preferred_element_type=F32)
        up = jnp.dot(xb, wub[...], preferred_element_type=F32)
        hid = (gate * _sigmoid(gate) * up).astype(BF16)
        _store_token_tiles(o_ref, jnp.dot(hid, wdb[...], preferred_element_type=F32))

    @pl.when(i >= nused)
    def _():
        o_ref[...] = jnp.zeros_like(o_ref)


def _moe_experts(block_e, nused, xg, wg, wu, wd, layer):
    nb = block_e.shape[0]
    d = wg.shape[2]
    nc = d // LANES
    de = wg.shape[3]
    grid_spec = pltpu.PrefetchScalarGridSpec(
        num_scalar_prefetch=2,
        grid=(nb,),
        in_specs=[pl.BlockSpec((MOE_RB * nc, LANES), lambda i, be, nu: (jnp.minimum(i, nu[0] - 1), 0)),
                  pl.BlockSpec((None, None, d, de), lambda i, be, nu: (layer, be[i], 0, 0)),
                  pl.BlockSpec((None, None, d, de), lambda i, be, nu: (layer, be[i], 0, 0)),
                  pl.BlockSpec((None, None, de, d), lambda i, be, nu: (layer, be[i], 0, 0))],
        out_specs=pl.BlockSpec((MOE_RB * nc, LANES), lambda i, be, nu: (i, 0)),
        scratch_shapes=[pltpu.VMEM((d, de), BF16), pltpu.VMEM((d, de), BF16), pltpu.VMEM((de, d), BF16)],
    )
    return pl.pallas_call(
        _moe_expert_kernel,
        grid_spec=grid_spec,
        out_shape=jax.ShapeDtypeStruct((nb * MOE_RB * nc, LANES), F32),
        compiler_params=_params(("arbitrary",)),
        name="moe_experts",
    )(block_e, nused, xg, wg, wu, wd)


def _moe_combine_kernel(pos_hbm, og_hbm, gates_ref, y0_ref, x_ref, mod_ref, g_ref, o_ref,
                        idx_smem, buf, idx_sem, row_sem, *, n_tiles):
    i = pl.program_id(0)
    slot = i % 2
    nxt = 1 - slot
    n_rows = TOP_K * DISP_TT
    nc = buf.shape[1] // n_rows

    def idx_copy(t, s):
        return pltpu.make_async_copy(pos_hbm.at[:, pl.ds(pl.multiple_of(t * DISP_TT, DISP_TT), DISP_TT)],
                                     idx_smem.at[s], idx_sem.at[s])

    def issue_rows(s):
        def body(r, carry):
            for k in range(TOP_K):
                p = idx_smem[s, k, r]
                pltpu.make_async_copy(og_hbm.at[pl.ds(pl.multiple_of(p * nc, nc), nc), :],
                                      buf.at[s, pl.ds(pl.multiple_of((k * DISP_TT + r) * nc, nc), nc), :],
                                      row_sem.at[s]).start()
            return carry
        lax.fori_loop(0, DISP_TT, body, 0, unroll=2)

    def wait_rows(s):
        pltpu.make_async_copy(og_hbm.at[pl.ds(0, n_rows * nc), :], buf.at[s], row_sem.at[s]).wait()

    @pl.when(i == 0)
    def _():
        idx_copy(0, 0).start()
        idx_copy(0, 0).wait()
        issue_rows(0)

        @pl.when(n_tiles > 1)
        def _():
            idx_copy(1, 1).start()

    @pl.when(i + 1 < n_tiles)
    def _():
        idx_copy(i + 1, nxt).wait()
        issue_rows(nxt)

    wait_rows(slot)

    @pl.when(i + 2 < n_tiles)
    def _():
        idx_copy(i + 2, slot).start()

    gates = gates_ref[...].T
    y = y0_ref[...]
    for k in range(TOP_K):
        y = y + gates[:, k:k + 1] * _load_token_tiles(buf, k * DISP_TT, DISP_TT, nc, lead=slot)
    ms = jnp.mean(y * y, axis=-1, keepdims=True)
    yn = y * lax.rsqrt(ms + RMS_EPS) * g_ref[...]
    o_ref[...] = x_ref[...] + mod_ref[0, 5:6, :] * yn


def _moe_combine(pos, og, gates, y0, xt, mod, g, n_tok, seq, batch):
    d = xt.shape[1]
    n_tiles = n_tok // DISP_TT
    row_spec = lambda w: pl.BlockSpec((DISP_TT, w), lambda i: (i, 0))
    return pl.pallas_call(
        functools.partial(_moe_combine_kernel, n_tiles=n_tiles),
        grid=(n_tiles,),
        in_specs=[pl.BlockSpec(memory_space=pl.ANY),
                  pl.BlockSpec(memory_space=pl.ANY),
                  pl.BlockSpec((TOP_K, DISP_TT), lambda i: (0, i)), row_spec(d), row_spec(d),
                  pl.BlockSpec((1, ADA_CHUNKS, d), _mod_index(DISP_TT, seq, batch)),
                  pl.BlockSpec((1, d), lambda i: (0, 0))],
        out_specs=row_spec(d),
        out_shape=jax.ShapeDtypeStruct(xt.shape, F32),
        input_output_aliases={4: 0},
        scratch_shapes=[pltpu.SMEM((2, TOP_K, DISP_TT), jnp.int32),
                        pltpu.VMEM((2, TOP_K * DISP_TT * (d // LANES), LANES), F32),
                        pltpu.SemaphoreType.DMA((2,)),
                        pltpu.SemaphoreType.DMA((2,))],
        compiler_params=_params(("arbitrary",)),
        name="moe_combine",
    )(pos, og, gates, y0, xt, mod, g.reshape(1, d))


def _dispatch_plan(counts, n_assign):
    e = counts.shape[0]
    padded = (counts + MOE_RB - 1) // MOE_RB * MOE_RB
    pend = jnp.cumsum(padded)
    pstart = pend - padded
    nb = -(-(n_assign + e * (MOE_RB - 1)) // MOE_RB)
    first_row = jnp.arange(nb, dtype=jnp.int32) * MOE_RB
    block_e = jnp.minimum(jnp.sum((pend[None, :] <= first_row[:, None]).astype(jnp.int32), axis=1), e - 1)
    nused = (pend[-1] // MOE_RB).astype(jnp.int32).reshape(1)
    lastblk = jnp.where(counts > 0, pend // MOE_RB - 1, -1)
    return pstart.astype(jnp.int32), lastblk.astype(jnp.int32), block_e.astype(jnp.int32), nused, nb


def _moe_layer(xt, mod, g_pre, g_post, router_w, router_b, w_gate, w_up, w_down, ws_gate, ws_up, ws_down,
               n_tok, seq, batch, layer):
    e = router_w.shape[1]
    rwt = router_w.astype(F32).T
    rwt_hi, rwt_lo = _split(rwt)
    h, y0, eidx, gates, rank, cnt = _moe_pre(xt, mod, g_pre, rwt_hi, rwt_lo, router_b.astype(F32).reshape(e, 1),
                                             ws_gate.astype(BF16), ws_up.astype(BF16), ws_down.astype(BF16),
                                             n_tok, seq, batch)
    pstart, lastblk, block_e, nused, nb = _dispatch_plan(cnt[:, 0], n_tok * TOP_K)
    pos = _moe_pos(eidx, rank, pstart.astype(F32).reshape(e, 1))
    xg = _moe_dispatch(lastblk, nused, pos, h, nb)
    og = _moe_experts(block_e, nused, xg, w_gate, w_up, w_down, layer)
    return _moe_combine(pos, og, gates, y0, xt, mod, g_post, n_tok, seq, batch)


def kernel(x, c, ctx, c_ctx, ada_w, ada_b, g_pre_mix, g_post_mix, g_pre_ffn, g_post_ffn, ab_w_in, ab_w_out, gla_dw_f, gla_db_f, gla_dw_b, gla_db_b, gla_norm_g, conv_w, conv_b, conv_ln_g, conv_ln_b, na_w_in, na_w_out, na_rpb, moe_router_w, moe_router_b, moe_w_gate, moe_w_up, moe_w_down, moe_ws_gate, moe_ws_up, moe_ws_down):
    batch, seq, d = x.shape
    ctx_len = ctx.shape[1]
    depth = ada_w.shape[0]
    n_x = batch * seq
    n_all = n_x + batch * ctx_len
    assert seq % ROW_TILE == 0 and (batch * ctx_len) % ROW_TILE == 0 and n_x % ctx_len == 0

    xt = jnp.concatenate([x.reshape(n_x, d), ctx.reshape(batch * ctx_len, d)], axis=0)

    cc = jnp.zeros((ADA_ROWS, d), F32).at[:batch].set(c).at[batch].set(c_ctx)
    mods = _ada_mod(cc, ada_w, ada_b)[:, :batch + 1].reshape(depth, batch + 1, ADA_CHUNKS, d)

    for layer in range(depth):
        last = layer == depth - 1
        i = layer // 2
        mod = mods[layer]
        if layer % 2 == 0:
            w_in = ab_w_in[i]
            cuts = np.cumsum([0, GLA_QK, GLA_QK, GLA_V, GLA_V, GLA_LOWRANK, GLA_LOWRANK, CONV_CH, CONV_CH])
            seg = lambda j: w_in[:, cuts[j]:cuts[j + 1]]
            lr_pad = jnp.zeros((d, LANES - 2 * GLA_LOWRANK), w_in.dtype)
            w_cat = jnp.concatenate([seg(0), seg(1), seg(2), seg(3), seg(6), seg(7), seg(4), seg(5), lr_pad],
                                    axis=1).astype(BF16)
            qk, v, g, glu, lr = _prenorm_proj(xt, mod, g_pre_mix[layer], w_cat,
                                              (2 * GLA_QK, GLA_V, GLA_V, 2 * CONV_CH, LANES), 0, 1, seq, batch)
            dwcat = jnp.zeros((LANES, 2 * GLA_QK), F32)
            dwcat = dwcat.at[:GLA_LOWRANK, :GLA_QK].set(gla_dw_f[i])
            dwcat = dwcat.at[GLA_LOWRANK:2 * GLA_LOWRANK, GLA_QK:].set(gla_dw_b[i]).astype(BF16)
            dbcat = jnp.concatenate([gla_db_f[i], gla_db_b[i]]).reshape(1, 2 * GLA_QK).astype(F32)
            a_out = _gla(qk, v, g, lr, dwcat, dbcat, gla_norm_g[i].reshape(1, GLA_V).astype(F32),
                         seq, ctx_len, batch)
            cw = conv_w[i].astype(F32)
            cvec = lambda t: t.reshape(1, CONV_CH).astype(F32)
            cv_x = _conv_module(glu, cw, cvec(conv_b[i]), cvec(conv_ln_g[i]), cvec(conv_ln_b[i]), seq, 0, batch)
            cv_c = _conv_module(glu, cw, cvec(conv_b[i]), cvec(conv_ln_g[i]), cvec(conv_ln_b[i]), ctx_len,
                                n_x // ctx_len, batch)
            cv = jnp.concatenate([cv_x, cv_c], axis=0)
            n_rows = n_x if last else n_all
            xt = _outproj([a_out, cv], ab_w_out[i].astype(BF16), xt, mod, g_post_mix[layer], 2, n_rows, seq, batch)
        else:
            (z,) = _prenorm_proj(xt, mod, g_pre_mix[layer], na_w_in[i].astype(BF16),
                                 (3 * NA_HEADS * NA_DH,), 0, 1, seq, batch)
            bias = _na_bias_table(na_rpb[i], seq // GRID_W)
            o = _na_attention(z, bias, seq, ctx_len, batch)
            if last:
                xt = _outproj([o], na_w_out[i].astype(BF16), xt, mod, g_post_mix[layer], 2, n_x, seq, batch)
            else:
                raise NotImplementedError("context update after a neighbourhood-attention layer")
        n_tok = n_x if last else n_all
        xt = _moe_layer(xt, mod, g_pre_ffn[layer], g_post_ffn[layer], moe_router_w[layer], moe_router_b[layer],
                        moe_w_gate, moe_w_up, moe_w_down, moe_ws_gate[layer],
                        moe_ws_up[layer], moe_ws_down[layer], n_tok, seq, batch, layer)
    return xt[:n_x].reshape(batch, seq, d)
```

```python
import functools

import numpy as np
import jax
import jax.numpy as jnp
from jax import lax
from jax.experimental import pallas as pl
from jax.experimental.pallas import tpu as pltpu

F32 = jnp.float32
BF16 = jnp.bfloat16

GRID_W = 64
RMS_EPS = 1e-6
LN_EPS = 1e-5
ADA_CHUNKS = 6
GLA_HEADS = 4
GLA_DK = 64
GLA_DV = 128
GLA_LOWRANK = 16
GLA_GATE_NORM = 16.0
GLA_CHUNK = 64
ROPE_BASE = 10000.0
GLA_QK = GLA_HEADS * GLA_DK
GLA_V = GLA_HEADS * GLA_DV
CONV_CH = 512
CONV_WIDTH = 31
NA_HEADS = 16
NA_DH = 64
NA_WIN_ROWS = 8
NA_WIN_COLS = 16
N_EXPERTS = 256
TOP_K = 8
N_GROUPS = 8
TOPK_GROUPS = 4
ROUTE_SCALE = 2.5

LANES = 128
ROW_TILE = 512
ADA_ROWS = 24
ADA_TN = 1536
CONV_RC = 32
CONV_HALO = 16
NA_ROW_UNROLL = 4
MOE_RB = 512
DISP_TT = 128
VMEM_LIMIT = 56 * 1024 * 1024

_NT = (((1,), (1,)), ((), ()))
_TN = (((0,), (0,)), ((), ()))


def _params(sem, **kw):
    return pltpu.CompilerParams(dimension_semantics=sem, vmem_limit_bytes=VMEM_LIMIT, **kw)


def _sigmoid(x):
    return 1.0 / (1.0 + jnp.exp(-x))


def _split(a):
    hi = a.astype(BF16)
    lo = (a - hi.astype(F32)).astype(BF16)
    return hi, lo


def _dot3(a, b_hi, b_lo, dims=(((1,), (0,)), ((), ()))):
    a_hi, a_lo = _split(a)
    d = lambda x, y: lax.dot_general(x, y, dims, preferred_element_type=F32)
    return d(a_hi, b_hi) + (d(a_hi, b_lo) + d(a_lo, b_hi))


def _prenorm_mod(x, g, mod_ref, shift_i, scale_i):
    ms = jnp.mean(x * x, axis=-1, keepdims=True)
    h = x * lax.rsqrt(ms + RMS_EPS) * g
    return h * (1.0 + mod_ref[0, scale_i:scale_i + 1, :]) + mod_ref[0, shift_i:shift_i + 1, :]


def _mod_index(tile_rows, seq, batch):
    per_batch = seq // tile_rows
    return lambda i, *_: (jnp.minimum(i // per_batch, batch), 0, 0)


def _ada_kernel(c_ref, w_ref, b_ref, o_ref):
    c = c_ref[...]
    s = c * _sigmoid(c)
    w_hi, w_lo = _split(w_ref[...])
    o_ref[...] = _dot3(s, w_hi, w_lo) + b_ref[...]


def _ada_mod(cc, ada_w, ada_b):
    depth, d, n = ada_w.shape
    return pl.pallas_call(
        _ada_kernel,
        grid=(depth, n // ADA_TN),
        in_specs=[pl.BlockSpec((ADA_ROWS, d), lambda l, j: (0, 0)),
                  pl.BlockSpec((None, d, ADA_TN), lambda l, j: (l, 0, j)),
                  pl.BlockSpec((None, 1, ADA_TN), lambda l, j: (l, 0, j))],
        out_specs=pl.BlockSpec((None, ADA_ROWS, ADA_TN), lambda l, j: (l, 0, j)),
        out_shape=jax.ShapeDtypeStruct((depth, ADA_ROWS, n), F32),
        compiler_params=_params(("arbitrary", "arbitrary")),
        name="ada_mod",
    )(cc, ada_w, ada_b.reshape(depth, 1, n))


def _prenorm_proj_kernel(x_ref, mod_ref, g_ref, w_ref, *o_refs, shift_i, scale_i):
    h = _prenorm_mod(x_ref[...], g_ref[...], mod_ref, shift_i, scale_i).astype(BF16)
    off = 0
    for o_ref in o_refs:
        n = o_ref.shape[-1]
        o_ref[...] = jnp.dot(h, w_ref[:, off:off + n], preferred_element_type=F32).astype(o_ref.dtype)
        off += n


def _prenorm_proj(xt, mod, g, w, splits, shift_i, scale_i, seq, batch):
    rows, d = xt.shape
    n = w.shape[1]
    return pl.pallas_call(
        functools.partial(_prenorm_proj_kernel, shift_i=shift_i, scale_i=scale_i),
        grid=(rows // ROW_TILE,),
        in_specs=[pl.BlockSpec((ROW_TILE, d), lambda i: (i, 0)),
                  pl.BlockSpec((1, ADA_CHUNKS, d), _mod_index(ROW_TILE, seq, batch)),
                  pl.BlockSpec((1, d), lambda i: (0, 0)),
                  pl.BlockSpec((d, n), lambda i: (0, 0))],
        out_specs=[pl.BlockSpec((ROW_TILE, s), lambda i: (i, 0)) for s in splits],
        out_shape=[jax.ShapeDtypeStruct((rows, s), BF16) for s in splits],
        compiler_params=_params(("arbitrary",)),
        name="prenorm_proj",
    )(xt, mod, g.reshape(1, d), w)


def _gla_kernel(qkx_ref, vx_ref, gx_ref, lrx_ref, qkc_ref, vc_ref, gc_ref, lrc_ref,
                dw_ref, db_ref, cos_ref, sin_ref, gg_ref, ox_ref, oc_ref,
                sf_ref, sb_ref, accx_ref, accc_ref, *, n_x, n_c):
    L = GLA_CHUNK
    row = lax.broadcasted_iota(jnp.int32, (L, L), 0)
    col = lax.broadcasted_iota(jnp.int32, (L, L), 1)
    lower = row >= col
    upper = col >= row
    tri_f = jnp.where(lower, 1.0, 0.0).astype(BF16)
    tri_b = jnp.where(upper, 1.0, 0.0).astype(BF16)
    lane = lax.broadcasted_iota(jnp.int32, (1, GLA_QK), 1)
    head_of_lane = lane // GLA_DK
    first_half = (lane % (GLA_DK // 2)) < (GLA_DK // 4)
    bd_mask = (lax.broadcasted_iota(jnp.int32, (GLA_V, GLA_QK), 0) // GLA_DV
               == lax.broadcasted_iota(jnp.int32, (GLA_V, GLA_QK), 1) // GLA_DK)

    def swap_pairs(t):
        nf = GLA_DK // 4
        return jnp.where(first_half, pltpu.roll(t, GLA_QK - nf, 1), pltpu.roll(t, nf, 1))

    def chunk(fwd, c, qk_ref, v_ref, lr_ref, latent):
        rows = pl.ds(pl.multiple_of(c * L, L), L)
        q = qk_ref[rows, 0:GLA_QK].astype(F32) * (GLA_DK ** -0.5)
        k = qk_ref[rows, GLA_QK:2 * GLA_QK].astype(F32)
        if latent:
            cs = cos_ref[rows, :]
            sn = sin_ref[rows, :]
            q = q * cs + swap_pairs(q) * sn
            k = k * cs + swap_pairs(k) * sn
        dsl = slice(0, GLA_QK) if fwd else slice(GLA_QK, 2 * GLA_QK)
        pre = jnp.dot(lr_ref[rows, :], dw_ref[:, dsl], preferred_element_type=F32) + db_ref[:, dsl]
        logd = (jnp.minimum(pre, 0.0) - jnp.log(1.0 + jnp.exp(-jnp.abs(pre)))) * (1.0 / GLA_GATE_NORM)
        d_hi, d_lo = _split(logd)
        tri = tri_f if fwd else tri_b
        b = (jnp.dot(tri, d_hi, preferred_element_type=F32)
             + jnp.dot(tri, d_lo, preferred_element_type=F32))
        mid = L // 2 - 1 if fwd else L // 2
        end = L - 1 if fwd else 0
        b_mid = b[mid:mid + 1, :]
        b_end = b[end:end + 1, :]
        qe = q * jnp.exp(b - b_mid)
        ke = (k * jnp.exp(b_mid - b)).astype(BF16)
        q_in = (q * jnp.exp(b)).astype(BF16)
        k_out = (k * jnp.exp(b_end - b)).astype(BF16)
        dec = jnp.exp(b_end)
        vv = v_ref[rows, :]
        mask = lower if fwd else upper
        outs = []
        for h in range(GLA_HEADS):
            qm = jnp.where(head_of_lane == h, qe, 0.0).astype(BF16)
            a = lax.dot_general(qm, ke, _NT, preferred_element_type=F32)
            a = jnp.where(mask, a, 0.0).astype(BF16)
            outs.append(jnp.dot(a, vv[:, h * GLA_DV:(h + 1) * GLA_DV], preferred_element_type=F32))
        o_intra = jnp.concatenate(outs, axis=-1)
        s_ref = sf_ref if fwd else sb_ref
        st = s_ref[...]
        o_inter = lax.dot_general(q_in, st.astype(BF16), _NT, preferred_element_type=F32)
        upd = lax.dot_general(vv, k_out, _TN, preferred_element_type=F32)
        s_ref[...] = jnp.where(bd_mask, st * dec + upd, 0.0)
        return o_intra + o_inter

    def finish(o, g_ref, c):
        rows = pl.ds(pl.multiple_of(c * L, L), L)
        g = g_ref[rows, :].astype(F32)
        parts = []
        for h in range(GLA_HEADS):
            oh = o[:, h * GLA_DV:(h + 1) * GLA_DV]
            ms = jnp.mean(oh * oh, axis=-1, keepdims=True)
            parts.append(oh * lax.rsqrt(ms + RMS_EPS))
        on = jnp.concatenate(parts, axis=-1) * gg_ref[...]
        return (on * (g * _sigmoid(g))).astype(BF16)

    def run_seq(n, qk_ref, v_ref, g_ref, lr_ref, acc_ref, o_ref, latent):
        half = n // 2

        def rows_of(c):
            return pl.ds(pl.multiple_of(c * L, L), L)

        def first(i, carry):
            cf, cb = i, n - 1 - i
            acc_ref[rows_of(cf), :] = chunk(True, cf, qk_ref, v_ref, lr_ref, latent)
            acc_ref[rows_of(cb), :] = chunk(False, cb, qk_ref, v_ref, lr_ref, latent)
            return carry

        def second(i, carry):
            cf, cb = i, n - 1 - i
            of = acc_ref[rows_of(cf), :] + chunk(True, cf, qk_ref, v_ref, lr_ref, latent)
            o_ref[rows_of(cf), :] = finish(of, g_ref, cf)
            ob = acc_ref[rows_of(cb), :] + chunk(False, cb, qk_ref, v_ref, lr_ref, latent)
            o_ref[rows_of(cb), :] = finish(ob, g_ref, cb)
            return carry

        lax.fori_loop(0, half, first, 0)
        lax.fori_loop(half, n, second, 0)

    sf_ref[...] = jnp.zeros_like(sf_ref)
    sb_ref[...] = jnp.zeros_like(sb_ref)
    run_seq(n_c, qkc_ref, vc_ref, gc_ref, lrc_ref, accc_ref, oc_ref, False)
    run_seq(n_x, qkx_ref, vx_ref, gx_ref, lrx_ref, accx_ref, ox_ref, True)


def _rope_tables(seq):
    half = GLA_DK // 2
    nf = half // 2
    inv = ROPE_BASE ** (-jnp.arange(nf, dtype=F32) / nf)
    t = jnp.arange(seq)
    rows = (t // GRID_W).astype(F32)
    cols = (t % GRID_W).astype(F32)
    ar = rows[:, None] * inv[None, :]
    ac = cols[:, None] * inv[None, :]
    cos_h = jnp.concatenate([jnp.cos(ar), jnp.cos(ar), jnp.cos(ac), jnp.cos(ac)], axis=-1)
    sin_h = jnp.concatenate([-jnp.sin(ar), jnp.sin(ar), -jnp.sin(ac), jnp.sin(ac)], axis=-1)
    return jnp.tile(cos_h, (1, GLA_HEADS)), jnp.tile(sin_h, (1, GLA_HEADS))


def _gla(qk, v, g, lr, dwcat, dbcat, gla_g, seq, ctx_len, batch):
    rows = qk.shape[0]
    cos_t, sin_t = _rope_tables(seq)
    cb = (batch * seq) // ctx_len
    xs = lambda w: pl.BlockSpec((seq, w), lambda b: (b, 0))
    cs = lambda w: pl.BlockSpec((ctx_len, w), lambda b: (cb + b, 0))
    full = lambda a: pl.BlockSpec(a.shape, lambda b: (0,) * a.ndim)
    out_x, out_c = pl.pallas_call(
        functools.partial(_gla_kernel, n_x=seq // GLA_CHUNK, n_c=ctx_len // GLA_CHUNK),
        grid=(batch,),
        in_specs=[xs(2 * GLA_QK), xs(GLA_V), xs(GLA_V), xs(LANES),
                  cs(2 * GLA_QK), cs(GLA_V), cs(GLA_V), cs(LANES),
                  full(dwcat), full(dbcat), full(cos_t), full(sin_t), full(gla_g)],
        out_specs=[pl.BlockSpec((seq, GLA_V), lambda b: (b, 0)),
                   pl.BlockSpec((ctx_len, GLA_V), lambda b: (b, 0))],
        out_shape=[jax.ShapeDtypeStruct((batch * seq, GLA_V), BF16),
                   jax.ShapeDtypeStruct((batch * ctx_len, GLA_V), BF16)],
        scratch_shapes=[pltpu.VMEM((GLA_V, GLA_QK), F32), pltpu.VMEM((GLA_V, GLA_QK), F32),
                        pltpu.VMEM((seq, GLA_V), F32), pltpu.VMEM((ctx_len, GLA_V), F32)],
        compiler_params=_params(("arbitrary",)),
        name="gla",
    )(qk, v, g, lr, qk, v, g, lr, dwcat, dbcat, cos_t, sin_t, gla_g)
    del rows
    return jnp.concatenate([out_x, out_c], axis=0)


def _conv_kernel(glu_ref, w_ref, cb_ref, lg_ref, lb_ref, o_ref, pad_ref, win_ref, *, t_len):
    ch = CONV_CH
    zeros = jnp.zeros((CONV_HALO, ch), F32)
    pad_ref[0:CONV_HALO, :] = zeros
    pad_ref[CONV_HALO + t_len:2 * CONV_HALO + t_len, :] = zeros
    fill = 128

    def fill_body(i, carry):
        r = pl.multiple_of(i * fill, fill)
        a = glu_ref[pl.ds(r, fill), 0:ch].astype(F32)
        b = glu_ref[pl.ds(r, fill), ch:2 * ch].astype(F32)
        pad_ref[pl.ds(CONV_HALO + r, fill), :] = a * _sigmoid(b)
        return carry

    lax.fori_loop(0, t_len // fill, fill_body, 0)
    first_tap = CONV_HALO - CONV_WIDTH // 2

    def body(i, carry):
        base = pl.multiple_of(i * CONV_RC, CONV_RC)
        win_ref[...] = pad_ref[pl.ds(base, CONV_RC + 2 * CONV_HALO), :]
        acc = jnp.zeros((CONV_RC, ch), F32)
        for j in range(CONV_WIDTH):
            acc = acc + w_ref[j:j + 1, :] * win_ref[first_tap + j:first_tap + j + CONV_RC, :]
        acc = acc + cb_ref[...]
        mu = jnp.mean(acc, axis=-1, keepdims=True)
        cen = acc - mu
        var = jnp.mean(cen * cen, axis=-1, keepdims=True)
        y = cen * lax.rsqrt(var + LN_EPS) * lg_ref[...] + lb_ref[...]
        o_ref[pl.ds(base, CONV_RC), :] = (y * _sigmoid(y)).astype(BF16)
        return carry

    lax.fori_loop(0, t_len // CONV_RC, body, 0)


def _conv_module(glu, conv_w, conv_b, ln_g, ln_b, t_len, first_block, batch):
    full = lambda a: pl.BlockSpec(a.shape, lambda b: (0,) * a.ndim)
    return pl.pallas_call(
        functools.partial(_conv_kernel, t_len=t_len),
        grid=(batch,),
        in_specs=[pl.BlockSpec((t_len, 2 * CONV_CH), lambda b: (first_block + b, 0)),
                  full(conv_w), full(conv_b), full(ln_g), full(ln_b)],
        out_specs=pl.BlockSpec((t_len, CONV_CH), lambda b: (b, 0)),
        out_shape=jax.ShapeDtypeStruct((batch * t_len, CONV_CH), BF16),
        scratch_shapes=[pltpu.VMEM((t_len + 2 * CONV_HALO, CONV_CH), F32),
                        pltpu.VMEM((CONV_RC + 2 * CONV_HALO, CONV_CH), F32)],
        compiler_params=_params(("arbitrary",)),
        name="conv_module",
    )(glu, conv_w, conv_b, ln_g, ln_b)


def _outproj_kernel(*refs, n_lhs, gate_i):
    lhs_refs = refs[:n_lhs]
    w_ref, x_ref, mod_ref, g_ref, o_ref = refs[n_lhs:]
    off = 0
    y = None
    for l_ref in lhs_refs:
        kk = l_ref.shape[-1]
        t = jnp.dot(l_ref[...], w_ref[off:off + kk, :], preferred_element_type=F32)
        y = t if y is None else y + t
        off += kk
    ms = jnp.mean(y * y, axis=-1, keepdims=True)
    yn = y * lax.rsqrt(ms + RMS_EPS) * g_ref[...]
    o_ref[...] = x_ref[...] + mod_ref[0, gate_i:gate_i + 1, :] * yn


def _outproj(lhs_list, w, xt, mod, g, gate_i, n_rows, seq, batch):
    d = xt.shape[1]
    n_lhs = len(lhs_list)
    in_specs = [pl.BlockSpec((ROW_TILE, a.shape[1]), lambda i: (i, 0)) for a in lhs_list]
    in_specs += [pl.BlockSpec(w.shape, lambda i: (0, 0)),
                 pl.BlockSpec((ROW_TILE, d), lambda i: (i, 0)),
                 pl.BlockSpec((1, ADA_CHUNKS, d), _mod_index(ROW_TILE, seq, batch)),
                 pl.BlockSpec((1, d), lambda i: (0, 0))]
    return pl.pallas_call(
        functools.partial(_outproj_kernel, n_lhs=n_lhs, gate_i=gate_i),
        grid=(n_rows // ROW_TILE,),
        in_specs=in_specs,
        out_specs=pl.BlockSpec((ROW_TILE, d), lambda i: (i, 0)),
        out_shape=jax.ShapeDtypeStruct(xt.shape, F32),
        input_output_aliases={n_lhs + 1: 0},
        compiler_params=_params(("arbitrary",)),
        name="outproj",
    )(*lhs_list, w, xt, mod, g.reshape(1, d))


def _na_kernel(q_ref, k_ref, v_ref, kc_ref, vc_ref, bias_ref, o_ref, *, n_rows):
    wr = NA_WIN_ROWS
    lane = lax.broadcasted_iota(jnp.int32, (1, LANES), 1)
    head0 = lane < NA_DH
    kc = kc_ref[...]
    vc = vc_ref[...]

    def body(r, carry):
        r0 = jnp.clip(r - wr // 2, 0, n_rows - wr)
        delta = r - r0
        q2 = q_ref[pl.ds(pl.multiple_of(r * GRID_W, GRID_W), GRID_W), :] * (NA_DH ** -0.5)
        kw = k_ref[pl.ds(pl.multiple_of(r0 * GRID_W, GRID_W), wr * GRID_W), :]
        vw = v_ref[pl.ds(pl.multiple_of(r0 * GRID_W, GRID_W), wr * GRID_W), :]
        zero = jnp.zeros_like(q2)
        qs = jnp.concatenate([jnp.where(head0, q2, zero), jnp.where(head0, zero, q2)], axis=0)
        bias = bias_ref[:, delta].reshape(2 * GRID_W, wr * GRID_W)
        s_w = lax.dot_general(qs, kw, _NT, preferred_element_type=F32) + bias
        s_c = lax.dot_general(qs, kc, _NT, preferred_element_type=F32)
        m = jnp.maximum(jnp.max(s_w, axis=-1, keepdims=True), jnp.max(s_c, axis=-1, keepdims=True))
        p_w = jnp.exp(s_w - m)
        p_c = jnp.exp(s_c - m)
        den = jnp.sum(p_w, axis=-1, keepdims=True) + jnp.sum(p_c, axis=-1, keepdims=True)
        o = (jnp.dot(p_w.astype(BF16), vw, preferred_element_type=F32)
             + jnp.dot(p_c.astype(BF16), vc, preferred_element_type=F32)) / den
        o_ref[pl.ds(pl.multiple_of(r * GRID_W, GRID_W), GRID_W), :] = (
            jnp.where(head0, o[:GRID_W], o[GRID_W:]).astype(BF16))
        return carry

    lax.fori_loop(0, n_rows, body, 0, unroll=NA_ROW_UNROLL)


def _na_bias_table(rpb, n_rows):
    wr = min(NA_WIN_ROWS, n_rows)
    qc = np.arange(GRID_W)
    cstart = np.clip(qc - NA_WIN_COLS // 2, 0, GRID_W - NA_WIN_COLS)
    kcol = np.arange(GRID_W)
    ok = (kcol[None, :] >= cstart[:, None]) & (kcol[None, :] < cstart[:, None] + NA_WIN_COLS)
    cidx = np.clip(kcol[None, :] - qc[:, None] + NA_WIN_COLS - 1, 0, 2 * NA_WIN_COLS - 2)
    delta = np.arange(wr)
    j = np.arange(wr)
    ridx = j[None, :] - delta[:, None] + NA_WIN_ROWS - 1
    t = rpb[:, ridx][:, :, :, cidx]
    t = jnp.where(jnp.asarray(ok)[None, None, None], t.astype(F32), -jnp.inf)
    t = t.transpose(0, 1, 3, 2, 4)
    return t.reshape(rpb.shape[0], wr, GRID_W, wr * GRID_W)


def _na_attention(z, bias, seq, ctx_len, batch):
    n_pairs = NA_HEADS * NA_DH // LANES
    n_rows = seq // GRID_W
    cb = (batch * seq) // ctx_len
    return pl.pallas_call(
        functools.partial(_na_kernel, n_rows=n_rows),
        grid=(n_pairs, batch),
        in_specs=[pl.BlockSpec((seq, LANES), lambda p, b: (b, p)),
                  pl.BlockSpec((seq, LANES), lambda p, b: (b, n_pairs + p)),
                  pl.BlockSpec((seq, LANES), lambda p, b: (b, 2 * n_pairs + p)),
                  pl.BlockSpec((ctx_len, LANES), lambda p, b: (cb + b, n_pairs + p)),
                  pl.BlockSpec((ctx_len, LANES), lambda p, b: (cb + b, 2 * n_pairs + p)),
                  pl.BlockSpec((2,) + bias.shape[1:], lambda p, b: (p, 0, 0, 0))],
        out_specs=pl.BlockSpec((seq, LANES), lambda p, b: (b, p)),
        out_shape=jax.ShapeDtypeStruct((batch * seq, NA_HEADS * NA_DH), BF16),
        compiler_params=_params(("arbitrary", "arbitrary")),
        name="na_attention",
    )(z, z, z, z, z, bias)


def _store_token_tiles(ref, val, row0=0):
    n, d = val.shape
    nc = d // LANES
    for k in range(nc):
        ref[pl.ds(row0 * nc + k, n, stride=nc), :] = val[:, k * LANES:(k + 1) * LANES]


def _load_token_tiles(ref, row0, n, nc, lead=None):
    chunks = []
    for k in range(nc):
        idx = pl.ds(row0 * nc + k, n, stride=nc)
        chunks.append(ref[idx, :] if lead is None else ref[lead, idx, :])
    return jnp.concatenate(chunks, axis=-1)


def _first_argmax(x, iota_f, n):
    m = jnp.max(x, axis=0, keepdims=True)
    idx = jnp.min(jnp.where(x == m, iota_f, float(n)), axis=0, keepdims=True)
    return m, idx


def _moe_pre_kernel(x_ref, mod_ref, g_ref, rwh_ref, rwl_ref, rb_ref, wsg_ref, wsu_ref, wsd_ref,
                    h_ref, y0_ref, eidx_ref, gates_ref, rank_ref, cnt_ref, base_ref):
    tm = x_ref.shape[0]
    n_e = rwh_ref.shape[0]

    @pl.when(pl.program_id(0) == 0)
    def _():
        base_ref[...] = jnp.zeros_like(base_ref)

    h = _prenorm_mod(x_ref[...], g_ref[...], mod_ref, 3, 4)
    _store_token_tiles(h_ref, h)
    hb = h.astype(BF16)
    gate = jnp.dot(hb, wsg_ref[...], preferred_element_type=F32)
    up = jnp.dot(hb, wsu_ref[...], preferred_element_type=F32)
    hid = (gate * _sigmoid(gate) * up).astype(BF16)
    y0_ref[...] = jnp.dot(hid, wsd_ref[...], preferred_element_type=F32)

    h_lo = (h - hb.astype(F32)).astype(BF16)
    d = lambda a, b: lax.dot_general(a, b, _NT, preferred_element_type=F32)
    logits = d(rwh_ref[...], hb) + (d(rwh_ref[...], h_lo) + d(rwl_ref[...], hb))
    scores = _sigmoid(logits)
    biased = scores + rb_ref[...]
    neg = -jnp.inf
    gsz = n_e // N_GROUPS
    io_g = lax.broadcasted_iota(jnp.int32, (gsz, tm), 0).astype(F32)
    grp_rows = []
    for gi in range(N_GROUPS):
        xg = biased[gi * gsz:(gi + 1) * gsz]
        m1, i1 = _first_argmax(xg, io_g, gsz)
        m2 = jnp.max(jnp.where(io_g == i1, neg, xg), axis=0, keepdims=True)
        grp_rows.append(m1 + m2)
    cur = jnp.concatenate(grp_rows, axis=0)
    io_n = lax.broadcasted_iota(jnp.int32, (N_GROUPS, tm), 0).astype(F32)
    keep = jnp.zeros((N_GROUPS, tm), F32)
    for _ in range(TOPK_GROUPS):
        _, ii = _first_argmax(cur, io_n, N_GROUPS)
        sel = io_n == ii
        keep = jnp.where(sel, 1.0, keep)
        cur = jnp.where(sel, neg, cur)
    cur = jnp.concatenate([jnp.where(keep[gi:gi + 1] > 0.0, biased[gi * gsz:(gi + 1) * gsz], neg)
                           for gi in range(N_GROUPS)], axis=0)
    io_e = lax.broadcasted_iota(jnp.int32, (n_e, tm), 0).astype(F32)
    idxs, gvals = [], []
    chosen = jnp.zeros((n_e, tm), F32)
    for _ in range(TOP_K):
        _, ii = _first_argmax(cur, io_e, n_e)
        sel = io_e == ii
        idxs.append(ii)
        gvals.append(jnp.sum(jnp.where(sel, scores, 0.0), axis=0, keepdims=True))
        chosen = jnp.where(sel, 1.0, chosen)
        cur = jnp.where(sel, neg, cur)
    gv = jnp.concatenate(gvals, axis=0)
    gates_ref[...] = gv / jnp.sum(gv, axis=0, keepdims=True) * ROUTE_SCALE
    eidx_ref[...] = jnp.concatenate(idxs, axis=0).astype(jnp.int32)

    earlier = (lax.broadcasted_iota(jnp.int32, (tm, tm), 0) < lax.broadcasted_iota(jnp.int32, (tm, tm), 1))
    before = jnp.dot(chosen.astype(BF16), jnp.where(earlier, 1.0, 0.0).astype(BF16), preferred_element_type=F32)
    before = before + base_ref[:, 0:1]
    ranks = [jnp.sum(jnp.where(io_e == ii, before, 0.0), axis=0, keepdims=True) for ii in idxs]
    rank_ref[...] = jnp.concatenate(ranks, axis=0).astype(jnp.int32)
    base_ref[...] = base_ref[...] + jnp.sum(chosen, axis=1, keepdims=True)
    cnt_ref[...] = base_ref[...].astype(jnp.int32)


def _moe_pre(xt, mod, g, rwt_hi, rwt_lo, rb, wsg, wsu, wsd, n_tok, seq, batch):
    d = xt.shape[1]
    e = rwt_hi.shape[0]
    full = lambda a: pl.BlockSpec(a.shape, lambda i: (0,) * a.ndim)
    tok_major = lambda: pl.BlockSpec((TOP_K, ROW_TILE), lambda i: (0, i))
    return pl.pallas_call(
        _moe_pre_kernel,
        grid=(n_tok // ROW_TILE,),
        in_specs=[pl.BlockSpec((ROW_TILE, d), lambda i: (i, 0)),
                  pl.BlockSpec((1, ADA_CHUNKS, d), _mod_index(ROW_TILE, seq, batch)),
                  pl.BlockSpec((1, d), lambda i: (0, 0)),
                  full(rwt_hi), full(rwt_lo), full(rb), full(wsg), full(wsu), full(wsd)],
        out_specs=[pl.BlockSpec((ROW_TILE * (d // LANES), LANES), lambda i: (i, 0)),
                   pl.BlockSpec((ROW_TILE, d), lambda i: (i, 0)),
                   tok_major(), tok_major(), tok_major(),
                   pl.BlockSpec((e, LANES), lambda i: (0, 0))],
        out_shape=[jax.ShapeDtypeStruct((n_tok * (d // LANES), LANES), F32),
                   jax.ShapeDtypeStruct((n_tok, d), F32),
                   jax.ShapeDtypeStruct((TOP_K, n_tok), jnp.int32),
                   jax.ShapeDtypeStruct((TOP_K, n_tok), F32),
                   jax.ShapeDtypeStruct((TOP_K, n_tok), jnp.int32),
                   jax.ShapeDtypeStruct((e, LANES), jnp.int32)],
        scratch_shapes=[pltpu.VMEM((e, LANES), F32)],
        compiler_params=_params(("arbitrary",)),
        name="moe_pre",
    )(xt, mod, g.reshape(1, d), rwt_hi, rwt_lo, rb, wsg, wsu, wsd)


def _moe_pos_kernel(eidx_ref, rank_ref, pstart_ref, pos_ref):
    n_e = pstart_ref.shape[0]
    tp = eidx_ref.shape[1]
    io_e = lax.broadcasted_iota(jnp.int32, (n_e, tp), 0)
    ps = pstart_ref[...]
    e = eidx_ref[...]
    rows = [jnp.sum(jnp.where(io_e == e[k:k + 1, :], ps, 0.0), axis=0, keepdims=True) for k in range(TOP_K)]
    pos_ref[...] = rank_ref[...] + jnp.concatenate(rows, axis=0).astype(jnp.int32)


def _moe_pos(eidx, rank, pstart):
    n_tok = eidx.shape[1]
    blk = lambda: pl.BlockSpec((TOP_K, ROW_TILE), lambda i: (0, i))
    return pl.pallas_call(
        _moe_pos_kernel,
        grid=(n_tok // ROW_TILE,),
        in_specs=[blk(), blk(), pl.BlockSpec(pstart.shape, lambda i: (0, 0))],
        out_specs=blk(),
        out_shape=jax.ShapeDtypeStruct(eidx.shape, jnp.int32),
        compiler_params=_params(("arbitrary",)),
        name="moe_pos",
    )(eidx, rank, pstart)


def _moe_dispatch_kernel(lastblk_ref, nused_ref, pos_hbm, h_hbm, xg_hbm, idx_smem, zbuf, hbuf, idx_sem, row_sem, z_sem,
                         h_sem, *, n_tiles, nc, n_blocks):
    i = pl.program_id(0)
    slot = i % 2
    n_e = lastblk_ref.shape[0]
    rows_per_step = DISP_TT * TOP_K

    def h_copy(t, s):
        return pltpu.make_async_copy(h_hbm.at[pl.ds(pl.multiple_of(t * (DISP_TT * nc), DISP_TT * nc), DISP_TT * nc), :],
                                     hbuf.at[s], h_sem.at[s])

    def idx_copy(t, s):
        return pltpu.make_async_copy(pos_hbm.at[:, pl.ds(pl.multiple_of(t * DISP_TT, DISP_TT), DISP_TT)],
                                     idx_smem.at[s], idx_sem.at[s])

    def block_copy(b):
        return pltpu.make_async_copy(zbuf, xg_hbm.at[pl.ds(pl.multiple_of(b * (MOE_RB * nc), MOE_RB * nc),
                                                           MOE_RB * nc), :], z_sem)

    def wait_rows(s):
        pltpu.make_async_copy(h_hbm.at[pl.ds(0, rows_per_step * nc), :], xg_hbm.at[pl.ds(0, rows_per_step * nc), :],
                              row_sem.at[s]).wait()

    @pl.when(i == 0)
    def _():
        idx_copy(0, 0).start()
        zbuf[...] = jnp.zeros_like(zbuf)

        def for_zero_blocks(fn):
            def per_expert(e, carry):
                @pl.when(lastblk_ref[e] >= 0)
                def _():
                    fn(block_copy(lastblk_ref[e]))
                return carry

            def per_block(b, carry):
                fn(block_copy(b))
                return carry

            lax.fori_loop(0, n_e, per_expert, 0)
            lax.fori_loop(nused_ref[0], n_blocks, per_block, 0)

        for_zero_blocks(lambda cp: cp.start())
        for_zero_blocks(lambda cp: cp.wait())

        h_copy(0, 0).start()

    idx_copy(i, slot).wait()
    hslot = i % 3
    h_copy(i, hslot).wait()

    @pl.when(i + 1 < n_tiles)
    def _():
        idx_copy(i + 1, 1 - slot).start()
        h_copy(i + 1, (i + 1) % 3).start()

    def body(r, carry):
        src = hbuf.at[hslot, pl.ds(pl.multiple_of(r * nc, nc), nc), :]
        for k in range(TOP_K):
            p = idx_smem[slot, k, r]
            pltpu.make_async_copy(src, xg_hbm.at[pl.ds(pl.multiple_of(p * nc, nc), nc), :], row_sem.at[slot]).start()
        return carry

    lax.fori_loop(0, DISP_TT, body, 0, unroll=2)

    @pl.when(i > 0)
    def _():
        wait_rows(1 - slot)

    @pl.when(i == n_tiles - 1)
    def _():
        wait_rows(slot)


def _moe_dispatch(lastblk, nused, pos, h, n_blocks):
    n_tok = pos.shape[1]
    nc = h.shape[0] // n_tok
    n_tiles = n_tok // DISP_TT
    grid_spec = pltpu.PrefetchScalarGridSpec(
        num_scalar_prefetch=2,
        grid=(n_tiles,),
        in_specs=[pl.BlockSpec(memory_space=pl.ANY), pl.BlockSpec(memory_space=pl.ANY)],
        out_specs=pl.BlockSpec(memory_space=pl.ANY),
        scratch_shapes=[pltpu.SMEM((2, TOP_K, DISP_TT), jnp.int32),
                        pltpu.VMEM((MOE_RB * nc, LANES), F32),
                        pltpu.VMEM((3, DISP_TT * nc, LANES), F32),
                        pltpu.SemaphoreType.DMA((2,)),
                        pltpu.SemaphoreType.DMA((2,)),
                        pltpu.SemaphoreType.DMA(()),
                        pltpu.SemaphoreType.DMA((3,))],
    )
    return pl.pallas_call(
        functools.partial(_moe_dispatch_kernel, n_tiles=n_tiles, nc=nc, n_blocks=n_blocks),
        grid_spec=grid_spec,
        out_shape=jax.ShapeDtypeStruct((n_blocks * MOE_RB * nc, LANES), F32),
        compiler_params=_params(("arbitrary",)),
        name="moe_dispatch",
    )(lastblk, nused, pos, h)


def _moe_expert_kernel(be_ref, nused_ref, x_ref, wg_ref, wu_ref, wd_ref, o_ref, wgb, wub, wdb):
    i = pl.program_id(0)
    nused = nused_ref[0]
    nc = x_ref.shape[0] // MOE_RB
    new_expert = jnp.logical_or(i == 0, be_ref[i] != be_ref[jnp.maximum(i - 1, 0)])

    @pl.when(jnp.logical_and(i < nused, new_expert))
    def _():
        wgb[...] = wg_ref[...].astype(BF16)
        wub[...] = wu_ref[...].astype(BF16)
        wdb[...] = wd_ref[...].astype(BF16)

    @pl.when(i < nused)
    def _():
        xb = _load_token_tiles(x_ref, 0, MOE_RB, nc).astype(BF16)
        gate = jnp.dot(xb, wgb[...], preferred_element_type=F32)
        up = jnp.dot(xb, wub[...], preferred_element_type=F32)
        hid = (gate * _sigmoid(gate) * up).astype(BF16)
        _store_token_tiles(o_ref, jnp.dot(hid, wdb[...], preferred_element_type=F32))

    @pl.when(i >= nused)
    def _():
        o_ref[...] = jnp.zeros_like(o_ref)


def _moe_experts(block_e, nused, xg, wg, wu, wd, layer):
    nb = block_e.shape[0]
    d = wg.shape[2]
    nc = d // LANES
    de = wg.shape[3]
    grid_spec = pltpu.PrefetchScalarGridSpec(
        num_scalar_prefetch=2,
        grid=(nb,),
        in_specs=[pl.BlockSpec((MOE_RB * nc, LANES), lambda i, be, nu: (jnp.minimum(i, nu[0] - 1), 0)),
                  pl.BlockSpec((None, None, d, de), lambda i, be, nu: (layer, be[i], 0, 0)),
                  pl.BlockSpec((None, None, d, de), lambda i, be, nu: (layer, be[i], 0, 0)),
                  pl.BlockSpec((None, None, de, d), lambda i, be, nu: (layer, be[i], 0, 0))],
        out_specs=pl.BlockSpec((MOE_RB * nc, LANES), lambda i, be, nu: (i, 0)),
        scratch_shapes=[pltpu.VMEM((d, de), BF16), pltpu.VMEM((d, de), BF16), pltpu.VMEM((de, d), BF16)],
    )
    return pl.pallas_call(
        _moe_expert_kernel,
        grid_spec=grid_spec,
        out_shape=jax.ShapeDtypeStruct((nb * MOE_RB * nc, LANES), F32),
        compiler_params=_params(("arbitrary",)),
        name="moe_experts",
    )(block_e, nused, xg, wg, wu, wd)


def _moe_combine_kernel(pos_hbm, og_hbm, gates_ref, y0_ref, x_ref, mod_ref, g_ref, o_ref,
                        idx_smem, buf, idx_sem, row_sem, *, n_tiles):
    i = pl.program_id(0)
    slot = i % 2
    nxt = 1 - slot
    n_rows = TOP_K * DISP_TT
    nc = buf.shape[1] // n_rows

    def idx_copy(t, s):
        return pltpu.make_async_copy(pos_hbm.at[:, pl.ds(pl.multiple_of(t * DISP_TT, DISP_TT), DISP_TT)],
                                     idx_smem.at[s], idx_sem.at[s])

    def issue_rows(s):
        def body(r, carry):
            for k in range(TOP_K):
                p = idx_smem[s, k, r]
                pltpu.make_async_copy(og_hbm.at[pl.ds(pl.multiple_of(p * nc, nc), nc), :],
                                      buf.at[s, pl.ds(pl.multiple_of((k * DISP_TT + r) * nc, nc), nc), :],
                                      row_sem.at[s]).start()
            return carry
        lax.fori_loop(0, DISP_TT, body, 0, unroll=2)

    def wait_rows(s):
        pltpu.make_async_copy(og_hbm.at[pl.ds(0, n_rows * nc), :], buf.at[s], row_sem.at[s]).wait()

    @pl.when(i == 0)
    def _():
        idx_copy(0, 0).start()
        idx_copy(0, 0).wait()
        issue_rows(0)

        @pl.when(n_tiles > 1)
        def _():
            idx_copy(1, 1).start()

    @pl.when(i + 1 < n_tiles)
    def _():
        idx_copy(i + 1, nxt).wait()
        issue_rows(nxt)

    wait_rows(slot)

    @pl.when(i + 2 < n_tiles)
    def _():
        idx_copy(i + 2, slot).start()

    gates = gates_ref[...].T
    y = y0_ref[...]
    for k in range(TOP_K):
        y = y + gates[:, k:k + 1] * _load_token_tiles(buf, k * DISP_TT, DISP_TT, nc, lead=slot)
    ms = jnp.mean(y * y, axis=-1, keepdims=True)
    yn = y * lax.rsqrt(ms + RMS_EPS) * g_ref[...]
    o_ref[...] = x_ref[...] + mod_ref[0, 5:6, :] * yn


def _moe_combine(pos, og, gates, y0, xt, mod, g, n_tok, seq, batch):
    d = xt.shape[1]
    n_tiles = n_tok // DISP_TT
    row_spec = lambda w: pl.BlockSpec((DISP_TT, w), lambda i: (i, 0))
    return pl.pallas_call(
        functools.partial(_moe_combine_kernel, n_tiles=n_tiles),
        grid=(n_tiles,),
        in_specs=[pl.BlockSpec(memory_space=pl.ANY),
                  pl.BlockSpec(memory_space=pl.ANY),
                  pl.BlockSpec((TOP_K, DISP_TT), lambda i: (0, i)), row_spec(d), row_spec(d),
                  pl.BlockSpec((1, ADA_CHUNKS, d), _mod_index(DISP_TT, seq, batch)),
                  pl.BlockSpec((1, d), lambda i: (0, 0))],
        out_specs=row_spec(d),
        out_shape=jax.ShapeDtypeStruct(xt.shape, F32),
        input_output_aliases={4: 0},
        scratch_shapes=[pltpu.SMEM((2, TOP_K, DISP_TT), jnp.int32),
                        pltpu.VMEM((2, TOP_K * DISP_TT * (d // LANES), LANES), F32),
                        pltpu.SemaphoreType.DMA((2,)),
                        pltpu.SemaphoreType.DMA((2,))],
        compiler_params=_params(("arbitrary",)),
        name="moe_combine",
    )(pos, og, gates, y0, xt, mod, g.reshape(1, d))


def _dispatch_plan(counts, n_assign):
    e = counts.shape[0]
    padded = (counts + MOE_RB - 1) // MOE_RB * MOE_RB
    pend = jnp.cumsum(padded)
    pstart = pend - padded
    nb = -(-(n_assign + e * (MOE_RB - 1)) // MOE_RB)
    first_row = jnp.arange(nb, dtype=jnp.int32) * MOE_RB
    block_e = jnp.minimum(jnp.sum((pend[None, :] <= first_row[:, None]).astype(jnp.int32), axis=1), e - 1)
    nused = (pend[-1] // MOE_RB).astype(jnp.int32).reshape(1)
    lastblk = jnp.where(counts > 0, pend // MOE_RB - 1, -1)
    return pstart.astype(jnp.int32), lastblk.astype(jnp.int32), block_e.astype(jnp.int32), nused, nb


def _moe_layer(xt, mod, g_pre, g_post, router_w, router_b, w_gate, w_up, w_down, ws_gate, ws_up, ws_down,
               n_tok, seq, batch, layer):
    e = router_w.shape[1]
    rwt = router_w.astype(F32).T
    rwt_hi, rwt_lo = _split(rwt)
    h, y0, eidx, gates, rank, cnt = _moe_pre(xt, mod, g_pre, rwt_hi, rwt_lo, router_b.astype(F32).reshape(e, 1),
                                             ws_gate.astype(BF16), ws_up.astype(BF16), ws_down.astype(BF16),
                                             n_tok, seq, batch)
    pstart, lastblk, block_e, nused, nb = _dispatch_plan(cnt[:, 0], n_tok * TOP_K)
    pos = _moe_pos(eidx, rank, pstart.astype(F32).reshape(e, 1))
    xg = _moe_dispatch(lastblk, nused, pos, h, nb)
    og = _moe_experts(block_e, nused, xg, w_gate, w_up, w_down, layer)
    return _moe_combine(pos, og, gates, y0, xt, mod, g_post, n_tok, seq, batch)


def kernel(x, c, ctx, c_ctx, ada_w, ada_b, g_pre_mix, g_post_mix, g_pre_ffn, g_post_ffn, ab_w_in, ab_w_out, gla_dw_f, gla_db_f, gla_dw_b, gla_db_b, gla_norm_g, conv_w, conv_b, conv_ln_g, conv_ln_b, na_w_in, na_w_out, na_rpb, moe_router_w, moe_router_b, moe_w_gate, moe_w_up, moe_w_down, moe_ws_gate, moe_ws_up, moe_ws_down):
    batch, seq, d = x.shape
    ctx_len = ctx.shape[1]
    depth = ada_w.shape[0]
    n_x = batch * seq
    n_all = n_x + batch * ctx_len
    assert seq % ROW_TILE == 0 and (batch * ctx_len) % ROW_TILE == 0 and n_x % ctx_len == 0

    xt = jnp.concatenate([x.reshape(n_x, d), ctx.reshape(batch * ctx_len, d)], axis=0)

    cc = jnp.zeros((ADA_ROWS, d), F32).at[:batch].set(c).at[batch].set(c_ctx)
    mods = _ada_mod(cc, ada_w, ada_b)[:, :batch + 1].reshape(depth, batch + 1, ADA_CHUNKS, d)

    for layer in range(depth):
        last = layer == depth - 1
        i = layer // 2
        mod = mods[layer]
        if layer % 2 == 0:
            w_in = ab_w_in[i]
            cuts = np.cumsum([0, GLA_QK, GLA_QK, GLA_V, GLA_V, GLA_LOWRANK, GLA_LOWRANK, CONV_CH, CONV_CH])
            seg = lambda j: w_in[:, cuts[j]:cuts[j + 1]]
            lr_pad = jnp.zeros((d, LANES - 2 * GLA_LOWRANK), w_in.dtype)
            w_cat = jnp.concatenate([seg(0), seg(1), seg(2), seg(3), seg(6), seg(7), seg(4), seg(5), lr_pad],
                                    axis=1).astype(BF16)
            qk, v, g, glu, lr = _prenorm_proj(xt, mod, g_pre_mix[layer], w_cat,
                                              (2 * GLA_QK, GLA_V, GLA_V, 2 * CONV_CH, LANES), 0, 1, seq, batch)
            dwcat = jnp.zeros((LANES, 2 * GLA_QK), F32)
            dwcat = dwcat.at[:GLA_LOWRANK, :GLA_QK].set(gla_dw_f[i])
            dwcat = dwcat.at[GLA_LOWRANK:2 * GLA_LOWRANK, GLA_QK:].set(gla_dw_b[i]).astype(BF16)
            dbcat = jnp.concatenate([gla_db_f[i], gla_db_b[i]]).reshape(1, 2 * GLA_QK).astype(F32)
            a_out = _gla(qk, v, g, lr, dwcat, dbcat, gla_norm_g[i].reshape(1, GLA_V).astype(F32),
                         seq, ctx_len, batch)
            cw = conv_w[i].astype(F32)
            cvec = lambda t: t.reshape(1, CONV_CH).astype(F32)
            cv_x = _conv_module(glu, cw, cvec(conv_b[i]), cvec(conv_ln_g[i]), cvec(conv_ln_b[i]), seq, 0, batch)
            cv_c = _conv_module(glu, cw, cvec(conv_b[i]), cvec(conv_ln_g[i]), cvec(conv_ln_b[i]), ctx_len,
                                n_x // ctx_len, batch)
            cv = jnp.concatenate([cv_x, cv_c], axis=0)
            n_rows = n_x if last else n_all
            xt = _outproj([a_out, cv], ab_w_out[i].astype(BF16), xt, mod, g_post_mix[layer], 2, n_rows, seq, batch)
        else:
            (z,) = _prenorm_proj(xt, mod, g_pre_mix[layer], na_w_in[i].astype(BF16),
                                 (3 * NA_HEADS * NA_DH,), 0, 1, seq, batch)
            bias = _na_bias_table(na_rpb[i], seq // GRID_W)
            o = _na_attention(z, bias, seq, ctx_len, batch)
            if last:
                xt = _outproj([o], na_w_out[i].astype(BF16), xt, mod, g_post_mix[layer], 2, n_x, seq, batch)
            else:
                raise NotImplementedError("context update after a neighbourhood-attention layer")
        n_tok = n_x if last else n_all
        xt = _moe_layer(xt, mod, g_pre_ffn[layer], g_post_ffn[layer], moe_router_w[layer], moe_router_b[layer],
                        moe_w_gate, moe_w_up, moe_w_down, moe_ws_gate[layer],
                        moe_ws_up[layer], moe_ws_down[layer], n_tok, seq, batch, layer)
    return xt[:n_x].reshape(batch, seq, d)
```

```python
import functools

import numpy as np
import jax
import jax.numpy as jnp
from jax import lax
from jax.experimental import pallas as pl
from jax.experimental.pallas import tpu as pltpu

F32 = jnp.float32
BF16 = jnp.bfloat16

GRID_W = 64
RMS_EPS = 1e-6
LN_EPS = 1e-5
ADA_CHUNKS = 6
GLA_HEADS = 4
GLA_DK = 64
GLA_DV = 128
GLA_LOWRANK = 16
GLA_GATE_NORM = 16.0
GLA_CHUNK = 64
ROPE_BASE = 10000.0
GLA_QK = GLA_HEADS * GLA_DK
GLA_V = GLA_HEADS * GLA_DV
CONV_CH = 512
CONV_WIDTH = 31
NA_HEADS = 16
NA_DH = 64
NA_WIN_ROWS = 8
NA_WIN_COLS = 16
N_EXPERTS = 256
TOP_K = 8
N_GROUPS = 8
TOPK_GROUPS = 4
ROUTE_SCALE = 2.5

LANES = 128
ROW_TILE = 512
ADA_ROWS = 24
ADA_TN = 1536
CONV_RC = 32
CONV_HALO = 16
GLA_PAIR = 2
NA_ROW_UNROLL = 4
MOE_RB = 512
DISP_TT = 128
VMEM_LIMIT = 56 * 1024 * 1024

_NT = (((1,), (1,)), ((), ()))
_TN = (((0,), (0,)), ((), ()))


def _params(sem, **kw):
    return pltpu.CompilerParams(dimension_semantics=sem, vmem_limit_bytes=VMEM_LIMIT, **kw)


def _sigmoid(x):
    return 1.0 / (1.0 + jnp.exp(-x))


def _split(a):
    hi = a.astype(BF16)
    lo = (a - hi.astype(F32)).astype(BF16)
    return hi, lo


def _dot3(a, b_hi, b_lo, dims=(((1,), (0,)), ((), ()))):
    a_hi, a_lo = _split(a)
    d = lambda x, y: lax.dot_general(x, y, dims, preferred_element_type=F32)
    return d(a_hi, b_hi) + (d(a_hi, b_lo) + d(a_lo, b_hi))


def _prenorm_mod(x, g, mod_ref, shift_i, scale_i):
    ms = jnp.mean(x * x, axis=-1, keepdims=True)
    h = x * lax.rsqrt(ms + RMS_EPS) * g
    return h * (1.0 + mod_ref[0, scale_i:scale_i + 1, :]) + mod_ref[0, shift_i:shift_i + 1, :]


def _mod_index(tile_rows, seq, batch):
    per_batch = seq // tile_rows
    return lambda i, *_: (jnp.minimum(i // per_batch, batch), 0, 0)


def _ada_kernel(c_ref, w_ref, b_ref, o_ref):
    c = c_ref[...]
    s = c * _sigmoid(c)
    w_hi, w_lo = _split(w_ref[...])
    o_ref[...] = _dot3(s, w_hi, w_lo) + b_ref[...]


def _ada_mod(cc, ada_w, ada_b):
    depth, d, n = ada_w.shape
    return pl.pallas_call(
        _ada_kernel,
        grid=(depth, n // ADA_TN),
        in_specs=[pl.BlockSpec((ADA_ROWS, d), lambda l, j: (0, 0)),
                  pl.BlockSpec((None, d, ADA_TN), lambda l, j: (l, 0, j)),
                  pl.BlockSpec((None, 1, ADA_TN), lambda l, j: (l, 0, j))],
        out_specs=pl.BlockSpec((None, ADA_ROWS, ADA_TN), lambda l, j: (l, 0, j)),
        out_shape=jax.ShapeDtypeStruct((depth, ADA_ROWS, n), F32),
        compiler_params=_params(("arbitrary", "arbitrary")),
        name="ada_mod",
    )(cc, ada_w, ada_b.reshape(depth, 1, n))


def _prenorm_proj_kernel(x_ref, mod_ref, g_ref, w_ref, *o_refs, shift_i, scale_i):
    h = _prenorm_mod(x_ref[...], g_ref[...], mod_ref, shift_i, scale_i).astype(BF16)
    off = 0
    for o_ref in o_refs:
        n = o_ref.shape[-1]
        o_ref[...] = jnp.dot(h, w_ref[:, off:off + n], preferred_element_type=F32).astype(o_ref.dtype)
        off += n


def _prenorm_proj(xt, mod, g, w, splits, shift_i, scale_i, seq, batch):
    rows, d = xt.shape
    n = w.shape[1]
    return pl.pallas_call(
        functools.partial(_prenorm_proj_kernel, shift_i=shift_i, scale_i=scale_i),
        grid=(rows // ROW_TILE,),
        in_specs=[pl.BlockSpec((ROW_TILE, d), lambda i: (i, 0)),
                  pl.BlockSpec((1, ADA_CHUNKS, d), _mod_index(ROW_TILE, seq, batch)),
                  pl.BlockSpec((1, d), lambda i: (0, 0)),
                  pl.BlockSpec((d, n), lambda i: (0, 0))],
        out_specs=[pl.BlockSpec((ROW_TILE, s), lambda i: (i, 0)) for s in splits],
        out_shape=[jax.ShapeDtypeStruct((rows, s), BF16) for s in splits],
        compiler_params=_params(("arbitrary",)),
        name="prenorm_proj",
    )(xt, mod, g.reshape(1, d), w)


def _gla_kernel(qkx_ref, vx_ref, gx_ref, lrx_ref, qkc_ref, vc_ref, gc_ref, lrc_ref,
                dw_ref, db_ref, cos_ref, sin_ref, gg_ref, ox_ref, oc_ref,
                sf_ref, sb_ref, accx_ref, accc_ref, *, n_x, n_c):
    L = GLA_CHUNK
    row = lax.broadcasted_iota(jnp.int32, (L, L), 0)
    col = lax.broadcasted_iota(jnp.int32, (L, L), 1)
    lower = row >= col
    upper = col >= row
    tri_f = jnp.where(lower, 1.0, 0.0).astype(BF16)
    tri_b = jnp.where(upper, 1.0, 0.0).astype(BF16)
    lane = lax.broadcasted_iota(jnp.int32, (1, GLA_QK), 1)
    head_of_lane = lane // GLA_DK
    first_half = (lane % (GLA_DK // 2)) < (GLA_DK // 4)
    bd_mask = (lax.broadcasted_iota(jnp.int32, (GLA_V, GLA_QK), 0) // GLA_DV
               == lax.broadcasted_iota(jnp.int32, (GLA_V, GLA_QK), 1) // GLA_DK)

    def swap_pairs(t):
        nf = GLA_DK // 4
        return jnp.where(first_half, pltpu.roll(t, GLA_QK - nf, 1), pltpu.roll(t, nf, 1))

    def chunk(fwd, c, qk_ref, v_ref, lr_ref, latent):
        rows = pl.ds(pl.multiple_of(c * L, L), L)
        q = qk_ref[rows, 0:GLA_QK].astype(F32) * (GLA_DK ** -0.5)
        k = qk_ref[rows, GLA_QK:2 * GLA_QK].astype(F32)
        if latent:
            cs = cos_ref[rows, :]
            sn = sin_ref[rows, :]
            q = q * cs + swap_pairs(q) * sn
            k = k * cs + swap_pairs(k) * sn
        dsl = slice(0, GLA_QK) if fwd else slice(GLA_QK, 2 * GLA_QK)
        pre = jnp.dot(lr_ref[rows, :], dw_ref[:, dsl], preferred_element_type=F32) + db_ref[:, dsl]
        yield
        logd = (jnp.minimum(pre, 0.0) - jnp.log(1.0 + jnp.exp(-jnp.abs(pre)))) * (1.0 / GLA_GATE_NORM)
        d_hi, d_lo = _split(logd)
        tri = tri_f if fwd else tri_b
        b = (jnp.dot(tri, d_hi, preferred_element_type=F32)
             + jnp.dot(tri, d_lo, preferred_element_type=F32))
        yield
        mid = L // 2 - 1 if fwd else L // 2
        end = L - 1 if fwd else 0
        b_mid = b[mid:mid + 1, :]
        b_end = b[end:end + 1, :]
        qe = q * jnp.exp(b - b_mid)
        ke = (k * jnp.exp(b_mid - b)).astype(BF16)
        q_in = (q * jnp.exp(b)).astype(BF16)
        k_out = (k * jnp.exp(b_end - b)).astype(BF16)
        dec = jnp.exp(b_end)
        vv = v_ref[rows, :]
        mask = lower if fwd else upper
        scores = []
        for h in range(GLA_HEADS):
            qm = jnp.where(head_of_lane == h, qe, 0.0).astype(BF16)
            scores.append(lax.dot_general(qm, ke, _NT, preferred_element_type=F32))
        yield
        outs = []
        for h in range(GLA_HEADS):
            a = jnp.where(mask, scores[h], 0.0).astype(BF16)
            outs.append(jnp.dot(a, vv[:, h * GLA_DV:(h + 1) * GLA_DV], preferred_element_type=F32))
        o_intra = jnp.concatenate(outs, axis=-1)
        yield

        def last():
            s_ref = sf_ref if fwd else sb_ref
            st = s_ref[...]
            o_inter = lax.dot_general(q_in, st.astype(BF16), _NT, preferred_element_type=F32)
            upd = lax.dot_general(vv, k_out, _TN, preferred_element_type=F32)
            s_ref[...] = jnp.where(bd_mask, st * dec + upd, 0.0)
            return o_intra + o_inter

        return last

    def run_chunks(specs):
        gens = [chunk(*sp) for sp in specs]
        lasts = [None] * len(gens)
        live = list(range(len(gens)))
        while live:
            for gi in list(live):
                try:
                    next(gens[gi])
                except StopIteration as stop:
                    lasts[gi] = stop.value
                    live.remove(gi)
        return [fn() for fn in lasts]

    def finish(o, g_ref, c):
        rows = pl.ds(pl.multiple_of(c * L, L), L)
        g = g_ref[rows, :].astype(F32)
        parts = []
        for h in range(GLA_HEADS):
            oh = o[:, h * GLA_DV:(h + 1) * GLA_DV]
            ms = jnp.mean(oh * oh, axis=-1, keepdims=True)
            parts.append(oh * lax.rsqrt(ms + RMS_EPS))
        on = jnp.concatenate(parts, axis=-1) * gg_ref[...]
        return (on * (g * _sigmoid(g))).astype(BF16)

    def run_seq(n, qk_ref, v_ref, g_ref, lr_ref, acc_ref, o_ref, latent):
        half = n // 2

        def rows_of(c):
            return pl.ds(pl.multiple_of(c * L, L), L)

        def chunks_of(i):
            cs = [(True, i * GLA_PAIR + u) for u in range(GLA_PAIR)]
            cs += [(False, n - 1 - (i * GLA_PAIR + u)) for u in range(GLA_PAIR)]
            return cs

        def first(i, carry):
            cs = chunks_of(i)
            res = run_chunks([(f, c, qk_ref, v_ref, lr_ref, latent) for f, c in cs])
            for (f, c), o in zip(cs, res):
                acc_ref[rows_of(c), :] = o
            return carry

        def second(i, carry):
            cs = chunks_of(i)
            res = run_chunks([(f, c, qk_ref, v_ref, lr_ref, latent) for f, c in cs])
            for (f, c), o in zip(cs, res):
                o_ref[rows_of(c), :] = finish(acc_ref[rows_of(c), :] + o, g_ref, c)
            return carry

        assert half % GLA_PAIR == 0
        lax.fori_loop(0, half // GLA_PAIR, first, 0)
        lax.fori_loop(half // GLA_PAIR, n // GLA_PAIR, second, 0)

    sf_ref[...] = jnp.zeros_like(sf_ref)
    sb_ref[...] = jnp.zeros_like(sb_ref)
    run_seq(n_c, qkc_ref, vc_ref, gc_ref, lrc_ref, accc_ref, oc_ref, False)
    run_seq(n_x, qkx_ref, vx_ref, gx_ref, lrx_ref, accx_ref, ox_ref, True)


def _rope_tables(seq):
    half = GLA_DK // 2
    nf = half // 2
    inv = ROPE_BASE ** (-jnp.arange(nf, dtype=F32) / nf)
    t = jnp.arange(seq)
    rows = (t // GRID_W).astype(F32)
    cols = (t % GRID_W).astype(F32)
    ar = rows[:, None] * inv[None, :]
    ac = cols[:, None] * inv[None, :]
    cos_h = jnp.concatenate([jnp.cos(ar), jnp.cos(ar), jnp.cos(ac), jnp.cos(ac)], axis=-1)
    sin_h = jnp.concatenate([-jnp.sin(ar), jnp.sin(ar), -jnp.sin(ac), jnp.sin(ac)], axis=-1)
    return jnp.tile(cos_h, (1, GLA_HEADS)), jnp.tile(sin_h, (1, GLA_HEADS))


def _gla(qk, v, g, lr, dwcat, dbcat, gla_g, seq, ctx_len, batch):
    rows = qk.shape[0]
    cos_t, sin_t = _rope_tables(seq)
    cb = (batch * seq) // ctx_len
    xs = lambda w: pl.BlockSpec((seq, w), lambda b: (b, 0))
    cs = lambda w: pl.BlockSpec((ctx_len, w), lambda b: (cb + b, 0))
    full = lambda a: pl.BlockSpec(a.shape, lambda b: (0,) * a.ndim)
    out_x, out_c = pl.pallas_call(
        functools.partial(_gla_kernel, n_x=seq // GLA_CHUNK, n_c=ctx_len // GLA_CHUNK),
        grid=(batch,),
        in_specs=[xs(2 * GLA_QK), xs(GLA_V), xs(GLA_V), xs(LANES),
                  cs(2 * GLA_QK), cs(GLA_V), cs(GLA_V), cs(LANES),
                  full(dwcat), full(dbcat), full(cos_t), full(sin_t), full(gla_g)],
        out_specs=[pl.BlockSpec((seq, GLA_V), lambda b: (b, 0)),
                   pl.BlockSpec((ctx_len, GLA_V), lambda b: (b, 0))],
        out_shape=[jax.ShapeDtypeStruct((batch * seq, GLA_V), BF16),
                   jax.ShapeDtypeStruct((batch * ctx_len, GLA_V), BF16)],
        scratch_shapes=[pltpu.VMEM((GLA_V, GLA_QK), F32), pltpu.VMEM((GLA_V, GLA_QK), F32),
                        pltpu.VMEM((seq, GLA_V), F32), pltpu.VMEM((ctx_len, GLA_V), F32)],
        compiler_params=_params(("arbitrary",)),
        name="gla",
    )(qk, v, g, lr, qk, v, g, lr, dwcat, dbcat, cos_t, sin_t, gla_g)
    del rows
    return jnp.concatenate([out_x, out_c], axis=0)


def _conv_kernel(glu_ref, w_ref, cb_ref, lg_ref, lb_ref, o_ref, pad_ref, win_ref, *, t_len):
    ch = CONV_CH
    zeros = jnp.zeros((CONV_HALO, ch), F32)
    pad_ref[0:CONV_HALO, :] = zeros
    pad_ref[CONV_HALO + t_len:2 * CONV_HALO + t_len, :] = zeros
    fill = 128

    def fill_body(i, carry):
        r = pl.multiple_of(i * fill, fill)
        a = glu_ref[pl.ds(r, fill), 0:ch].astype(F32)
        b = glu_ref[pl.ds(r, fill), ch:2 * ch].astype(F32)
        pad_ref[pl.ds(CONV_HALO + r, fill), :] = a * _sigmoid(b)
        return carry

    lax.fori_loop(0, t_len // fill, fill_body, 0)
    first_tap = CONV_HALO - CONV_WIDTH // 2

    def body(i, carry):
        base = pl.multiple_of(i * CONV_RC, CONV_RC)
        win_ref[...] = pad_ref[pl.ds(base, CONV_RC + 2 * CONV_HALO), :]
        acc = jnp.zeros((CONV_RC, ch), F32)
        for j in range(CONV_WIDTH):
            acc = acc + w_ref[j:j + 1, :] * win_ref[first_tap + j:first_tap + j + CONV_RC, :]
        acc = acc + cb_ref[...]
        mu = jnp.mean(acc, axis=-1, keepdims=True)
        cen = acc - mu
        var = jnp.mean(cen * cen, axis=-1, keepdims=True)
        y = cen * lax.rsqrt(var + LN_EPS) * lg_ref[...] + lb_ref[...]
        o_ref[pl.ds(base, CONV_RC), :] = (y * _sigmoid(y)).astype(BF16)
        return carry

    lax.fori_loop(0, t_len // CONV_RC, body, 0)


def _conv_module(glu, conv_w, conv_b, ln_g, ln_b, t_len, first_block, batch):
    full = lambda a: pl.BlockSpec(a.shape, lambda b: (0,) * a.ndim)
    return pl.pallas_call(
        functools.partial(_conv_kernel, t_len=t_len),
        grid=(batch,),
        in_specs=[pl.BlockSpec((t_len, 2 * CONV_CH), lambda b: (first_block + b, 0)),
                  full(conv_w), full(conv_b), full(ln_g), full(ln_b)],
        out_specs=pl.BlockSpec((t_len, CONV_CH), lambda b: (b, 0)),
        out_shape=jax.ShapeDtypeStruct((batch * t_len, CONV_CH), BF16),
        scratch_shapes=[pltpu.VMEM((t_len + 2 * CONV_HALO, CONV_CH), F32),
                        pltpu.VMEM((CONV_RC + 2 * CONV_HALO, CONV_CH), F32)],
        compiler_params=_params(("arbitrary",)),
        name="conv_module",
    )(glu, conv_w, conv_b, ln_g, ln_b)


def _outproj_kernel(*refs, n_lhs, gate_i):
    lhs_refs = refs[:n_lhs]
    w_ref, x_ref, mod_ref, g_ref, o_ref = refs[n_lhs:]
    off = 0
    y = None
    for l_ref in lhs_refs:
        kk = l_ref.shape[-1]
        t = jnp.dot(l_ref[...], w_ref[off:off + kk, :], preferred_element_type=F32)
        y = t if y is None else y + t
        off += kk
    ms = jnp.mean(y * y, axis=-1, keepdims=True)
    yn = y * lax.rsqrt(ms + RMS_EPS) * g_ref[...]
    o_ref[...] = x_ref[...] + mod_ref[0, gate_i:gate_i + 1, :] * yn


def _outproj(lhs_list, w, xt, mod, g, gate_i, n_rows, seq, batch):
    d = xt.shape[1]
    n_lhs = len(lhs_list)
    in_specs = [pl.BlockSpec((ROW_TILE, a.shape[1]), lambda i: (i, 0)) for a in lhs_list]
    in_specs += [pl.BlockSpec(w.shape, lambda i: (0, 0)),
                 pl.BlockSpec((ROW_TILE, d), lambda i: (i, 0)),
                 pl.BlockSpec((1, ADA_CHUNKS, d), _mod_index(ROW_TILE, seq, batch)),
                 pl.BlockSpec((1, d), lambda i: (0, 0))]
    return pl.pallas_call(
        functools.partial(_outproj_kernel, n_lhs=n_lhs, gate_i=gate_i),
        grid=(n_rows // ROW_TILE,),
        in_specs=in_specs,
        out_specs=pl.BlockSpec((ROW_TILE, d), lambda i: (i, 0)),
        out_shape=jax.ShapeDtypeStruct(xt.shape, F32),
        input_output_aliases={n_lhs + 1: 0},
        compiler_params=_params(("arbitrary",)),
        name="outproj",
    )(*lhs_list, w, xt, mod, g.reshape(1, d))


def _na_kernel(q_ref, k_ref, v_ref, kc_ref, vc_ref, bias_ref, o_ref, *, n_rows):
    wr = NA_WIN_ROWS
    lane = lax.broadcasted_iota(jnp.int32, (1, LANES), 1)
    head0 = lane < NA_DH
    kc = kc_ref[...]
    vc = vc_ref[...]

    def body(it, carry):
        rs = [it * NA_ROW_UNROLL + u for u in range(NA_ROW_UNROLL)]
        r0s = [jnp.clip(r - wr // 2, 0, n_rows - wr) for r in rs]
        win = [pl.ds(pl.multiple_of(r0 * GRID_W, GRID_W), wr * GRID_W) for r0 in r0s]
        scores = []
        for r, r0, w in zip(rs, r0s, win):
            q2 = q_ref[pl.ds(pl.multiple_of(r * GRID_W, GRID_W), GRID_W), :] * (NA_DH ** -0.5)
            zero = jnp.zeros_like(q2)
            qs = jnp.concatenate([jnp.where(head0, q2, zero), jnp.where(head0, zero, q2)], axis=0)
            bias = bias_ref[:, r - r0].reshape(2 * GRID_W, wr * GRID_W)
            s_w = lax.dot_general(qs, k_ref[w, :], _NT, preferred_element_type=F32) + bias
            s_c = lax.dot_general(qs, kc, _NT, preferred_element_type=F32)
            scores.append((s_w, s_c))
        probs = []
        for s_w, s_c in scores:
            m = jnp.maximum(jnp.max(s_w, axis=-1, keepdims=True), jnp.max(s_c, axis=-1, keepdims=True))
            p_w = jnp.exp(s_w - m)
            p_c = jnp.exp(s_c - m)
            den = jnp.sum(p_w, axis=-1, keepdims=True) + jnp.sum(p_c, axis=-1, keepdims=True)
            probs.append((p_w.astype(BF16), p_c.astype(BF16), den))
        for r, w, (p_w, p_c, den) in zip(rs, win, probs):
            o = (jnp.dot(p_w, v_ref[w, :], preferred_element_type=F32)
                 + jnp.dot(p_c, vc, preferred_element_type=F32)) / den
            o_ref[pl.ds(pl.multiple_of(r * GRID_W, GRID_W), GRID_W), :] = (
                jnp.where(head0, o[:GRID_W], o[GRID_W:]).astype(BF16))
        return carry

    assert n_rows % NA_ROW_UNROLL == 0
    lax.fori_loop(0, n_rows // NA_ROW_UNROLL, body, 0)


def _na_bias_table(rpb, n_rows):
    wr = min(NA_WIN_ROWS, n_rows)
    qc = np.arange(GRID_W)
    cstart = np.clip(qc - NA_WIN_COLS // 2, 0, GRID_W - NA_WIN_COLS)
    kcol = np.arange(GRID_W)
    ok = (kcol[None, :] >= cstart[:, None]) & (kcol[None, :] < cstart[:, None] + NA_WIN_COLS)
    cidx = np.clip(kcol[None, :] - qc[:, None] + NA_WIN_COLS - 1, 0, 2 * NA_WIN_COLS - 2)
    delta = np.arange(wr)
    j = np.arange(wr)
    ridx = j[None, :] - delta[:, None] + NA_WIN_ROWS - 1
    t = rpb[:, ridx][:, :, :, cidx]
    t = jnp.where(jnp.asarray(ok)[None, None, None], t.astype(F32), -jnp.inf)
    t = t.transpose(0, 1, 3, 2, 4)
    return t.reshape(rpb.shape[0], wr, GRID_W, wr * GRID_W)


def _na_attention(z, bias, seq, ctx_len, batch):
    n_pairs = NA_HEADS * NA_DH // LANES
    n_rows = seq // GRID_W
    cb = (batch * seq) // ctx_len
    return pl.pallas_call(
        functools.partial(_na_kernel, n_rows=n_rows),
        grid=(n_pairs, batch),
        in_specs=[pl.BlockSpec((seq, LANES), lambda p, b: (b, p)),
                  pl.BlockSpec((seq, LANES), lambda p, b: (b, n_pairs + p)),
                  pl.BlockSpec((seq, LANES), lambda p, b: (b, 2 * n_pairs + p)),
                  pl.BlockSpec((ctx_len, LANES), lambda p, b: (cb + b, n_pairs + p)),
                  pl.BlockSpec((ctx_len, LANES), lambda p, b: (cb + b, 2 * n_pairs + p)),
                  pl.BlockSpec((2,) + bias.shape[1:], lambda p, b: (p, 0, 0, 0))],
        out_specs=pl.BlockSpec((seq, LANES), lambda p, b: (b, p)),
        out_shape=jax.ShapeDtypeStruct((batch * seq, NA_HEADS * NA_DH), BF16),
        compiler_params=_params(("arbitrary", "arbitrary")),
        name="na_attention",
    )(z, z, z, z, z, bias)


def _store_token_tiles(ref, val, row0=0):
    n, d = val.shape
    nc = d // LANES
    for k in range(nc):
        ref[pl.ds(row0 * nc + k, n, stride=nc), :] = val[:, k * LANES:(k + 1) * LANES]


def _load_token_tiles(ref, row0, n, nc, lead=None):
    chunks = []
    for k in range(nc):
        idx = pl.ds(row0 * nc + k, n, stride=nc)
        chunks.append(ref[idx, :] if lead is None else ref[lead, idx, :])
    return jnp.concatenate(chunks, axis=-1)


def _first_argmax(x, iota_f, n):
    m = jnp.max(x, axis=0, keepdims=True)
    idx = jnp.min(jnp.where(x == m, iota_f, float(n)), axis=0, keepdims=True)
    return m, idx


def _moe_pre_kernel(x_ref, mod_ref, g_ref, rwh_ref, rwl_ref, rb_ref, wsg_ref, wsu_ref, wsd_ref,
                    h_ref, y0_ref, eidx_ref, gates_ref, rank_ref, cnt_ref, base_ref):
    tm = x_ref.shape[0]
    n_e = rwh_ref.shape[0]

    @pl.when(pl.program_id(0) == 0)
    def _():
        base_ref[...] = jnp.zeros_like(base_ref)

    h = _prenorm_mod(x_ref[...], g_ref[...], mod_ref, 3, 4)
    _store_token_tiles(h_ref, h)
    hb = h.astype(BF16)
    gate = jnp.dot(hb, wsg_ref[...], preferred_element_type=F32)
    up = jnp.dot(hb, wsu_ref[...], preferred_element_type=F32)
    hid = (gate * _sigmoid(gate) * up).astype(BF16)
    y0_ref[...] = jnp.dot(hid, wsd_ref[...], preferred_element_type=F32)

    h_lo = (h - hb.astype(F32)).astype(BF16)
    d = lambda a, b: lax.dot_general(a, b, _NT, preferred_element_type=F32)
    logits = d(rwh_ref[...], hb) + (d(rwh_ref[...], h_lo) + d(rwl_ref[...], hb))
    scores = _sigmoid(logits)
    biased = scores + rb_ref[...]
    neg = -jnp.inf
    gsz = n_e // N_GROUPS
    io_g = lax.broadcasted_iota(jnp.int32, (gsz, tm), 0).astype(F32)
    grp_rows = []
    for gi in range(N_GROUPS):
        xg = biased[gi * gsz:(gi + 1) * gsz]
        m1, i1 = _first_argmax(xg, io_g, gsz)
        m2 = jnp.max(jnp.where(io_g == i1, neg, xg), axis=0, keepdims=True)
        grp_rows.append(m1 + m2)
    cur = jnp.concatenate(grp_rows, axis=0)
    io_n = lax.broadcasted_iota(jnp.int32, (N_GROUPS, tm), 0).astype(F32)
    keep = jnp.zeros((N_GROUPS, tm), F32)
    for _ in range(TOPK_GROUPS):
        _, ii = _first_argmax(cur, io_n, N_GROUPS)
        sel = io_n == ii
        keep = jnp.where(sel, 1.0, keep)
        cur = jnp.where(sel, neg, cur)
    cur = jnp.concatenate([jnp.where(keep[gi:gi + 1] > 0.0, biased[gi * gsz:(gi + 1) * gsz], neg)
                           for gi in range(N_GROUPS)], axis=0)
    io_e = lax.broadcasted_iota(jnp.int32, (n_e, tm), 0).astype(F32)
    idxs, gvals = [], []
    chosen = jnp.zeros((n_e, tm), F32)
    for _ in range(TOP_K):
        _, ii = _first_argmax(cur, io_e, n_e)
        sel = io_e == ii
        idxs.append(ii)
        gvals.append(jnp.sum(jnp.where(sel, scores, 0.0), axis=0, keepdims=True))
        chosen = jnp.where(sel, 1.0, chosen)
        cur = jnp.where(sel, neg, cur)
    gv = jnp.concatenate(gvals, axis=0)
    gates_ref[...] = gv / jnp.sum(gv, axis=0, keepdims=True) * ROUTE_SCALE
    eidx_ref[...] = jnp.concatenate(idxs, axis=0).astype(jnp.int32)

    earlier = (lax.broadcasted_iota(jnp.int32, (tm, tm), 0) < lax.broadcasted_iota(jnp.int32, (tm, tm), 1))
    before = jnp.dot(chosen.astype(BF16), jnp.where(earlier, 1.0, 0.0).astype(BF16), preferred_element_type=F32)
    before = before + base_ref[:, 0:1]
    ranks = [jnp.sum(jnp.where(io_e == ii, before, 0.0), axis=0, keepdims=True) for ii in idxs]
    rank_ref[...] = jnp.concatenate(ranks, axis=0).astype(jnp.int32)
    base_ref[...] = base_ref[...] + jnp.sum(chosen, axis=1, keepdims=True)
    cnt_ref[...] = base_ref[...].astype(jnp.int32)


def _moe_pre(xt, mod, g, rwt_hi, rwt_lo, rb, wsg, wsu, wsd, n_tok, seq, batch):
    d = xt.shape[1]
    e = rwt_hi.shape[0]
    full = lambda a: pl.BlockSpec(a.shape, lambda i: (0,) * a.ndim)
    tok_major = lambda: pl.BlockSpec((TOP_K, ROW_TILE), lambda i: (0, i))
    return pl.pallas_call(
        _moe_pre_kernel,
        grid=(n_tok // ROW_TILE,),
        in_specs=[pl.BlockSpec((ROW_TILE, d), lambda i: (i, 0)),
                  pl.BlockSpec((1, ADA_CHUNKS, d), _mod_index(ROW_TILE, seq, batch)),
                  pl.BlockSpec((1, d), lambda i: (0, 0)),
                  full(rwt_hi), full(rwt_lo), full(rb), full(wsg), full(wsu), full(wsd)],
        out_specs=[pl.BlockSpec((ROW_TILE * (d // LANES), LANES), lambda i: (i, 0)),
                   pl.BlockSpec((ROW_TILE, d), lambda i: (i, 0)),
                   tok_major(), tok_major(), tok_major(),
                   pl.BlockSpec((e, LANES), lambda i: (0, 0))],
        out_shape=[jax.ShapeDtypeStruct((n_tok * (d // LANES), LANES), F32),
                   jax.ShapeDtypeStruct((n_tok, d), F32),
                   jax.ShapeDtypeStruct((TOP_K, n_tok), jnp.int32),
                   jax.ShapeDtypeStruct((TOP_K, n_tok), F32),
                   jax.ShapeDtypeStruct((TOP_K, n_tok), jnp.int32),
                   jax.ShapeDtypeStruct((e, LANES), jnp.int32)],
        scratch_shapes=[pltpu.VMEM((e, LANES), F32)],
        compiler_params=_params(("arbitrary",)),
        name="moe_pre",
    )(xt, mod, g.reshape(1, d), rwt_hi, rwt_lo, rb, wsg, wsu, wsd)


def _moe_pos_kernel(eidx_ref, rank_ref, pstart_ref, pos_ref):
    n_e = pstart_ref.shape[0]
    tp = eidx_ref.shape[1]
    io_e = lax.broadcasted_iota(jnp.int32, (n_e, tp), 0)
    ps = pstart_ref[...]
    e = eidx_ref[...]
    rows = [jnp.sum(jnp.where(io_e == e[k:k + 1, :], ps, 0.0), axis=0, keepdims=True) for k in range(TOP_K)]
    pos_ref[...] = rank_ref[...] + jnp.concatenate(rows, axis=0).astype(jnp.int32)


def _moe_pos(eidx, rank, pstart):
    n_tok = eidx.shape[1]
    blk = lambda: pl.BlockSpec((TOP_K, ROW_TILE), lambda i: (0, i))
    return pl.pallas_call(
        _moe_pos_kernel,
        grid=(n_tok // ROW_TILE,),
        in_specs=[blk(), blk(), pl.BlockSpec(pstart.shape, lambda i: (0, 0))],
        out_specs=blk(),
        out_shape=jax.ShapeDtypeStruct(eidx.shape, jnp.int32),
        compiler_params=_params(("arbitrary",)),
        name="moe_pos",
    )(eidx, rank, pstart)


def _moe_dispatch_kernel(lastblk_ref, nused_ref, pos_hbm, h_hbm, xg_hbm, idx_smem, zbuf, hbuf, idx_sem, row_sem, z_sem,
                         h_sem, *, n_tiles, nc, n_blocks):
    i = pl.program_id(0)
    slot = i % 2
    n_e = lastblk_ref.shape[0]
    rows_per_step = DISP_TT * TOP_K

    def h_copy(t, s):
        return pltpu.make_async_copy(h_hbm.at[pl.ds(pl.multiple_of(t * (DISP_TT * nc), DISP_TT * nc), DISP_TT * nc), :],
                                     hbuf.at[s], h_sem.at[s])

    def idx_copy(t, s):
        return pltpu.make_async_copy(pos_hbm.at[:, pl.ds(pl.multiple_of(t * DISP_TT, DISP_TT), DISP_TT)],
                                     idx_smem.at[s], idx_sem.at[s])

    def block_copy(b):
        return pltpu.make_async_copy(zbuf, xg_hbm.at[pl.ds(pl.multiple_of(b * (MOE_RB * nc), MOE_RB * nc),
                                                           MOE_RB * nc), :], z_sem)

    def wait_rows(s):
        pltpu.make_async_copy(h_hbm.at[pl.ds(0, rows_per_step * nc), :], xg_hbm.at[pl.ds(0, rows_per_step * nc), :],
                              row_sem.at[s]).wait()

    @pl.when(i == 0)
    def _():
        idx_copy(0, 0).start()
        zbuf[...] = jnp.zeros_like(zbuf)

        def for_zero_blocks(fn):
            def per_expert(e, carry):
                @pl.when(lastblk_ref[e] >= 0)
                def _():
                    fn(block_copy(lastblk_ref[e]))
                return carry

            def per_block(b, carry):
                fn(block_copy(b))
                return carry

            lax.fori_loop(0, n_e, per_expert, 0)
            lax.fori_loop(nused_ref[0], n_blocks, per_block, 0)

        for_zero_blocks(lambda cp: cp.start())
        for_zero_blocks(lambda cp: cp.wait())

        h_copy(0, 0).start()

    idx_copy(i, slot).wait()
    hslot = i % 3
    h_copy(i, hslot).wait()

    @pl.when(i + 1 < n_tiles)
    def _():
        idx_copy(i + 1, 1 - slot).start()
        h_copy(i + 1, (i + 1) % 3).start()

    def body(r, carry):
        src = hbuf.at[hslot, pl.ds(pl.multiple_of(r * nc, nc), nc), :]
        for k in range(TOP_K):
            p = idx_smem[slot, k, r]
            pltpu.make_async_copy(src, xg_hbm.at[pl.ds(pl.multiple_of(p * nc, nc), nc), :], row_sem.at[slot]).start()
        return carry

    lax.fori_loop(0, DISP_TT, body, 0, unroll=2)

    @pl.when(i > 0)
    def _():
        wait_rows(1 - slot)

    @pl.when(i == n_tiles - 1)
    def _():
        wait_rows(slot)


def _moe_dispatch(lastblk, nused, pos, h, n_blocks):
    n_tok = pos.shape[1]
    nc = h.shape[0] // n_tok
    n_tiles = n_tok // DISP_TT
    grid_spec = pltpu.PrefetchScalarGridSpec(
        num_scalar_prefetch=2,
        grid=(n_tiles,),
        in_specs=[pl.BlockSpec(memory_space=pl.ANY), pl.BlockSpec(memory_space=pl.ANY)],
        out_specs=pl.BlockSpec(memory_space=pl.ANY),
        scratch_shapes=[pltpu.SMEM((2, TOP_K, DISP_TT), jnp.int32),
                        pltpu.VMEM((MOE_RB * nc, LANES), F32),
                        pltpu.VMEM((3, DISP_TT * nc, LANES), F32),
                        pltpu.SemaphoreType.DMA((2,)),
                        pltpu.SemaphoreType.DMA((2,)),
                        pltpu.SemaphoreType.DMA(()),
                        pltpu.SemaphoreType.DMA((3,))],
    )
    return pl.pallas_call(
        functools.partial(_moe_dispatch_kernel, n_tiles=n_tiles, nc=nc, n_blocks=n_blocks),
        grid_spec=grid_spec,
        out_shape=jax.ShapeDtypeStruct((n_blocks * MOE_RB * nc, LANES), F32),
        compiler_params=_params(("arbitrary",)),
        name="moe_dispatch",
    )(lastblk, nused, pos, h)


def _moe_expert_kernel(be_ref, nused_ref, x_ref, wg_ref, wu_ref, wd_ref, o_ref, wgb, wub, wdb):
    i = pl.program_id(0)
    nused = nused_ref[0]
    nc = x_ref.shape[0] // MOE_RB
    new_expert = jnp.logical_or(i == 0, be_ref[i] != be_ref[jnp.maximum(i - 1, 0)])

    @pl.when(jnp.logical_and(i < nused, new_expert))
    def _():
        wgb[...] = wg_ref[...].astype(BF16)
        wub[...] = wu_ref[...].astype(BF16)
        wdb[...] = wd_ref[...].astype(BF16)

    @pl.when(i < nused)
    def _():
        xb = _load_token_tiles(x_ref, 0, MOE_RB, nc).astype(BF16)
        gate = jnp.dot(xb, wgb[...], preferred_element_type=F32)
        up = jnp.dot(xb, wub[...], preferred_element_type=F32)
        hid = (gate * _sigmoid(gate) * up).astype(BF16)
        _store_token_tiles(o_ref, jnp.dot(hid, wdb[...], preferred_element_type=F32))

    @pl.when(i >= nused)
    def _():
        o_ref[...] = jnp.zeros_like(o_ref)


def _moe_experts(block_e, nused, xg, wg, wu, wd, layer):
    nb = block_e.shape[0]
    d = wg.shape[2]
    nc = d // LANES
    de = wg.shape[3]
    grid_spec = pltpu.PrefetchScalarGridSpec(
        num_scalar_prefetch=2,
        grid=(nb,),
        in_specs=[pl.BlockSpec((MOE_RB * nc, LANES), lambda i, be, nu: (jnp.minimum(i, nu[0] - 1), 0)),
                  pl.BlockSpec((None, None, d, de), lambda i, be, nu: (layer, be[i], 0, 0)),
                  pl.BlockSpec((None, None, d, de), lambda i, be, nu: (layer, be[i], 0, 0)),
                  pl.BlockSpec((None, None, de, d), lambda i, be, nu: (layer, be[i], 0, 0))],
        out_specs=pl.BlockSpec((MOE_RB * nc, LANES), lambda i, be, nu: (i, 0)),
        scratch_shapes=[pltpu.VMEM((d, de), BF16), pltpu.VMEM((d, de), BF16), pltpu.VMEM((de, d), BF16)],
    )
    return pl.pallas_call(
        _moe_expert_kernel,
        grid_spec=grid_spec,
        out_shape=jax.ShapeDtypeStruct((nb * MOE_RB * nc, LANES), F32),
        compiler_params=_params(("arbitrary",)),
        name="moe_experts",
    )(block_e, nused, xg, wg, wu, wd)


def _moe_combine_kernel(pos_hbm, og_hbm, gates_ref, y0_ref, x_ref, mod_ref, g_ref, o_ref,
                        idx_smem, buf, idx_sem, row_sem, *, n_tiles):
    i = pl.program_id(0)
    slot = i % 2
    nxt = 1 - slot
    n_rows = TOP_K * DISP_TT
    nc = buf.shape[1] // n_rows

    def idx_copy(t, s):
        return pltpu.make_async_copy(pos_hbm.at[:, pl.ds(pl.multiple_of(t * DISP_TT, DISP_TT), DISP_TT)],
                                     idx_smem.at[s], idx_sem.at[s])

    def issue_rows(s):
        def body(r, carry):
            for k in range(TOP_K):
                p = idx_smem[s, k, r]
                pltpu.make_async_copy(og_hbm.at[pl.ds(pl.multiple_of(p * nc, nc), nc), :],
                                      buf.at[s, pl.ds(pl.multiple_of((k * DISP_TT + r) * nc, nc), nc), :],
                                      row_sem.at[s]).start()
            return carry
        lax.fori_loop(0, DISP_TT, body, 0, unroll=2)

    def wait_rows(s):
        pltpu.make_async_copy(og_hbm.at[pl.ds(0, n_rows * nc), :], buf.at[s], row_sem.at[s]).wait()

    @pl.when(i == 0)
    def _():
        idx_copy(0, 0).start()
        idx_copy(0, 0).wait()
        issue_rows(0)

        @pl.when(n_tiles > 1)
        def _():
            idx_copy(1, 1).start()

    @pl.when(i + 1 < n_tiles)
    def _():
        idx_copy(i + 1, nxt).wait()
        issue_rows(nxt)

    wait_rows(slot)

    @pl.when(i + 2 < n_tiles)
    def _():
        idx_copy(i + 2, slot).start()

    gates = gates_ref[...].T
    y = y0_ref[...]
    for k in range(TOP_K):
        y = y + gates[:, k:k + 1] * _load_token_tiles(buf, k * DISP_TT, DISP_TT, nc, lead=slot)
    ms = jnp.mean(y * y, axis=-1, keepdims=True)
    yn = y * lax.rsqrt(ms + RMS_EPS) * g_ref[...]
    o_ref[...] = x_ref[...] + mod_ref[0, 5:6, :] * yn


def _moe_combine(pos, og, gates, y0, xt, mod, g, n_tok, seq, batch):
    d = xt.shape[1]
    n_tiles = n_tok // DISP_TT
    row_spec = lambda w: pl.BlockSpec((DISP_TT, w), lambda i: (i, 0))
    return pl.pallas_call(
        functools.partial(_moe_combine_kernel, n_tiles=n_tiles),
        grid=(n_tiles,),
        in_specs=[pl.BlockSpec(memory_space=pl.ANY),
                  pl.BlockSpec(memory_space=pl.ANY),
                  pl.BlockSpec((TOP_K, DISP_TT), lambda i: (0, i)), row_spec(d), row_spec(d),
                  pl.BlockSpec((1, ADA_CHUNKS, d), _mod_index(DISP_TT, seq, batch)),
                  pl.BlockSpec((1, d), lambda i: (0, 0))],
        out_specs=row_spec(d),
        out_shape=jax.ShapeDtypeStruct(xt.shape, F32),
        input_output_aliases={4: 0},
        scratch_shapes=[pltpu.SMEM((2, TOP_K, DISP_TT), jnp.int32),
                        pltpu.VMEM((2, TOP_K * DISP_TT * (d // LANES), LANES), F32),
                        pltpu.SemaphoreType.DMA((2,)),
                        pltpu.SemaphoreType.DMA((2,))],
        compiler_params=_params(("arbitrary",)),
        name="moe_combine",
    )(pos, og, gates, y0, xt, mod, g.reshape(1, d))


def _dispatch_plan(counts, n_assign):
    e = counts.shape[0]
    padded = (counts + MOE_RB - 1) // MOE_RB * MOE_RB
    pend = jnp.cumsum(padded)
    pstart = pend - padded
    nb = -(-(n_assign + e * (MOE_RB - 1)) // MOE_RB)
    first_row = jnp.arange(nb, dtype=jnp.int32) * MOE_RB
    block_e = jnp.minimum(jnp.sum((pend[None, :] <= first_row[:, None]).astype(jnp.int32), axis=1), e - 1)
    nused = (pend[-1] // MOE_RB).astype(jnp.int32).reshape(1)
    lastblk = jnp.where(counts > 0, pend // MOE_RB - 1, -1)
    return pstart.astype(jnp.int32), lastblk.astype(jnp.int32), block_e.astype(jnp.int32), nused, nb


def _moe_layer(xt, mod, g_pre, g_post, router_w, router_b, w_gate, w_up, w_down, ws_gate, ws_up, ws_down,
               n_tok, seq, batch, layer):
    e = router_w.shape[1]
    rwt = router_w.astype(F32).T
    rwt_hi, rwt_lo = _split(rwt)
    h, y0, eidx, gates, rank, cnt = _moe_pre(xt, mod, g_pre, rwt_hi, rwt_lo, router_b.astype(F32).reshape(e, 1),
                                             ws_gate.astype(BF16), ws_up.astype(BF16), ws_down.astype(BF16),
                                             n_tok, seq, batch)
    pstart, lastblk, block_e, nused, nb = _dispatch_plan(cnt[:, 0], n_tok * TOP_K)
    pos = _moe_pos(eidx, rank, pstart.astype(F32).reshape(e, 1))
    xg = _moe_dispatch(lastblk, nused, pos, h, nb)
    og = _moe_experts(block_e, nused, xg, w_gate, w_up, w_down, layer)
    return _moe_combine(pos, og, gates, y0, xt, mod, g_post, n_tok, seq, batch)


def kernel(x, c, ctx, c_ctx, ada_w, ada_b, g_pre_mix, g_post_mix, g_pre_ffn, g_post_ffn, ab_w_in, ab_w_out, gla_dw_f, gla_db_f, gla_dw_b, gla_db_b, gla_norm_g, conv_w, conv_b, conv_ln_g, conv_ln_b, na_w_in, na_w_out, na_rpb, moe_router_w, moe_router_b, moe_w_gate, moe_w_up, moe_w_down, moe_ws_gate, moe_ws_up, moe_ws_down):
    batch, seq, d = x.shape
    ctx_len = ctx.shape[1]
    depth = ada_w.shape[0]
    n_x = batch * seq
    n_all = n_x + batch * ctx_len
    assert seq % ROW_TILE == 0 and (batch * ctx_len) % ROW_TILE == 0 and n_x % ctx_len == 0

    xt = jnp.concatenate([x.reshape(n_x, d), ctx.reshape(batch * ctx_len, d)], axis=0)

    cc = jnp.zeros((ADA_ROWS, d), F32).at[:batch].set(c).at[batch].set(c_ctx)
    mods = _ada_mod(cc, ada_w, ada_b)[:, :batch + 1].reshape(depth, batch + 1, ADA_CHUNKS, d)

    for layer in range(depth):
        last = layer == depth - 1
        i = layer // 2
        mod = mods[layer]
        if layer % 2 == 0:
            w_in = ab_w_in[i]
            cuts = np.cumsum([0, GLA_QK, GLA_QK, GLA_V, GLA_V, GLA_LOWRANK, GLA_LOWRANK, CONV_CH, CONV_CH])
            seg = lambda j: w_in[:, cuts[j]:cuts[j + 1]]
            lr_pad = jnp.zeros((d, LANES - 2 * GLA_LOWRANK), w_in.dtype)
            w_cat = jnp.concatenate([seg(0), seg(1), seg(2), seg(3), seg(6), seg(7), seg(4), seg(5), lr_pad],
                                    axis=1).astype(BF16)
            qk, v, g, glu, lr = _prenorm_proj(xt, mod, g_pre_mix[layer], w_cat,
                                              (2 * GLA_QK, GLA_V, GLA_V, 2 * CONV_CH, LANES), 0, 1, seq, batch)
            dwcat = jnp.zeros((LANES, 2 * GLA_QK), F32)
            dwcat = dwcat.at[:GLA_LOWRANK, :GLA_QK].set(gla_dw_f[i])
            dwcat = dwcat.at[GLA_LOWRANK:2 * GLA_LOWRANK, GLA_QK:].set(gla_dw_b[i]).astype(BF16)
            dbcat = jnp.concatenate([gla_db_f[i], gla_db_b[i]]).reshape(1, 2 * GLA_QK).astype(F32)
            a_out = _gla(qk, v, g, lr, dwcat, dbcat, gla_norm_g[i].reshape(1, GLA_V).astype(F32),
                         seq, ctx_len, batch)
            cw = conv_w[i].astype(F32)
            cvec = lambda t: t.reshape(1, CONV_CH).astype(F32)
            cv_x = _conv_module(glu, cw, cvec(conv_b[i]), cvec(conv_ln_g[i]), cvec(conv_ln_b[i]), seq, 0, batch)
            cv_c = _conv_module(glu, cw, cvec(conv_b[i]), cvec(conv_ln_g[i]), cvec(conv_ln_b[i]), ctx_len,
                                n_x // ctx_len, batch)
            cv = jnp.concatenate([cv_x, cv_c], axis=0)
            n_rows = n_x if last else n_all
            xt = _outproj([a_out, cv], ab_w_out[i].astype(BF16), xt, mod, g_post_mix[layer], 2, n_rows, seq, batch)
        else:
            (z,) = _prenorm_proj(xt, mod, g_pre_mix[layer], na_w_in[i].astype(BF16),
                                 (3 * NA_HEADS * NA_DH,), 0, 1, seq, batch)
            bias = _na_bias_table(na_rpb[i], seq // GRID_W)
            o = _na_attention(z, bias, seq, ctx_len, batch)
            if last:
                xt = _outproj([o], na_w_out[i].astype(BF16), xt, mod, g_post_mix[layer], 2, n_x, seq, batch)
            else:
                raise NotImplementedError("context update after a neighbourhood-attention layer")
        n_tok = n_x if last else n_all
        xt = _moe_layer(xt, mod, g_pre_ffn[layer], g_post_ffn[layer], moe_router_w[layer], moe_router_b[layer],
                        moe_w_gate, moe_w_up, moe_w_down, moe_ws_gate[layer],
                        moe_ws_up[layer], moe_ws_down[layer], n_tok, seq, batch, layer)
    return xt[:n_x].reshape(batch, seq, d)
```

```python
import functools

import numpy as np
import jax
import jax.numpy as jnp
from jax import lax
from jax.experimental import pallas as pl
from jax.experimental.pallas import tpu as pltpu

F32 = jnp.float32
BF16 = jnp.bfloat16

GRID_W = 64
RMS_EPS = 1e-6
LN_EPS = 1e-5
ADA_CHUNKS = 6
GLA_HEADS = 4
GLA_DK = 64
GLA_DV = 128
GLA_LOWRANK = 16
GLA_GATE_NORM = 16.0
GLA_CHUNK = 64
ROPE_BASE = 10000.0
GLA_QK = GLA_HEADS * GLA_DK
GLA_V = GLA_HEADS * GLA_DV
CONV_CH = 512
CONV_WIDTH = 31
NA_HEADS = 16
NA_DH = 64
NA_WIN_ROWS = 8
NA_WIN_COLS = 16
N_EXPERTS = 256
TOP_K = 8
N_GROUPS = 8
TOPK_GROUPS = 4
ROUTE_SCALE = 2.5

LANES = 128
ROW_TILE = 512
ADA_ROWS = 24
ADA_TN = 1536
CONV_RC = 32
CONV_HALO = 16
GLA_PAIR = 2
NA_ROW_UNROLL = 4
MOE_RB = 512
DISP_TT = 128
VMEM_LIMIT = 56 * 1024 * 1024

_NT = (((1,), (1,)), ((), ()))
_TN = (((0,), (0,)), ((), ()))


def _params(sem, **kw):
    return pltpu.CompilerParams(dimension_semantics=sem, vmem_limit_bytes=VMEM_LIMIT, **kw)


def _sigmoid(x):
    return 1.0 / (1.0 + jnp.exp(-x))


def _split(a):
    hi = a.astype(BF16)
    lo = (a - hi.astype(F32)).astype(BF16)
    return hi, lo


def _dot3(a, b_hi, b_lo, dims=(((1,), (0,)), ((), ()))):
    a_hi, a_lo = _split(a)
    d = lambda x, y: lax.dot_general(x, y, dims, preferred_element_type=F32)
    return d(a_hi, b_hi) + (d(a_hi, b_lo) + d(a_lo, b_hi))


def _prenorm_mod(x, g, mod_ref, shift_i, scale_i):
    ms = jnp.mean(x * x, axis=-1, keepdims=True)
    h = x * lax.rsqrt(ms + RMS_EPS) * g
    return h * (1.0 + mod_ref[0, scale_i:scale_i + 1, :]) + mod_ref[0, shift_i:shift_i + 1, :]


def _mod_index(tile_rows, seq, batch):
    per_batch = seq // tile_rows
    return lambda i, *_: (jnp.minimum(i // per_batch, batch), 0, 0)


def _ada_kernel(c_ref, w_ref, b_ref, o_ref):
    c = c_ref[...]
    s = c * _sigmoid(c)
    w_hi, w_lo = _split(w_ref[...])
    o_ref[...] = _dot3(s, w_hi, w_lo) + b_ref[...]


def _ada_mod(cc, ada_w, ada_b):
    depth, d, n = ada_w.shape
    return pl.pallas_call(
        _ada_kernel,
        grid=(depth, n // ADA_TN),
        in_specs=[pl.BlockSpec((ADA_ROWS, d), lambda l, j: (0, 0)),
                  pl.BlockSpec((None, d, ADA_TN), lambda l, j: (l, 0, j)),
                  pl.BlockSpec((None, 1, ADA_TN), lambda l, j: (l, 0, j))],
        out_specs=pl.BlockSpec((None, ADA_ROWS, ADA_TN), lambda l, j: (l, 0, j)),
        out_shape=jax.ShapeDtypeStruct((depth, ADA_ROWS, n), F32),
        compiler_params=_params(("arbitrary", "arbitrary")),
        name="ada_mod",
    )(cc, ada_w, ada_b.reshape(depth, 1, n))


def _prenorm_proj_kernel(x_ref, mod_ref, g_ref, w_ref, *o_refs, shift_i, scale_i):
    h = _prenorm_mod(x_ref[...], g_ref[...], mod_ref, shift_i, scale_i).astype(BF16)
    off = 0
    for o_ref in o_refs:
        n = o_ref.shape[-1]
        o_ref[...] = jnp.dot(h, w_ref[:, off:off + n], preferred_element_type=F32).astype(o_ref.dtype)
        off += n


def _prenorm_proj(xt, mod, g, w, splits, shift_i, scale_i, seq, batch):
    rows, d = xt.shape
    n = w.shape[1]
    return pl.pallas_call(
        functools.partial(_prenorm_proj_kernel, shift_i=shift_i, scale_i=scale_i),
        grid=(rows // ROW_TILE,),
        in_specs=[pl.BlockSpec((ROW_TILE, d), lambda i: (i, 0)),
                  pl.BlockSpec((1, ADA_CHUNKS, d), _mod_index(ROW_TILE, seq, batch)),
                  pl.BlockSpec((1, d), lambda i: (0, 0)),
                  pl.BlockSpec((d, n), lambda i: (0, 0))],
        out_specs=[pl.BlockSpec((ROW_TILE, s), lambda i: (i, 0)) for s in splits],
        out_shape=[jax.ShapeDtypeStruct((rows, s), BF16) for s in splits],
        compiler_params=_params(("arbitrary",)),
        name="prenorm_proj",
    )(xt, mod, g.reshape(1, d), w)


def _gla_kernel(qkx_ref, vx_ref, gx_ref, lrx_ref, qkc_ref, vc_ref, gc_ref, lrc_ref,
                dw_ref, db_ref, cos_ref, sin_ref, gg_ref, ox_ref, oc_ref,
                sf_ref, sb_ref, accx_ref, accc_ref, *, n_x, n_c):
    L = GLA_CHUNK
    row = lax.broadcasted_iota(jnp.int32, (L, L), 0)
    col = lax.broadcasted_iota(jnp.int32, (L, L), 1)
    lower = row >= col
    upper = col >= row
    tri_f = jnp.where(lower, 1.0, 0.0).astype(BF16)
    tri_b = jnp.where(upper, 1.0, 0.0).astype(BF16)
    lane = lax.broadcasted_iota(jnp.int32, (1, GLA_QK), 1)
    head_of_lane = lane // GLA_DK
    first_half = (lane % (GLA_DK // 2)) < (GLA_DK // 4)
    bd_mask = (lax.broadcasted_iota(jnp.int32, (GLA_V, GLA_QK), 0) // GLA_DV
               == lax.broadcasted_iota(jnp.int32, (GLA_V, GLA_QK), 1) // GLA_DK)

    def swap_pairs(t):
        nf = GLA_DK // 4
        return jnp.where(first_half, pltpu.roll(t, GLA_QK - nf, 1), pltpu.roll(t, nf, 1))

    def chunk(fwd, c, qk_ref, v_ref, lr_ref, latent):
        rows = pl.ds(pl.multiple_of(c * L, L), L)
        q = qk_ref[rows, 0:GLA_QK].astype(F32) * (GLA_DK ** -0.5)
        k = qk_ref[rows, GLA_QK:2 * GLA_QK].astype(F32)
        if latent:
            cs = cos_ref[rows, :]
            sn = sin_ref[rows, :]
            q = q * cs + swap_pairs(q) * sn
            k = k * cs + swap_pairs(k) * sn
        dsl = slice(0, GLA_QK) if fwd else slice(GLA_QK, 2 * GLA_QK)
        pre = jnp.dot(lr_ref[rows, :], dw_ref[:, dsl], preferred_element_type=F32) + db_ref[:, dsl]
        yield
        logd = (jnp.minimum(pre, 0.0) - jnp.log(1.0 + jnp.exp(-jnp.abs(pre)))) * (1.0 / GLA_GATE_NORM)
        d_hi, d_lo = _split(logd)
        tri = tri_f if fwd else tri_b
        b = (jnp.dot(tri, d_hi, preferred_element_type=F32)
             + jnp.dot(tri, d_lo, preferred_element_type=F32))
        yield
        mid = L // 2 - 1 if fwd else L // 2
        end = L - 1 if fwd else 0
        b_mid = b[mid:mid + 1, :]
        b_end = b[end:end + 1, :]
        qe = q * jnp.exp(b - b_mid)
        ke = (k * jnp.exp(b_mid - b)).astype(BF16)
        q_in = (q * jnp.exp(b)).astype(BF16)
        k_out = (k * jnp.exp(b_end - b)).astype(BF16)
        dec = jnp.exp(b_end)
        vv = v_ref[rows, :]
        mask = lower if fwd else upper
        scores = []
        for h in range(GLA_HEADS):
            qm = jnp.where(head_of_lane == h, qe, 0.0).astype(BF16)
            scores.append(lax.dot_general(qm, ke, _NT, preferred_element_type=F32))
        yield
        outs = []
        for h in range(GLA_HEADS):
            a = jnp.where(mask, scores[h], 0.0).astype(BF16)
            outs.append(jnp.dot(a, vv[:, h * GLA_DV:(h + 1) * GLA_DV], preferred_element_type=F32))
        o_intra = jnp.concatenate(outs, axis=-1)
        yield

        def last():
            s_ref = sf_ref if fwd else sb_ref
            st = s_ref[...]
            o_inter = lax.dot_general(q_in, st.astype(BF16), _NT, preferred_element_type=F32)
            upd = lax.dot_general(vv, k_out, _TN, preferred_element_type=F32)
            s_ref[...] = jnp.where(bd_mask, st * dec + upd, 0.0)
            return o_intra + o_inter

        return last

    def run_chunks(specs):
        gens = [chunk(*sp) for sp in specs]
        lasts = [None] * len(gens)
        live = list(range(len(gens)))
        while live:
            for gi in list(live):
                try:
                    next(gens[gi])
                except StopIteration as stop:
                    lasts[gi] = stop.value
                    live.remove(gi)
        return [fn() for fn in lasts]

    def finish(o, g_ref, c):
        rows = pl.ds(pl.multiple_of(c * L, L), L)
        g = g_ref[rows, :].astype(F32)
        parts = []
        for h in range(GLA_HEADS):
            oh = o[:, h * GLA_DV:(h + 1) * GLA_DV]
            ms = jnp.mean(oh * oh, axis=-1, keepdims=True)
            parts.append(oh * lax.rsqrt(ms + RMS_EPS))
        on = jnp.concatenate(parts, axis=-1) * gg_ref[...]
        return (on * (g * _sigmoid(g))).astype(BF16)

    def run_seq(n, qk_ref, v_ref, g_ref, lr_ref, acc_ref, o_ref, latent):
        half = n // 2

        def rows_of(c):
            return pl.ds(pl.multiple_of(c * L, L), L)

        def chunks_of(i):
            cs = [(True, i * GLA_PAIR + u) for u in range(GLA_PAIR)]
            cs += [(False, n - 1 - (i * GLA_PAIR + u)) for u in range(GLA_PAIR)]
            return cs

        def first(i, carry):
            cs = chunks_of(i)
            res = run_chunks([(f, c, qk_ref, v_ref, lr_ref, latent) for f, c in cs])
            for (f, c), o in zip(cs, res):
                acc_ref[rows_of(c), :] = o
            return carry

        def second(i, carry):
            cs = chunks_of(i)
            res = run_chunks([(f, c, qk_ref, v_ref, lr_ref, latent) for f, c in cs])
            for (f, c), o in zip(cs, res):
                o_ref[rows_of(c), :] = finish(acc_ref[rows_of(c), :] + o, g_ref, c)
            return carry

        assert half % GLA_PAIR == 0
        lax.fori_loop(0, half // GLA_PAIR, first, 0)
        lax.fori_loop(half // GLA_PAIR, n // GLA_PAIR, second, 0)

    sf_ref[...] = jnp.zeros_like(sf_ref)
    sb_ref[...] = jnp.zeros_like(sb_ref)
    run_seq(n_c, qkc_ref, vc_ref, gc_ref, lrc_ref, accc_ref, oc_ref, False)
    run_seq(n_x, qkx_ref, vx_ref, gx_ref, lrx_ref, accx_ref, ox_ref, True)


def _rope_tables(seq):
    half = GLA_DK // 2
    nf = half // 2
    inv = ROPE_BASE ** (-jnp.arange(nf, dtype=F32) / nf)
    t = jnp.arange(seq)
    rows = (t // GRID_W).astype(F32)
    cols = (t % GRID_W).astype(F32)
    ar = rows[:, None] * inv[None, :]
    ac = cols[:, None] * inv[None, :]
    cos_h = jnp.concatenate([jnp.cos(ar), jnp.cos(ar), jnp.cos(ac), jnp.cos(ac)], axis=-1)
    sin_h = jnp.concatenate([-jnp.sin(ar), jnp.sin(ar), -jnp.sin(ac), jnp.sin(ac)], axis=-1)
    return jnp.tile(cos_h, (1, GLA_HEADS)), jnp.tile(sin_h, (1, GLA_HEADS))


def _gla(qk, v, g, lr, dwcat, dbcat, gla_g, seq, ctx_len, batch):
    rows = qk.shape[0]
    cos_t, sin_t = _rope_tables(seq)
    cb = (batch * seq) // ctx_len
    xs = lambda w: pl.BlockSpec((seq, w), lambda b: (b, 0))
    cs = lambda w: pl.BlockSpec((ctx_len, w), lambda b: (cb + b, 0))
    full = lambda a: pl.BlockSpec(a.shape, lambda b: (0,) * a.ndim)
    out_x, out_c = pl.pallas_call(
        functools.partial(_gla_kernel, n_x=seq // GLA_CHUNK, n_c=ctx_len // GLA_CHUNK),
        grid=(batch,),
        in_specs=[xs(2 * GLA_QK), xs(GLA_V), xs(GLA_V), xs(LANES),
                  cs(2 * GLA_QK), cs(GLA_V), cs(GLA_V), cs(LANES),
                  full(dwcat), full(dbcat), full(cos_t), full(sin_t), full(gla_g)],
        out_specs=[pl.BlockSpec((seq, GLA_V), lambda b: (b, 0)),
                   pl.BlockSpec((ctx_len, GLA_V), lambda b: (b, 0))],
        out_shape=[jax.ShapeDtypeStruct((batch * seq, GLA_V), BF16),
                   jax.ShapeDtypeStruct((batch * ctx_len, GLA_V), BF16)],
        scratch_shapes=[pltpu.VMEM((GLA_V, GLA_QK), F32), pltpu.VMEM((GLA_V, GLA_QK), F32),
                        pltpu.VMEM((seq, GLA_V), F32), pltpu.VMEM((ctx_len, GLA_V), F32)],
        compiler_params=_params(("arbitrary",)),
        name="gla",
    )(qk, v, g, lr, qk, v, g, lr, dwcat, dbcat, cos_t, sin_t, gla_g)
    del rows
    return jnp.concatenate([out_x, out_c], axis=0)


def _conv_kernel(glu_ref, w_ref, cb_ref, lg_ref, lb_ref, o_ref, pad_ref, win_ref, *, t_len):
    ch = CONV_CH
    zeros = jnp.zeros((CONV_HALO, ch), F32)
    pad_ref[0:CONV_HALO, :] = zeros
    pad_ref[CONV_HALO + t_len:2 * CONV_HALO + t_len, :] = zeros
    fill = 128

    def fill_body(i, carry):
        r = pl.multiple_of(i * fill, fill)
        a = glu_ref[pl.ds(r, fill), 0:ch].astype(F32)
        b = glu_ref[pl.ds(r, fill), ch:2 * ch].astype(F32)
        pad_ref[pl.ds(CONV_HALO + r, fill), :] = a * _sigmoid(b)
        return carry

    lax.fori_loop(0, t_len // fill, fill_body, 0)
    first_tap = CONV_HALO - CONV_WIDTH // 2

    def body(i, carry):
        base = pl.multiple_of(i * CONV_RC, CONV_RC)
        win_ref[...] = pad_ref[pl.ds(base, CONV_RC + 2 * CONV_HALO), :]
        acc = jnp.zeros((CONV_RC, ch), F32)
        for j in range(CONV_WIDTH):
            acc = acc + w_ref[j:j + 1, :] * win_ref[first_tap + j:first_tap + j + CONV_RC, :]
        acc = acc + cb_ref[...]
        mu = jnp.mean(acc, axis=-1, keepdims=True)
        cen = acc - mu
        var = jnp.mean(cen * cen, axis=-1, keepdims=True)
        y = cen * lax.rsqrt(var + LN_EPS) * lg_ref[...] + lb_ref[...]
        o_ref[pl.ds(base, CONV_RC), :] = (y * _sigmoid(y)).astype(BF16)
        return carry

    lax.fori_loop(0, t_len // CONV_RC, body, 0)


def _conv_module(glu, conv_w, conv_b, ln_g, ln_b, t_len, first_block, batch):
    full = lambda a: pl.BlockSpec(a.shape, lambda b: (0,) * a.ndim)
    return pl.pallas_call(
        functools.partial(_conv_kernel, t_len=t_len),
        grid=(batch,),
        in_specs=[pl.BlockSpec((t_len, 2 * CONV_CH), lambda b: (first_block + b, 0)),
                  full(conv_w), full(conv_b), full(ln_g), full(ln_b)],
        out_specs=pl.BlockSpec((t_len, CONV_CH), lambda b: (b, 0)),
        out_shape=jax.ShapeDtypeStruct((batch * t_len, CONV_CH), BF16),
        scratch_shapes=[pltpu.VMEM((t_len + 2 * CONV_HALO, CONV_CH), F32),
                        pltpu.VMEM((CONV_RC + 2 * CONV_HALO, CONV_CH), F32)],
        compiler_params=_params(("arbitrary",)),
        name="conv_module",
    )(glu, conv_w, conv_b, ln_g, ln_b)


def _outproj_kernel(*refs, n_lhs, gate_i):
    lhs_refs = refs[:n_lhs]
    w_ref, x_ref, mod_ref, g_ref, o_ref = refs[n_lhs:]
    off = 0
    y = None
    for l_ref in lhs_refs:
        kk = l_ref.shape[-1]
        t = jnp.dot(l_ref[...], w_ref[off:off + kk, :], preferred_element_type=F32)
        y = t if y is None else y + t
        off += kk
    ms = jnp.mean(y * y, axis=-1, keepdims=True)
    yn = y * lax.rsqrt(ms + RMS_EPS) * g_ref[...]
    o_ref[...] = x_ref[...] + mod_ref[0, gate_i:gate_i + 1, :] * yn


def _outproj(lhs_list, w, xt, mod, g, gate_i, n_rows, seq, batch):
    d = xt.shape[1]
    n_lhs = len(lhs_list)
    in_specs = [pl.BlockSpec((ROW_TILE, a.shape[1]), lambda i: (i, 0)) for a in lhs_list]
    in_specs += [pl.BlockSpec(w.shape, lambda i: (0, 0)),
                 pl.BlockSpec((ROW_TILE, d), lambda i: (i, 0)),
                 pl.BlockSpec((1, ADA_CHUNKS, d), _mod_index(ROW_TILE, seq, batch)),
                 pl.BlockSpec((1, d), lambda i: (0, 0))]
    return pl.pallas_call(
        functools.partial(_outproj_kernel, n_lhs=n_lhs, gate_i=gate_i),
        grid=(n_rows // ROW_TILE,),
        in_specs=in_specs,
        out_specs=pl.BlockSpec((ROW_TILE, d), lambda i: (i, 0)),
        out_shape=jax.ShapeDtypeStruct(xt.shape, F32),
        input_output_aliases={n_lhs + 1: 0},
        compiler_params=_params(("arbitrary",)),
        name="outproj",
    )(*lhs_list, w, xt, mod, g.reshape(1, d))


def _na_kernel(q_ref, k_ref, v_ref, kc_ref, vc_ref, bias_ref, o_ref, *, n_rows):
    wr = NA_WIN_ROWS
    lane = lax.broadcasted_iota(jnp.int32, (1, LANES), 1)
    head0 = lane < NA_DH
    kc = kc_ref[...]
    vc = vc_ref[...]

    def body(it, carry):
        rs = [it * NA_ROW_UNROLL + u for u in range(NA_ROW_UNROLL)]
        r0s = [jnp.clip(r - wr // 2, 0, n_rows - wr) for r in rs]
        win = [pl.ds(pl.multiple_of(r0 * GRID_W, GRID_W), wr * GRID_W) for r0 in r0s]
        scores = []
        for r, r0, w in zip(rs, r0s, win):
            q2 = q_ref[pl.ds(pl.multiple_of(r * GRID_W, GRID_W), GRID_W), :] * (NA_DH ** -0.5)
            zero = jnp.zeros_like(q2)
            qs = jnp.concatenate([jnp.where(head0, q2, zero), jnp.where(head0, zero, q2)], axis=0)
            bias = bias_ref[:, r - r0].reshape(2 * GRID_W, wr * GRID_W)
            s_w = lax.dot_general(qs, k_ref[w, :], _NT, preferred_element_type=F32) + bias
            s_c = lax.dot_general(qs, kc, _NT, preferred_element_type=F32)
            scores.append((s_w, s_c))
        probs = []
        for s_w, s_c in scores:
            m = jnp.maximum(jnp.max(s_w, axis=-1, keepdims=True), jnp.max(s_c, axis=-1, keepdims=True))
            p_w = jnp.exp(s_w - m)
            p_c = jnp.exp(s_c - m)
            den = jnp.sum(p_w, axis=-1, keepdims=True) + jnp.sum(p_c, axis=-1, keepdims=True)
            probs.append((p_w.astype(BF16), p_c.astype(BF16), den))
        for r, w, (p_w, p_c, den) in zip(rs, win, probs):
            o = (jnp.dot(p_w, v_ref[w, :], preferred_element_type=F32)
                 + jnp.dot(p_c, vc, preferred_element_type=F32)) / den
            o_ref[pl.ds(pl.multiple_of(r * GRID_W, GRID_W), GRID_W), :] = (
                jnp.where(head0, o[:GRID_W], o[GRID_W:]).astype(BF16))
        return carry

    assert n_rows % NA_ROW_UNROLL == 0
    lax.fori_loop(0, n_rows // NA_ROW_UNROLL, body, 0)


def _na_bias_table(rpb, n_rows):
    wr = min(NA_WIN_ROWS, n_rows)
    qc = np.arange(GRID_W)
    cstart = np.clip(qc - NA_WIN_COLS // 2, 0, GRID_W - NA_WIN_COLS)
    kcol = np.arange(GRID_W)
    ok = (kcol[None, :] >= cstart[:, None]) & (kcol[None, :] < cstart[:, None] + NA_WIN_COLS)
    cidx = np.clip(kcol[None, :] - qc[:, None] + NA_WIN_COLS - 1, 0, 2 * NA_WIN_COLS - 2)
    delta = np.arange(wr)
    j = np.arange(wr)
    ridx = j[None, :] - delta[:, None] + NA_WIN_ROWS - 1
    t = rpb[:, ridx][:, :, :, cidx]
    t = jnp.where(jnp.asarray(ok)[None, None, None], t.astype(F32), -jnp.inf)
    t = t.transpose(0, 1, 3, 2, 4)
    return t.reshape(rpb.shape[0], wr, GRID_W, wr * GRID_W)


def _na_attention(z, bias, seq, ctx_len, batch):
    n_pairs = NA_HEADS * NA_DH // LANES
    n_rows = seq // GRID_W
    cb = (batch * seq) // ctx_len
    return pl.pallas_call(
        functools.partial(_na_kernel, n_rows=n_rows),
        grid=(n_pairs, batch),
        in_specs=[pl.BlockSpec((seq, LANES), lambda p, b: (b, p)),
                  pl.BlockSpec((seq, LANES), lambda p, b: (b, n_pairs + p)),
                  pl.BlockSpec((seq, LANES), lambda p, b: (b, 2 * n_pairs + p)),
                  pl.BlockSpec((ctx_len, LANES), lambda p, b: (cb + b, n_pairs + p)),
                  pl.BlockSpec((ctx_len, LANES), lambda p, b: (cb + b, 2 * n_pairs + p)),
                  pl.BlockSpec((2,) + bias.shape[1:], lambda p, b: (p, 0, 0, 0))],
        out_specs=pl.BlockSpec((seq, LANES), lambda p, b: (b, p)),
        out_shape=jax.ShapeDtypeStruct((batch * seq, NA_HEADS * NA_DH), BF16),
        compiler_params=_params(("arbitrary", "arbitrary")),
        name="na_attention",
    )(z, z, z, z, z, bias)


def _store_token_tiles(ref, val, row0=0):
    n, d = val.shape
    nc = d // LANES
    for k in range(nc):
        ref[pl.ds(row0 * nc + k, n, stride=nc), :] = val[:, k * LANES:(k + 1) * LANES]


def _load_token_tiles(ref, row0, n, nc, lead=None):
    chunks = []
    for k in range(nc):
        idx = pl.ds(row0 * nc + k, n, stride=nc)
        chunks.append(ref[idx, :] if lead is None else ref[lead, idx, :])
    return jnp.concatenate(chunks, axis=-1)


def _first_argmax(x, iota_f, n):
    m = jnp.max(x, axis=0, keepdims=True)
    idx = jnp.min(jnp.where(x == m, iota_f, float(n)), axis=0, keepdims=True)
    return m, idx


def _moe_pre_kernel(x_ref, mod_ref, g_ref, rwh_ref, rwl_ref, rb_ref, wsg_ref, wsu_ref, wsd_ref,
                    h_ref, y0_ref, eidx_ref, gates_ref, rank_ref, cnt_ref, base_ref):
    tm = x_ref.shape[0]
    n_e = rwh_ref.shape[0]

    @pl.when(pl.program_id(0) == 0)
    def _():
        base_ref[...] = jnp.zeros_like(base_ref)

    h = _prenorm_mod(x_ref[...], g_ref[...], mod_ref, 3, 4)
    _store_token_tiles(h_ref, h)
    hb = h.astype(BF16)
    gate = jnp.dot(hb, wsg_ref[...], preferred_element_type=F32)
    up = jnp.dot(hb, wsu_ref[...], preferred_element_type=F32)
    hid = (gate * _sigmoid(gate) * up).astype(BF16)
    y0_ref[...] = jnp.dot(hid, wsd_ref[...], preferred_element_type=F32)

    h_lo = (h - hb.astype(F32)).astype(BF16)
    d = lambda a, b: lax.dot_general(a, b, _NT, preferred_element_type=F32)
    logits = d(rwh_ref[...], hb) + (d(rwh_ref[...], h_lo) + d(rwl_ref[...], hb))
    scores = _sigmoid(logits)
    biased = scores + rb_ref[...]
    neg = -jnp.inf
    gsz = n_e // N_GROUPS
    io_g = lax.broadcasted_iota(jnp.int32, (gsz, tm), 0).astype(F32)
    grp_rows = []
    for gi in range(N_GROUPS):
        xg = biased[gi * gsz:(gi + 1) * gsz]
        m1, i1 = _first_argmax(xg, io_g, gsz)
        m2 = jnp.max(jnp.where(io_g == i1, neg, xg), axis=0, keepdims=True)
        grp_rows.append(m1 + m2)
    cur = jnp.concatenate(grp_rows, axis=0)
    io_n = lax.broadcasted_iota(jnp.int32, (N_GROUPS, tm), 0).astype(F32)
    keep = jnp.zeros((N_GROUPS, tm), F32)
    for _ in range(TOPK_GROUPS):
        _, ii = _first_argmax(cur, io_n, N_GROUPS)
        sel = io_n == ii
        keep = jnp.where(sel, 1.0, keep)
        cur = jnp.where(sel, neg, cur)
    cur = jnp.concatenate([jnp.where(keep[gi:gi + 1] > 0.0, biased[gi * gsz:(gi + 1) * gsz], neg)
                           for gi in range(N_GROUPS)], axis=0)
    io_e = lax.broadcasted_iota(jnp.int32, (n_e, tm), 0).astype(F32)
    idxs, gvals = [], []
    chosen = jnp.zeros((n_e, tm), F32)
    for _ in range(TOP_K):
        _, ii = _first_argmax(cur, io_e, n_e)
        sel = io_e == ii
        idxs.append(ii)
        gvals.append(jnp.sum(jnp.where(sel, scores, 0.0), axis=0, keepdims=True))
        chosen = jnp.where(sel, 1.0, chosen)
        cur = jnp.where(sel, neg, cur)
    gv = jnp.concatenate(gvals, axis=0)
    gates_ref[...] = gv / jnp.sum(gv, axis=0, keepdims=True) * ROUTE_SCALE
    eidx_ref[...] = jnp.concatenate(idxs, axis=0).astype(jnp.int32)

    earlier = (lax.broadcasted_iota(jnp.int32, (tm, tm), 0) < lax.broadcasted_iota(jnp.int32, (tm, tm), 1))
    before = jnp.dot(chosen.astype(BF16), jnp.where(earlier, 1.0, 0.0).astype(BF16), preferred_element_type=F32)
    before = before + base_ref[:, 0:1]
    ranks = [jnp.sum(jnp.where(io_e == ii, before, 0.0), axis=0, keepdims=True) for ii in idxs]
    rank_ref[...] = jnp.concatenate(ranks, axis=0).astype(jnp.int32)
    base_ref[...] = base_ref[...] + jnp.sum(chosen, axis=1, keepdims=True)
    cnt_ref[...] = base_ref[...].astype(jnp.int32)


def _moe_pre(xt, mod, g, rwt_hi, rwt_lo, rb, wsg, wsu, wsd, n_tok, seq, batch):
    d = xt.shape[1]
    e = rwt_hi.shape[0]
    full = lambda a: pl.BlockSpec(a.shape, lambda i: (0,) * a.ndim)
    tok_major = lambda: pl.BlockSpec((TOP_K, ROW_TILE), lambda i: (0, i))
    return pl.pallas_call(
        _moe_pre_kernel,
        grid=(n_tok // ROW_TILE,),
        in_specs=[pl.BlockSpec((ROW_TILE, d), lambda i: (i, 0)),
                  pl.BlockSpec((1, ADA_CHUNKS, d), _mod_index(ROW_TILE, seq, batch)),
                  pl.BlockSpec((1, d), lambda i: (0, 0)),
                  full(rwt_hi), full(rwt_lo), full(rb), full(wsg), full(wsu), full(wsd)],
        out_specs=[pl.BlockSpec((ROW_TILE * (d // LANES), LANES), lambda i: (i, 0)),
                   pl.BlockSpec((ROW_TILE, d), lambda i: (i, 0)),
                   tok_major(), tok_major(), tok_major(),
                   pl.BlockSpec((e, LANES), lambda i: (0, 0))],
        out_shape=[jax.ShapeDtypeStruct((n_tok * (d // LANES), LANES), F32),
                   jax.ShapeDtypeStruct((n_tok, d), F32),
                   jax.ShapeDtypeStruct((TOP_K, n_tok), jnp.int32),
                   jax.ShapeDtypeStruct((TOP_K, n_tok), F32),
                   jax.ShapeDtypeStruct((TOP_K, n_tok), jnp.int32),
                   jax.ShapeDtypeStruct((e, LANES), jnp.int32)],
        scratch_shapes=[pltpu.VMEM((e, LANES), F32)],
        compiler_params=_params(("arbitrary",)),
        name="moe_pre",
    )(xt, mod, g.reshape(1, d), rwt_hi, rwt_lo, rb, wsg, wsu, wsd)


def _moe_pos_kernel(eidx_ref, rank_ref, pstart_ref, pos_ref):
    n_e = pstart_ref.shape[0]
    tp = eidx_ref.shape[1]
    io_e = lax.broadcasted_iota(jnp.int32, (n_e, tp), 0)
    ps = pstart_ref[...]
    e = eidx_ref[...]
    rows = [jnp.sum(jnp.where(io_e == e[k:k + 1, :], ps, 0.0), axis=0, keepdims=True) for k in range(TOP_K)]
    pos_ref[...] = rank_ref[...] + jnp.concatenate(rows, axis=0).astype(jnp.int32)


def _moe_pos(eidx, rank, pstart):
    n_tok = eidx.shape[1]
    blk = lambda: pl.BlockSpec((TOP_K, ROW_TILE), lambda i: (0, i))
    return pl.pallas_call(
        _moe_pos_kernel,
        grid=(n_tok // ROW_TILE,),
        in_specs=[blk(), blk(), pl.BlockSpec(pstart.shape, lambda i: (0, 0))],
        out_specs=blk(),
        out_shape=jax.ShapeDtypeStruct(eidx.shape, jnp.int32),
        compiler_params=_params(("arbitrary",)),
        name="moe_pos",
    )(eidx, rank, pstart)


def _moe_dispatch_kernel(lastblk_ref, nused_ref, pos_hbm, h_hbm, xg_hbm, idx_smem, zbuf, hbuf, idx_sem, row_sem, z_sem,
                         h_sem, *, n_tiles, nc, n_blocks):
    i = pl.program_id(0)
    slot = i % 2
    n_e = lastblk_ref.shape[0]
    rows_per_step = DISP_TT * TOP_K

    def h_copy(t, s):
        return pltpu.make_async_copy(h_hbm.at[pl.ds(pl.multiple_of(t * (DISP_TT * nc), DISP_TT * nc), DISP_TT * nc), :],
                                     hbuf.at[s], h_sem.at[s])

    def idx_copy(t, s):
        return pltpu.make_async_copy(pos_hbm.at[:, pl.ds(pl.multiple_of(t * DISP_TT, DISP_TT), DISP_TT)],
                                     idx_smem.at[s], idx_sem.at[s])

    def block_copy(b):
        return pltpu.make_async_copy(zbuf, xg_hbm.at[pl.ds(pl.multiple_of(b * (MOE_RB * nc), MOE_RB * nc),
                                                           MOE_RB * nc), :], z_sem)

    def wait_rows(s):
        pltpu.make_async_copy(h_hbm.at[pl.ds(0, rows_per_step * nc), :], xg_hbm.at[pl.ds(0, rows_per_step * nc), :],
                              row_sem.at[s]).wait()

    @pl.when(i == 0)
    def _():
        idx_copy(0, 0).start()
        zbuf[...] = jnp.zeros_like(zbuf)

        def for_zero_blocks(fn):
            def per_expert(e, carry):
                @pl.when(lastblk_ref[e] >= 0)
                def _():
                    fn(block_copy(lastblk_ref[e]))
                return carry

            def per_block(b, carry):
                fn(block_copy(b))
                return carry

            lax.fori_loop(0, n_e, per_expert, 0)
            lax.fori_loop(nused_ref[0], n_blocks, per_block, 0)

        for_zero_blocks(lambda cp: cp.start())
        for_zero_blocks(lambda cp: cp.wait())

        h_copy(0, 0).start()

    idx_copy(i, slot).wait()
    hslot = i % 3
    h_copy(i, hslot).wait()

    @pl.when(i + 1 < n_tiles)
    def _():
        idx_copy(i + 1, 1 - slot).start()
        h_copy(i + 1, (i + 1) % 3).start()

    def body(r, carry):
        src = hbuf.at[hslot, pl.ds(pl.multiple_of(r * nc, nc), nc), :]
        for k in range(TOP_K):
            p = idx_smem[slot, k, r]
            pltpu.make_async_copy(src, xg_hbm.at[pl.ds(pl.multiple_of(p * nc, nc), nc), :],
                                  row_sem.at[slot]).start(priority=k % 2)
        return carry

    lax.fori_loop(0, DISP_TT, body, 0, unroll=2)

    @pl.when(i > 0)
    def _():
        wait_rows(1 - slot)

    @pl.when(i == n_tiles - 1)
    def _():
        wait_rows(slot)


def _moe_dispatch(lastblk, nused, pos, h, n_blocks):
    n_tok = pos.shape[1]
    nc = h.shape[0] // n_tok
    n_tiles = n_tok // DISP_TT
    grid_spec = pltpu.PrefetchScalarGridSpec(
        num_scalar_prefetch=2,
        grid=(n_tiles,),
        in_specs=[pl.BlockSpec(memory_space=pl.ANY), pl.BlockSpec(memory_space=pl.ANY)],
        out_specs=pl.BlockSpec(memory_space=pl.ANY),
        scratch_shapes=[pltpu.SMEM((2, TOP_K, DISP_TT), jnp.int32),
                        pltpu.VMEM((MOE_RB * nc, LANES), F32),
                        pltpu.VMEM((3, DISP_TT * nc, LANES), F32),
                        pltpu.SemaphoreType.DMA((2,)),
                        pltpu.SemaphoreType.DMA((2,)),
                        pltpu.SemaphoreType.DMA(()),
                        pltpu.SemaphoreType.DMA((3,))],
    )
    return pl.pallas_call(
        functools.partial(_moe_dispatch_kernel, n_tiles=n_tiles, nc=nc, n_blocks=n_blocks),
        grid_spec=grid_spec,
        out_shape=jax.ShapeDtypeStruct((n_blocks * MOE_RB * nc, LANES), F32),
        compiler_params=_params(("arbitrary",)),
        name="moe_dispatch",
    )(lastblk, nused, pos, h)


def _moe_expert_kernel(be_ref, nused_ref, x_ref, wg_ref, wu_ref, wd_ref, o_ref, wgb, wub, wdb):
    i = pl.program_id(0)
    nused = nused_ref[0]
    nc = x_ref.shape[0] // MOE_RB
    new_expert = jnp.logical_or(i == 0, be_ref[i] != be_ref[jnp.maximum(i - 1, 0)])

    @pl.when(jnp.logical_and(i < nused, new_expert))
    def _():
        wgb[...] = wg_ref[...].astype(BF16)
        wub[...] = wu_ref[...].astype(BF16)
        wdb[...] = wd_ref[...].astype(BF16)

    @pl.when(i < nused)
    def _():
        xb = _load_token_tiles(x_ref, 0, MOE_RB, nc).astype(BF16)
        gate = jnp.dot(xb, wgb[...], preferred_element_type=F32)
        up = jnp.dot(xb, wub[...], preferred_element_type=F32)
        hid = (gate * _sigmoid(gate) * up).astype(BF16)
        _store_token_tiles(o_ref, jnp.dot(hid, wdb[...], preferred_element_type=F32))

    @pl.when(i >= nused)
    def _():
        o_ref[...] = jnp.zeros_like(o_ref)


def _moe_experts(block_e, nused, xg, wg, wu, wd, layer):
    nb = block_e.shape[0]
    d = wg.shape[2]
    nc = d // LANES
    de = wg.shape[3]
    grid_spec = pltpu.PrefetchScalarGridSpec(
        num_scalar_prefetch=2,
        grid=(nb,),
        in_specs=[pl.BlockSpec((MOE_RB * nc, LANES), lambda i, be, nu: (jnp.minimum(i, nu[0] - 1), 0)),
                  pl.BlockSpec((None, None, d, de), lambda i, be, nu: (layer, be[i], 0, 0)),
                  pl.BlockSpec((None, None, d, de), lambda i, be, nu: (layer, be[i], 0, 0)),
                  pl.BlockSpec((None, None, de, d), lambda i, be, nu: (layer, be[i], 0, 0))],
        out_specs=pl.BlockSpec((MOE_RB * nc, LANES), lambda i, be, nu: (i, 0)),
        scratch_shapes=[pltpu.VMEM((d, de), BF16), pltpu.VMEM((d, de), BF16), pltpu.VMEM((de, d), BF16)],
    )
    return pl.pallas_call(
        _moe_expert_kernel,
        grid_spec=grid_spec,
        out_shape=jax.ShapeDtypeStruct((nb * MOE_RB * nc, LANES), F32),
        compiler_params=_params(("arbitrary",)),
        name="moe_experts",
    )(block_e, nused, xg, wg, wu, wd)


def _moe_combine_kernel(pos_hbm, og_hbm, gates_ref, y0_ref, x_ref, mod_ref, g_ref, o_ref,
                        idx_smem, buf, idx_sem, row_sem, *, n_tiles):
    i = pl.program_id(0)
    slot = i % 2
    nxt = 1 - slot
    n_rows = TOP_K * DISP_TT
    nc = buf.shape[1] // n_rows

    def idx_copy(t, s):
        return pltpu.make_async_copy(pos_hbm.at[:, pl.ds(pl.multiple_of(t * DISP_TT, DISP_TT), DISP_TT)],
                                     idx_smem.at[s], idx_sem.at[s])

    def issue_rows(s):
        def body(r, carry):
            for k in range(TOP_K):
                p = idx_smem[s, k, r]
                pltpu.make_async_copy(og_hbm.at[pl.ds(pl.multiple_of(p * nc, nc), nc), :],
                                      buf.at[s, pl.ds(pl.multiple_of((k * DISP_TT + r) * nc, nc), nc), :],
                                      row_sem.at[s]).start(priority=k % 2)
            return carry
        lax.fori_loop(0, DISP_TT, body, 0, unroll=2)

    def wait_rows(s):
        pltpu.make_async_copy(og_hbm.at[pl.ds(0, n_rows * nc), :], buf.at[s], row_sem.at[s]).wait()

    @pl.when(i == 0)
    def _():
        idx_copy(0, 0).start()
        idx_copy(0, 0).wait()
        issue_rows(0)

        @pl.when(n_tiles > 1)
        def _():
            idx_copy(1, 1).start()

    @pl.when(i + 1 < n_tiles)
    def _():
        idx_copy(i + 1, nxt).wait()
        issue_rows(nxt)

    wait_rows(slot)

    @pl.when(i + 2 < n_tiles)
    def _():
        idx_copy(i + 2, slot).start()

    gates = gates_ref[...].T
    y = y0_ref[...]
    for k in range(TOP_K):
        y = y + gates[:, k:k + 1] * _load_token_tiles(buf, k * DISP_TT, DISP_TT, nc, lead=slot)
    ms = jnp.mean(y * y, axis=-1, keepdims=True)
    yn = y * lax.rsqrt(ms + RMS_EPS) * g_ref[...]
    o_ref[...] = x_ref[...] + mod_ref[0, 5:6, :] * yn


def _moe_combine(pos, og, gates, y0, xt, mod, g, n_tok, seq, batch):
    d = xt.shape[1]
    n_tiles = n_tok // DISP_TT
    row_spec = lambda w: pl.BlockSpec((DISP_TT, w), lambda i: (i, 0))
    return pl.pallas_call(
        functools.partial(_moe_combine_kernel, n_tiles=n_tiles),
        grid=(n_tiles,),
        in_specs=[pl.BlockSpec(memory_space=pl.ANY),
                  pl.BlockSpec(memory_space=pl.ANY),
                  pl.BlockSpec((TOP_K, DISP_TT), lambda i: (0, i)), row_spec(d), row_spec(d),
                  pl.BlockSpec((1, ADA_CHUNKS, d), _mod_index(DISP_TT, seq, batch)),
                  pl.BlockSpec((1, d), lambda i: (0, 0))],
        out_specs=row_spec(d),
        out_shape=jax.ShapeDtypeStruct(xt.shape, F32),
        input_output_aliases={4: 0},
        scratch_shapes=[pltpu.SMEM((2, TOP_K, DISP_TT), jnp.int32),
                        pltpu.VMEM((2, TOP_K * DISP_TT * (d // LANES), LANES), F32),
                        pltpu.SemaphoreType.DMA((2,)),
                        pltpu.SemaphoreType.DMA((2,))],
        compiler_params=_params(("arbitrary",)),
        name="moe_combine",
    )(pos, og, gates, y0, xt, mod, g.reshape(1, d))


def _dispatch_plan(counts, n_assign):
    e = counts.shape[0]
    padded = (counts + MOE_RB - 1) // MOE_RB * MOE_RB
    pend = jnp.cumsum(padded)
    pstart = pend - padded
    nb = -(-(n_assign + e * (MOE_RB - 1)) // MOE_RB)
    first_row = jnp.arange(nb, dtype=jnp.int32) * MOE_RB
    block_e = jnp.minimum(jnp.sum((pend[None, :] <= first_row[:, None]).astype(jnp.int32), axis=1), e - 1)
    nused = (pend[-1] // MOE_RB).astype(jnp.int32).reshape(1)
    lastblk = jnp.where(counts > 0, pend // MOE_RB - 1, -1)
    return pstart.astype(jnp.int32), lastblk.astype(jnp.int32), block_e.astype(jnp.int32), nused, nb


def _moe_layer(xt, mod, g_pre, g_post, router_w, router_b, w_gate, w_up, w_down, ws_gate, ws_up, ws_down,
               n_tok, seq, batch, layer):
    e = router_w.shape[1]
    rwt = router_w.astype(F32).T
    rwt_hi, rwt_lo = _split(rwt)
    h, y0, eidx, gates, rank, cnt = _moe_pre(xt, mod, g_pre, rwt_hi, rwt_lo, router_b.astype(F32).reshape(e, 1),
                                             ws_gate.astype(BF16), ws_up.astype(BF16), ws_down.astype(BF16),
                                             n_tok, seq, batch)
    pstart, lastblk, block_e, nused, nb = _dispatch_plan(cnt[:, 0], n_tok * TOP_K)
    pos = _moe_pos(eidx, rank, pstart.astype(F32).reshape(e, 1))
    xg = _moe_dispatch(lastblk, nused, pos, h, nb)
    og = _moe_experts(block_e, nused, xg, w_gate, w_up, w_down, layer)
    return _moe_combine(pos, og, gates, y0, xt, mod, g_post, n_tok, seq, batch)


def kernel(x, c, ctx, c_ctx, ada_w, ada_b, g_pre_mix, g_post_mix, g_pre_ffn, g_post_ffn, ab_w_in, ab_w_out, gla_dw_f, gla_db_f, gla_dw_b, gla_db_b, gla_norm_g, conv_w, conv_b, conv_ln_g, conv_ln_b, na_w_in, na_w_out, na_rpb, moe_router_w, moe_router_b, moe_w_gate, moe_w_up, moe_w_down, moe_ws_gate, moe_ws_up, moe_ws_down):
    batch, seq, d = x.shape
    ctx_len = ctx.shape[1]
    depth = ada_w.shape[0]
    n_x = batch * seq
    n_all = n_x + batch * ctx_len
    assert seq % ROW_TILE == 0 and (batch * ctx_len) % ROW_TILE == 0 and n_x % ctx_len == 0

    xt = jnp.concatenate([x.reshape(n_x, d), ctx.reshape(batch * ctx_len, d)], axis=0)

    cc = jnp.zeros((ADA_ROWS, d), F32).at[:batch].set(c).at[batch].set(c_ctx)
    mods = _ada_mod(cc, ada_w, ada_b)[:, :batch + 1].reshape(depth, batch + 1, ADA_CHUNKS, d)

    for layer in range(depth):
        last = layer == depth - 1
        i = layer // 2
        mod = mods[layer]
        if layer % 2 == 0:
            w_in = ab_w_in[i]
            cuts = np.cumsum([0, GLA_QK, GLA_QK, GLA_V, GLA_V, GLA_LOWRANK, GLA_LOWRANK, CONV_CH, CONV_CH])
            seg = lambda j: w_in[:, cuts[j]:cuts[j + 1]]
            lr_pad = jnp.zeros((d, LANES - 2 * GLA_LOWRANK), w_in.dtype)
            w_cat = jnp.concatenate([seg(0), seg(1), seg(2), seg(3), seg(6), seg(7), seg(4), seg(5), lr_pad],
                                    axis=1).astype(BF16)
            qk, v, g, glu, lr = _prenorm_proj(xt, mod, g_pre_mix[layer], w_cat,
                                              (2 * GLA_QK, GLA_V, GLA_V, 2 * CONV_CH, LANES), 0, 1, seq, batch)
            dwcat = jnp.zeros((LANES, 2 * GLA_QK), F32)
            dwcat = dwcat.at[:GLA_LOWRANK, :GLA_QK].set(gla_dw_f[i])
            dwcat = dwcat.at[GLA_LOWRANK:2 * GLA_LOWRANK, GLA_QK:].set(gla_dw_b[i]).astype(BF16)
            dbcat = jnp.concatenate([gla_db_f[i], gla_db_b[i]]).reshape(1, 2 * GLA_QK).astype(F32)
            a_out = _gla(qk, v, g, lr, dwcat, dbcat, gla_norm_g[i].reshape(1, GLA_V).astype(F32),
                         seq, ctx_len, batch)
            cw = conv_w[i].astype(F32)
            cvec = lambda t: t.reshape(1, CONV_CH).astype(F32)
            cv_x = _conv_module(glu, cw, cvec(conv_b[i]), cvec(conv_ln_g[i]), cvec(conv_ln_b[i]), seq, 0, batch)
            cv_c = _conv_module(glu, cw, cvec(conv_b[i]), cvec(conv_ln_g[i]), cvec(conv_ln_b[i]), ctx_len,
                                n_x // ctx_len, batch)
            cv = jnp.concatenate([cv_x, cv_c], axis=0)
            n_rows = n_x if last else n_all
            xt = _outproj([a_out, cv], ab_w_out[i].astype(BF16), xt, mod, g_post_mix[layer], 2, n_rows, seq, batch)
        else:
            (z,) = _prenorm_proj(xt, mod, g_pre_mix[layer], na_w_in[i].astype(BF16),
                                 (3 * NA_HEADS * NA_DH,), 0, 1, seq, batch)
            bias = _na_bias_table(na_rpb[i], seq // GRID_W)
            o = _na_attention(z, bias, seq, ctx_len, batch)
            if last:
                xt = _outproj([o], na_w_out[i].astype(BF16), xt, mod, g_post_mix[layer], 2, n_x, seq, batch)
            else:
                raise NotImplementedError("context update after a neighbourhood-attention layer")
        n_tok = n_x if last else n_all
        xt = _moe_layer(xt, mod, g_pre_ffn[layer], g_post_ffn[layer], moe_router_w[layer], moe_router_b[layer],
                        moe_w_gate, moe_w_up, moe_w_down, moe_ws_gate[layer],
                        moe_ws_up[layer], moe_ws_down[layer], n_tok, seq, batch, layer)
    return xt[:n_x].reshape(batch, seq, d)
```

```python
import functools

import numpy as np
import jax
import jax.numpy as jnp
from jax import lax
from jax.experimental import pallas as pl
from jax.experimental.pallas import tpu as pltpu

F32 = jnp.float32
BF16 = jnp.bfloat16

GRID_W = 64
RMS_EPS = 1e-6
LN_EPS = 1e-5
ADA_CHUNKS = 6
GLA_HEADS = 4
GLA_DK = 64
GLA_DV = 128
GLA_LOWRANK = 16
GLA_GATE_NORM = 16.0
GLA_CHUNK = 64
ROPE_BASE = 10000.0
GLA_QK = GLA_HEADS * GLA_DK
GLA_V = GLA_HEADS * GLA_DV
CONV_CH = 512
CONV_WIDTH = 31
NA_HEADS = 16
NA_DH = 64
NA_WIN_ROWS = 8
NA_WIN_COLS = 16
N_EXPERTS = 256
TOP_K = 8
N_GROUPS = 8
TOPK_GROUPS = 4
ROUTE_SCALE = 2.5

LANES = 128
ROW_TILE = 512
ADA_ROWS = 24
ADA_TN = 1536
CONV_RC = 32
CONV_HALO = 16
GLA_PAIR = 2
NA_ROW_UNROLL = 4
MOE_RB = 512
DISP_TT = 128
IDX_TILE = TOP_K * DISP_TT
VMEM_LIMIT = 56 * 1024 * 1024

_NT = (((1,), (1,)), ((), ()))
_TN = (((0,), (0,)), ((), ()))


def _params(sem, **kw):
    return pltpu.CompilerParams(dimension_semantics=sem, vmem_limit_bytes=VMEM_LIMIT, **kw)


def _sigmoid(x):
    return 1.0 / (1.0 + jnp.exp(-x))


def _split(a):
    hi = a.astype(BF16)
    lo = (a - hi.astype(F32)).astype(BF16)
    return hi, lo


def _dot3(a, b_hi, b_lo, dims=(((1,), (0,)), ((), ()))):
    a_hi, a_lo = _split(a)
    d = lambda x, y: lax.dot_general(x, y, dims, preferred_element_type=F32)
    return d(a_hi, b_hi) + (d(a_hi, b_lo) + d(a_lo, b_hi))


def _prenorm_mod(x, g, mod_ref, shift_i, scale_i):
    ms = jnp.mean(x * x, axis=-1, keepdims=True)
    h = x * lax.rsqrt(ms + RMS_EPS) * g
    return h * (1.0 + mod_ref[0, scale_i:scale_i + 1, :]) + mod_ref[0, shift_i:shift_i + 1, :]


def _mod_index(tile_rows, seq, batch):
    per_batch = seq // tile_rows
    return lambda i, *_: (jnp.minimum(i // per_batch, batch), 0, 0)


def _ada_kernel(c_ref, w_ref, b_ref, o_ref):
    c = c_ref[...]
    s = c * _sigmoid(c)
    w_hi, w_lo = _split(w_ref[...])
    o_ref[...] = _dot3(s, w_hi, w_lo) + b_ref[...]


def _ada_mod(cc, ada_w, ada_b):
    depth, d, n = ada_w.shape
    return pl.pallas_call(
        _ada_kernel,
        grid=(depth, n // ADA_TN),
        in_specs=[pl.BlockSpec((ADA_ROWS, d), lambda l, j: (0, 0)),
                  pl.BlockSpec((None, d, ADA_TN), lambda l, j: (l, 0, j)),
                  pl.BlockSpec((None, 1, ADA_TN), lambda l, j: (l, 0, j))],
        out_specs=pl.BlockSpec((None, ADA_ROWS, ADA_TN), lambda l, j: (l, 0, j)),
        out_shape=jax.ShapeDtypeStruct((depth, ADA_ROWS, n), F32),
        compiler_params=_params(("arbitrary", "arbitrary")),
        name="ada_mod",
    )(cc, ada_w, ada_b.reshape(depth, 1, n))


def _prenorm_proj_kernel(x_ref, mod_ref, g_ref, w_ref, *o_refs, shift_i, scale_i):
    h = _prenorm_mod(x_ref[...], g_ref[...], mod_ref, shift_i, scale_i).astype(BF16)
    off = 0
    for o_ref in o_refs:
        n = o_ref.shape[-1]
        o_ref[...] = jnp.dot(h, w_ref[:, off:off + n], preferred_element_type=F32).astype(o_ref.dtype)
        off += n


def _prenorm_proj(xt, mod, g, w, splits, shift_i, scale_i, seq, batch):
    rows, d = xt.shape
    n = w.shape[1]
    return pl.pallas_call(
        functools.partial(_prenorm_proj_kernel, shift_i=shift_i, scale_i=scale_i),
        grid=(rows // ROW_TILE,),
        in_specs=[pl.BlockSpec((ROW_TILE, d), lambda i: (i, 0)),
                  pl.BlockSpec((1, ADA_CHUNKS, d), _mod_index(ROW_TILE, seq, batch)),
                  pl.BlockSpec((1, d), lambda i: (0, 0)),
                  pl.BlockSpec((d, n), lambda i: (0, 0))],
        out_specs=[pl.BlockSpec((ROW_TILE, s), lambda i: (i, 0)) for s in splits],
        out_shape=[jax.ShapeDtypeStruct((rows, s), BF16) for s in splits],
        compiler_params=_params(("arbitrary",)),
        name="prenorm_proj",
    )(xt, mod, g.reshape(1, d), w)


def _gla_kernel(qkx_ref, vx_ref, gx_ref, lrx_ref, qkc_ref, vc_ref, gc_ref, lrc_ref,
                dw_ref, db_ref, cos_ref, sin_ref, gg_ref, ox_ref, oc_ref,
                sf_ref, sb_ref, accx_ref, accc_ref, *, n_x, n_c):
    L = GLA_CHUNK
    row = lax.broadcasted_iota(jnp.int32, (L, L), 0)
    col = lax.broadcasted_iota(jnp.int32, (L, L), 1)
    lower = row >= col
    upper = col >= row
    tri_f = jnp.where(lower, 1.0, 0.0).astype(BF16)
    tri_b = jnp.where(upper, 1.0, 0.0).astype(BF16)
    lane = lax.broadcasted_iota(jnp.int32, (1, GLA_QK), 1)
    head_of_lane = lane // GLA_DK
    first_half = (lane % (GLA_DK // 2)) < (GLA_DK // 4)
    bd_mask = (lax.broadcasted_iota(jnp.int32, (GLA_V, GLA_QK), 0) // GLA_DV
               == lax.broadcasted_iota(jnp.int32, (GLA_V, GLA_QK), 1) // GLA_DK)

    def swap_pairs(t):
        nf = GLA_DK // 4
        return jnp.where(first_half, pltpu.roll(t, GLA_QK - nf, 1), pltpu.roll(t, nf, 1))

    def chunk(fwd, c, qk_ref, v_ref, lr_ref, latent):
        rows = pl.ds(pl.multiple_of(c * L, L), L)
        q = qk_ref[rows, 0:GLA_QK].astype(F32) * (GLA_DK ** -0.5)
        k = qk_ref[rows, GLA_QK:2 * GLA_QK].astype(F32)
        if latent:
            cs = cos_ref[rows, :]
            sn = sin_ref[rows, :]
            q = q * cs + swap_pairs(q) * sn
            k = k * cs + swap_pairs(k) * sn
        dsl = slice(0, GLA_QK) if fwd else slice(GLA_QK, 2 * GLA_QK)
        pre = jnp.dot(lr_ref[rows, :], dw_ref[:, dsl], preferred_element_type=F32) + db_ref[:, dsl]
        yield
        logd = (jnp.minimum(pre, 0.0) - jnp.log(1.0 + jnp.exp(-jnp.abs(pre)))) * (1.0 / GLA_GATE_NORM)
        d_hi, d_lo = _split(logd)
        tri = tri_f if fwd else tri_b
        b = (jnp.dot(tri, d_hi, preferred_element_type=F32)
             + jnp.dot(tri, d_lo, preferred_element_type=F32))
        yield
        mid = L // 2 - 1 if fwd else L // 2
        end = L - 1 if fwd else 0
        b_mid = b[mid:mid + 1, :]
        b_end = b[end:end + 1, :]
        qe = q * jnp.exp(b - b_mid)
        ke = (k * jnp.exp(b_mid - b)).astype(BF16)
        q_in = (q * jnp.exp(b)).astype(BF16)
        k_out = (k * jnp.exp(b_end - b)).astype(BF16)
        dec = jnp.exp(b_end)
        vv = v_ref[rows, :]
        mask = lower if fwd else upper
        scores = []
        for h in range(GLA_HEADS):
            qm = jnp.where(head_of_lane == h, qe, 0.0).astype(BF16)
            scores.append(lax.dot_general(qm, ke, _NT, preferred_element_type=F32))
        yield
        outs = []
        for h in range(GLA_HEADS):
            a = jnp.where(mask, scores[h], 0.0).astype(BF16)
            outs.append(jnp.dot(a, vv[:, h * GLA_DV:(h + 1) * GLA_DV], preferred_element_type=F32))
        o_intra = jnp.concatenate(outs, axis=-1)
        yield

        def last():
            s_ref = sf_ref if fwd else sb_ref
            st = s_ref[...]
            o_inter = lax.dot_general(q_in, st.astype(BF16), _NT, preferred_element_type=F32)
            upd = lax.dot_general(vv, k_out, _TN, preferred_element_type=F32)
            s_ref[...] = jnp.where(bd_mask, st * dec + upd, 0.0)
            return o_intra + o_inter

        return last

    def run_chunks(specs):
        gens = [chunk(*sp) for sp in specs]
        lasts = [None] * len(gens)
        live = list(range(len(gens)))
        while live:
            for gi in list(live):
                try:
                    next(gens[gi])
                except StopIteration as stop:
                    lasts[gi] = stop.value
                    live.remove(gi)
        return [fn() for fn in lasts]

    def finish(o, g_ref, c):
        rows = pl.ds(pl.multiple_of(c * L, L), L)
        g = g_ref[rows, :].astype(F32)
        parts = []
        for h in range(GLA_HEADS):
            oh = o[:, h * GLA_DV:(h + 1) * GLA_DV]
            ms = jnp.mean(oh * oh, axis=-1, keepdims=True)
            parts.append(oh * lax.rsqrt(ms + RMS_EPS))
        on = jnp.concatenate(parts, axis=-1) * gg_ref[...]
        return (on * (g * _sigmoid(g))).astype(BF16)

    def run_seq(n, qk_ref, v_ref, g_ref, lr_ref, acc_ref, o_ref, latent):
        half = n // 2

        def rows_of(c):
            return pl.ds(pl.multiple_of(c * L, L), L)

        def chunks_of(i):
            cs = [(True, i * GLA_PAIR + u) for u in range(GLA_PAIR)]
            cs += [(False, n - 1 - (i * GLA_PAIR + u)) for u in range(GLA_PAIR)]
            return cs

        def first(i, carry):
            cs = chunks_of(i)
            res = run_chunks([(f, c, qk_ref, v_ref, lr_ref, latent) for f, c in cs])
            for (f, c), o in zip(cs, res):
                acc_ref[rows_of(c), :] = o
            return carry

        def second(i, carry):
            cs = chunks_of(i)
            res = run_chunks([(f, c, qk_ref, v_ref, lr_ref, latent) for f, c in cs])
            for (f, c), o in zip(cs, res):
                o_ref[rows_of(c), :] = finish(acc_ref[rows_of(c), :] + o, g_ref, c)
            return carry

        assert half % GLA_PAIR == 0
        lax.fori_loop(0, half // GLA_PAIR, first, 0)
        lax.fori_loop(half // GLA_PAIR, n // GLA_PAIR, second, 0)

    sf_ref[...] = jnp.zeros_like(sf_ref)
    sb_ref[...] = jnp.zeros_like(sb_ref)
    run_seq(n_c, qkc_ref, vc_ref, gc_ref, lrc_ref, accc_ref, oc_ref, False)
    run_seq(n_x, qkx_ref, vx_ref, gx_ref, lrx_ref, accx_ref, ox_ref, True)


def _rope_tables(seq):
    half = GLA_DK // 2
    nf = half // 2
    inv = ROPE_BASE ** (-jnp.arange(nf, dtype=F32) / nf)
    t = jnp.arange(seq)
    rows = (t // GRID_W).astype(F32)
    cols = (t % GRID_W).astype(F32)
    ar = rows[:, None] * inv[None, :]
    ac = cols[:, None] * inv[None, :]
    cos_h = jnp.concatenate([jnp.cos(ar), jnp.cos(ar), jnp.cos(ac), jnp.cos(ac)], axis=-1)
    sin_h = jnp.concatenate([-jnp.sin(ar), jnp.sin(ar), -jnp.sin(ac), jnp.sin(ac)], axis=-1)
    return jnp.tile(cos_h, (1, GLA_HEADS)), jnp.tile(sin_h, (1, GLA_HEADS))


def _gla(qk, v, g, lr, dwcat, dbcat, gla_g, seq, ctx_len, batch):
    rows = qk.shape[0]
    cos_t, sin_t = _rope_tables(seq)
    cb = (batch * seq) // ctx_len
    xs = lambda w: pl.BlockSpec((seq, w), lambda b: (b, 0))
    cs = lambda w: pl.BlockSpec((ctx_len, w), lambda b: (cb + b, 0))
    full = lambda a: pl.BlockSpec(a.shape, lambda b: (0,) * a.ndim)
    out_x, out_c = pl.pallas_call(
        functools.partial(_gla_kernel, n_x=seq // GLA_CHUNK, n_c=ctx_len // GLA_CHUNK),
        grid=(batch,),
        in_specs=[xs(2 * GLA_QK), xs(GLA_V), xs(GLA_V), xs(LANES),
                  cs(2 * GLA_QK), cs(GLA_V), cs(GLA_V), cs(LANES),
                  full(dwcat), full(dbcat), full(cos_t), full(sin_t), full(gla_g)],
        out_specs=[pl.BlockSpec((seq, GLA_V), lambda b: (b, 0)),
                   pl.BlockSpec((ctx_len, GLA_V), lambda b: (b, 0))],
        out_shape=[jax.ShapeDtypeStruct((batch * seq, GLA_V), BF16),
                   jax.ShapeDtypeStruct((batch * ctx_len, GLA_V), BF16)],
        scratch_shapes=[pltpu.VMEM((GLA_V, GLA_QK), F32), pltpu.VMEM((GLA_V, GLA_QK), F32),
                        pltpu.VMEM((seq, GLA_V), F32), pltpu.VMEM((ctx_len, GLA_V), F32)],
        compiler_params=_params(("arbitrary",)),
        name="gla",
    )(qk, v, g, lr, qk, v, g, lr, dwcat, dbcat, cos_t, sin_t, gla_g)
    del rows
    return jnp.concatenate([out_x, out_c], axis=0)


def _conv_kernel(glu_ref, w_ref, cb_ref, lg_ref, lb_ref, o_ref, pad_ref, win_ref, *, t_len):
    ch = CONV_CH
    zeros = jnp.zeros((CONV_HALO, ch), F32)
    pad_ref[0:CONV_HALO, :] = zeros
    pad_ref[CONV_HALO + t_len:2 * CONV_HALO + t_len, :] = zeros
    fill = 128

    def fill_body(i, carry):
        r = pl.multiple_of(i * fill, fill)
        a = glu_ref[pl.ds(r, fill), 0:ch].astype(F32)
        b = glu_ref[pl.ds(r, fill), ch:2 * ch].astype(F32)
        pad_ref[pl.ds(CONV_HALO + r, fill), :] = a * _sigmoid(b)
        return carry

    lax.fori_loop(0, t_len // fill, fill_body, 0)
    first_tap = CONV_HALO - CONV_WIDTH // 2

    def body(i, carry):
        base = pl.multiple_of(i * CONV_RC, CONV_RC)
        win_ref[...] = pad_ref[pl.ds(base, CONV_RC + 2 * CONV_HALO), :]
        acc = jnp.zeros((CONV_RC, ch), F32)
        for j in range(CONV_WIDTH):
            acc = acc + w_ref[j:j + 1, :] * win_ref[first_tap + j:first_tap + j + CONV_RC, :]
        acc = acc + cb_ref[...]
        mu = jnp.mean(acc, axis=-1, keepdims=True)
        cen = acc - mu
        var = jnp.mean(cen * cen, axis=-1, keepdims=True)
        y = cen * lax.rsqrt(var + LN_EPS) * lg_ref[...] + lb_ref[...]
        o_ref[pl.ds(base, CONV_RC), :] = (y * _sigmoid(y)).astype(BF16)
        return carry

    lax.fori_loop(0, t_len // CONV_RC, body, 0)


def _conv_module(glu, conv_w, conv_b, ln_g, ln_b, t_len, first_block, batch):
    full = lambda a: pl.BlockSpec(a.shape, lambda b: (0,) * a.ndim)
    return pl.pallas_call(
        functools.partial(_conv_kernel, t_len=t_len),
        grid=(batch,),
        in_specs=[pl.BlockSpec((t_len, 2 * CONV_CH), lambda b: (first_block + b, 0)),
                  full(conv_w), full(conv_b), full(ln_g), full(ln_b)],
        out_specs=pl.BlockSpec((t_len, CONV_CH), lambda b: (b, 0)),
        out_shape=jax.ShapeDtypeStruct((batch * t_len, CONV_CH), BF16),
        scratch_shapes=[pltpu.VMEM((t_len + 2 * CONV_HALO, CONV_CH), F32),
                        pltpu.VMEM((CONV_RC + 2 * CONV_HALO, CONV_CH), F32)],
        compiler_params=_params(("arbitrary",)),
        name="conv_module",
    )(glu, conv_w, conv_b, ln_g, ln_b)


def _outproj_kernel(*refs, n_lhs, gate_i):
    lhs_refs = refs[:n_lhs]
    w_ref, x_ref, mod_ref, g_ref, o_ref = refs[n_lhs:]
    off = 0
    y = None
    for l_ref in lhs_refs:
        kk = l_ref.shape[-1]
        t = jnp.dot(l_ref[...], w_ref[off:off + kk, :], preferred_element_type=F32)
        y = t if y is None else y + t
        off += kk
    ms = jnp.mean(y * y, axis=-1, keepdims=True)
    yn = y * lax.rsqrt(ms + RMS_EPS) * g_ref[...]
    o_ref[...] = x_ref[...] + mod_ref[0, gate_i:gate_i + 1, :] * yn


def _outproj(lhs_list, w, xt, mod, g, gate_i, n_rows, seq, batch):
    d = xt.shape[1]
    n_lhs = len(lhs_list)
    in_specs = [pl.BlockSpec((ROW_TILE, a.shape[1]), lambda i: (i, 0)) for a in lhs_list]
    in_specs += [pl.BlockSpec(w.shape, lambda i: (0, 0)),
                 pl.BlockSpec((ROW_TILE, d), lambda i: (i, 0)),
                 pl.BlockSpec((1, ADA_CHUNKS, d), _mod_index(ROW_TILE, seq, batch)),
                 pl.BlockSpec((1, d), lambda i: (0, 0))]
    return pl.pallas_call(
        functools.partial(_outproj_kernel, n_lhs=n_lhs, gate_i=gate_i),
        grid=(n_rows // ROW_TILE,),
        in_specs=in_specs,
        out_specs=pl.BlockSpec((ROW_TILE, d), lambda i: (i, 0)),
        out_shape=jax.ShapeDtypeStruct(xt.shape, F32),
        input_output_aliases={n_lhs + 1: 0},
        compiler_params=_params(("arbitrary",)),
        name="outproj",
    )(*lhs_list, w, xt, mod, g.reshape(1, d))


def _na_kernel(q_ref, k_ref, v_ref, kc_ref, vc_ref, bias_ref, o_ref, *, n_rows):
    wr = NA_WIN_ROWS
    lane = lax.broadcasted_iota(jnp.int32, (1, LANES), 1)
    head0 = lane < NA_DH
    kc = kc_ref[...]
    vc = vc_ref[...]

    def body(it, carry):
        rs = [it * NA_ROW_UNROLL + u for u in range(NA_ROW_UNROLL)]
        r0s = [jnp.clip(r - wr // 2, 0, n_rows - wr) for r in rs]
        win = [pl.ds(pl.multiple_of(r0 * GRID_W, GRID_W), wr * GRID_W) for r0 in r0s]
        scores = []
        for r, r0, w in zip(rs, r0s, win):
            q2 = q_ref[pl.ds(pl.multiple_of(r * GRID_W, GRID_W), GRID_W), :] * (NA_DH ** -0.5)
            zero = jnp.zeros_like(q2)
            qs = jnp.concatenate([jnp.where(head0, q2, zero), jnp.where(head0, zero, q2)], axis=0)
            bias = bias_ref[:, r - r0].reshape(2 * GRID_W, wr * GRID_W)
            s_w = lax.dot_general(qs, k_ref[w, :], _NT, preferred_element_type=F32) + bias
            s_c = lax.dot_general(qs, kc, _NT, preferred_element_type=F32)
            scores.append((s_w, s_c))
        probs = []
        for s_w, s_c in scores:
            m = jnp.maximum(jnp.max(s_w, axis=-1, keepdims=True), jnp.max(s_c, axis=-1, keepdims=True))
            p_w = jnp.exp(s_w - m)
            p_c = jnp.exp(s_c - m)
            den = jnp.sum(p_w, axis=-1, keepdims=True) + jnp.sum(p_c, axis=-1, keepdims=True)
            probs.append((p_w.astype(BF16), p_c.astype(BF16), den))
        for r, w, (p_w, p_c, den) in zip(rs, win, probs):
            o = (jnp.dot(p_w, v_ref[w, :], preferred_element_type=F32)
                 + jnp.dot(p_c, vc, preferred_element_type=F32)) / den
            o_ref[pl.ds(pl.multiple_of(r * GRID_W, GRID_W), GRID_W), :] = (
                jnp.where(head0, o[:GRID_W], o[GRID_W:]).astype(BF16))
        return carry

    assert n_rows % NA_ROW_UNROLL == 0
    lax.fori_loop(0, n_rows // NA_ROW_UNROLL, body, 0)


def _na_bias_table(rpb, n_rows):
    wr = min(NA_WIN_ROWS, n_rows)
    qc = np.arange(GRID_W)
    cstart = np.clip(qc - NA_WIN_COLS // 2, 0, GRID_W - NA_WIN_COLS)
    kcol = np.arange(GRID_W)
    ok = (kcol[None, :] >= cstart[:, None]) & (kcol[None, :] < cstart[:, None] + NA_WIN_COLS)
    cidx = np.clip(kcol[None, :] - qc[:, None] + NA_WIN_COLS - 1, 0, 2 * NA_WIN_COLS - 2)
    delta = np.arange(wr)
    j = np.arange(wr)
    ridx = j[None, :] - delta[:, None] + NA_WIN_ROWS - 1
    t = rpb[:, ridx][:, :, :, cidx]
    t = jnp.where(jnp.asarray(ok)[None, None, None], t.astype(F32), -jnp.inf)
    t = t.transpose(0, 1, 3, 2, 4)
    return t.reshape(rpb.shape[0], wr, GRID_W, wr * GRID_W)


def _na_attention(z, bias, seq, ctx_len, batch):
    n_pairs = NA_HEADS * NA_DH // LANES
    n_rows = seq // GRID_W
    cb = (batch * seq) // ctx_len
    return pl.pallas_call(
        functools.partial(_na_kernel, n_rows=n_rows),
        grid=(n_pairs, batch),
        in_specs=[pl.BlockSpec((seq, LANES), lambda p, b: (b, p)),
                  pl.BlockSpec((seq, LANES), lambda p, b: (b, n_pairs + p)),
                  pl.BlockSpec((seq, LANES), lambda p, b: (b, 2 * n_pairs + p)),
                  pl.BlockSpec((ctx_len, LANES), lambda p, b: (cb + b, n_pairs + p)),
                  pl.BlockSpec((ctx_len, LANES), lambda p, b: (cb + b, 2 * n_pairs + p)),
                  pl.BlockSpec((2,) + bias.shape[1:], lambda p, b: (p, 0, 0, 0))],
        out_specs=pl.BlockSpec((seq, LANES), lambda p, b: (b, p)),
        out_shape=jax.ShapeDtypeStruct((batch * seq, NA_HEADS * NA_DH), BF16),
        compiler_params=_params(("arbitrary", "arbitrary")),
        name="na_attention",
    )(z, z, z, z, z, bias)


def _store_token_tiles(ref, val, row0=0):
    n, d = val.shape
    nc = d // LANES
    for k in range(nc):
        ref[pl.ds(row0 * nc + k, n, stride=nc), :] = val[:, k * LANES:(k + 1) * LANES]


def _load_token_tiles(ref, row0, n, nc, lead=None):
    chunks = []
    for k in range(nc):
        idx = pl.ds(row0 * nc + k, n, stride=nc)
        chunks.append(ref[idx, :] if lead is None else ref[lead, idx, :])
    return jnp.concatenate(chunks, axis=-1)


def _first_argmax(x, iota_f, n):
    m = jnp.max(x, axis=0, keepdims=True)
    idx = jnp.min(jnp.where(x == m, iota_f, float(n)), axis=0, keepdims=True)
    return m, idx


def _moe_pre_kernel(x_ref, mod_ref, g_ref, rwh_ref, rwl_ref, rb_ref, wsg_ref, wsu_ref, wsd_ref,
                    h_ref, y0_ref, eidx_ref, gates_ref, rank_ref, cnt_ref, base_ref):
    tm = x_ref.shape[0]
    n_e = rwh_ref.shape[0]

    @pl.when(pl.program_id(0) == 0)
    def _():
        base_ref[...] = jnp.zeros_like(base_ref)

    h = _prenorm_mod(x_ref[...], g_ref[...], mod_ref, 3, 4)
    _store_token_tiles(h_ref, h)
    hb = h.astype(BF16)
    gate = jnp.dot(hb, wsg_ref[...], preferred_element_type=F32)
    up = jnp.dot(hb, wsu_ref[...], preferred_element_type=F32)
    hid = (gate * _sigmoid(gate) * up).astype(BF16)
    y0_ref[...] = jnp.dot(hid, wsd_ref[...], preferred_element_type=F32)

    h_lo = (h - hb.astype(F32)).astype(BF16)
    d = lambda a, b: lax.dot_general(a, b, _NT, preferred_element_type=F32)
    logits = d(rwh_ref[...], hb) + (d(rwh_ref[...], h_lo) + d(rwl_ref[...], hb))
    scores = _sigmoid(logits)
    biased = scores + rb_ref[...]
    neg = -jnp.inf
    gsz = n_e // N_GROUPS
    io_g = lax.broadcasted_iota(jnp.int32, (gsz, tm), 0).astype(F32)
    grp_rows = []
    for gi in range(N_GROUPS):
        xg = biased[gi * gsz:(gi + 1) * gsz]
        m1, i1 = _first_argmax(xg, io_g, gsz)
        m2 = jnp.max(jnp.where(io_g == i1, neg, xg), axis=0, keepdims=True)
        grp_rows.append(m1 + m2)
    cur = jnp.concatenate(grp_rows, axis=0)
    io_n = lax.broadcasted_iota(jnp.int32, (N_GROUPS, tm), 0).astype(F32)
    keep = jnp.zeros((N_GROUPS, tm), F32)
    for _ in range(TOPK_GROUPS):
        _, ii = _first_argmax(cur, io_n, N_GROUPS)
        sel = io_n == ii
        keep = jnp.where(sel, 1.0, keep)
        cur = jnp.where(sel, neg, cur)
    cur = jnp.concatenate([jnp.where(keep[gi:gi + 1] > 0.0, biased[gi * gsz:(gi + 1) * gsz], neg)
                           for gi in range(N_GROUPS)], axis=0)
    io_e = lax.broadcasted_iota(jnp.int32, (n_e, tm), 0).astype(F32)
    idxs, gvals = [], []
    chosen = jnp.zeros((n_e, tm), F32)
    for _ in range(TOP_K):
        _, ii = _first_argmax(cur, io_e, n_e)
        sel = io_e == ii
        idxs.append(ii)
        gvals.append(jnp.sum(jnp.where(sel, scores, 0.0), axis=0, keepdims=True))
        chosen = jnp.where(sel, 1.0, chosen)
        cur = jnp.where(sel, neg, cur)
    gv = jnp.concatenate(gvals, axis=0)
    gates_ref[...] = gv / jnp.sum(gv, axis=0, keepdims=True) * ROUTE_SCALE
    eidx_ref[...] = jnp.concatenate(idxs, axis=0).astype(jnp.int32)

    earlier = (lax.broadcasted_iota(jnp.int32, (tm, tm), 0) < lax.broadcasted_iota(jnp.int32, (tm, tm), 1))
    before = jnp.dot(chosen.astype(BF16), jnp.where(earlier, 1.0, 0.0).astype(BF16), preferred_element_type=F32)
    before = before + base_ref[:, 0:1]
    ranks = [jnp.sum(jnp.where(io_e == ii, before, 0.0), axis=0, keepdims=True) for ii in idxs]
    rank_ref[...] = jnp.concatenate(ranks, axis=0).astype(jnp.int32)
    base_ref[...] = base_ref[...] + jnp.sum(chosen, axis=1, keepdims=True)
    cnt_ref[...] = base_ref[...].astype(jnp.int32)


def _moe_pre(xt, mod, g, rwt_hi, rwt_lo, rb, wsg, wsu, wsd, n_tok, seq, batch):
    d = xt.shape[1]
    e = rwt_hi.shape[0]
    full = lambda a: pl.BlockSpec(a.shape, lambda i: (0,) * a.ndim)
    tok_major = lambda: pl.BlockSpec((TOP_K, ROW_TILE), lambda i: (0, i))
    return pl.pallas_call(
        _moe_pre_kernel,
        grid=(n_tok // ROW_TILE,),
        in_specs=[pl.BlockSpec((ROW_TILE, d), lambda i: (i, 0)),
                  pl.BlockSpec((1, ADA_CHUNKS, d), _mod_index(ROW_TILE, seq, batch)),
                  pl.BlockSpec((1, d), lambda i: (0, 0)),
                  full(rwt_hi), full(rwt_lo), full(rb), full(wsg), full(wsu), full(wsd)],
        out_specs=[pl.BlockSpec((ROW_TILE * (d // LANES), LANES), lambda i: (i, 0)),
                   pl.BlockSpec((ROW_TILE, d), lambda i: (i, 0)),
                   tok_major(), tok_major(), tok_major(),
                   pl.BlockSpec((e, LANES), lambda i: (0, 0))],
        out_shape=[jax.ShapeDtypeStruct((n_tok * (d // LANES), LANES), F32),
                   jax.ShapeDtypeStruct((n_tok, d), F32),
                   jax.ShapeDtypeStruct((TOP_K, n_tok), jnp.int32),
                   jax.ShapeDtypeStruct((TOP_K, n_tok), F32),
                   jax.ShapeDtypeStruct((TOP_K, n_tok), jnp.int32),
                   jax.ShapeDtypeStruct((e, LANES), jnp.int32)],
        scratch_shapes=[pltpu.VMEM((e, LANES), F32)],
        compiler_params=_params(("arbitrary",)),
        name="moe_pre",
    )(xt, mod, g.reshape(1, d), rwt_hi, rwt_lo, rb, wsg, wsu, wsd)


def _moe_pos_kernel(eidx_ref, rank_ref, pstart_ref, pos_ref):
    n_e = pstart_ref.shape[0]
    tp = eidx_ref.shape[1]
    io_e = lax.broadcasted_iota(jnp.int32, (n_e, tp), 0)
    ps = pstart_ref[...]
    e = eidx_ref[...]
    rows = [jnp.sum(jnp.where(io_e == e[k:k + 1, :], ps, 0.0), axis=0, keepdims=True) for k in range(TOP_K)]
    pos_ref[...] = rank_ref[...] + jnp.concatenate(rows, axis=0).astype(jnp.int32)


def _moe_pos(eidx, rank, pstart):
    n_tok = eidx.shape[1]
    blk = lambda: pl.BlockSpec((TOP_K, ROW_TILE), lambda i: (0, i))
    return pl.pallas_call(
        _moe_pos_kernel,
        grid=(n_tok // ROW_TILE,),
        in_specs=[blk(), blk(), pl.BlockSpec(pstart.shape, lambda i: (0, 0))],
        out_specs=blk(),
        out_shape=jax.ShapeDtypeStruct(eidx.shape, jnp.int32),
        compiler_params=_params(("arbitrary",)),
        name="moe_pos",
    )(eidx, rank, pstart)


def _moe_dispatch_kernel(lastblk_ref, nused_ref, pos_hbm, h_hbm, xg_hbm, idx_smem, zbuf, hbuf, idx_sem, row_sem, z_sem,
                         h_sem, *, n_tiles, nc, n_blocks):
    i = pl.program_id(0)
    slot = i % 2
    n_e = lastblk_ref.shape[0]
    rows_per_step = DISP_TT * TOP_K

    def h_copy(t, s):
        return pltpu.make_async_copy(h_hbm.at[pl.ds(pl.multiple_of(t * (DISP_TT * nc), DISP_TT * nc), DISP_TT * nc), :],
                                     hbuf.at[s], h_sem.at[s])

    def idx_copy(t, s):
        return pltpu.make_async_copy(pos_hbm.at[pl.ds(pl.multiple_of(t * IDX_TILE, IDX_TILE), IDX_TILE)],
                                     idx_smem.at[pl.ds(pl.multiple_of(s * IDX_TILE, IDX_TILE), IDX_TILE)],
                                     idx_sem.at[s])

    def block_copy(b):
        return pltpu.make_async_copy(zbuf, xg_hbm.at[pl.ds(pl.multiple_of(b * (MOE_RB * nc), MOE_RB * nc),
                                                           MOE_RB * nc), :], z_sem)

    def wait_rows(s):
        pltpu.make_async_copy(h_hbm.at[pl.ds(0, rows_per_step * nc), :], xg_hbm.at[pl.ds(0, rows_per_step * nc), :],
                              row_sem.at[s]).wait()

    @pl.when(i == 0)
    def _():
        idx_copy(0, 0).start()
        zbuf[...] = jnp.zeros_like(zbuf)

        def for_zero_blocks(fn):
            def per_expert(e, carry):
                @pl.when(lastblk_ref[e] >= 0)
                def _():
                    fn(block_copy(lastblk_ref[e]))
                return carry

            def per_block(b, carry):
                fn(block_copy(b))
                return carry

            lax.fori_loop(0, n_e, per_expert, 0)
            lax.fori_loop(nused_ref[0], n_blocks, per_block, 0)

        for_zero_blocks(lambda cp: cp.start())
        for_zero_blocks(lambda cp: cp.wait())

        h_copy(0, 0).start()

    idx_copy(i, slot).wait()
    hslot = i % 3
    h_copy(i, hslot).wait()

    @pl.when(i + 1 < n_tiles)
    def _():
        idx_copy(i + 1, 1 - slot).start()
        h_copy(i + 1, (i + 1) % 3).start()

    def body(r, carry):
        src = hbuf.at[hslot, pl.ds(pl.multiple_of(r * nc, nc), nc), :]
        base = slot * IDX_TILE + r
        for k in range(TOP_K):
            p = idx_smem[base + k * DISP_TT]
            pltpu.make_async_copy(src, xg_hbm.at[pl.ds(pl.multiple_of(p * nc, nc), nc), :],
                                  row_sem.at[slot]).start(priority=k % 2)
        return carry

    lax.fori_loop(0, DISP_TT, body, 0, unroll=2)

    @pl.when(i > 0)
    def _():
        wait_rows(1 - slot)

    @pl.when(i == n_tiles - 1)
    def _():
        wait_rows(slot)


def _moe_dispatch(lastblk, nused, pos, h, n_blocks):
    n_tok = pos.shape[0] // TOP_K
    nc = h.shape[0] // n_tok
    n_tiles = n_tok // DISP_TT
    grid_spec = pltpu.PrefetchScalarGridSpec(
        num_scalar_prefetch=2,
        grid=(n_tiles,),
        in_specs=[pl.BlockSpec(memory_space=pl.ANY), pl.BlockSpec(memory_space=pl.ANY)],
        out_specs=pl.BlockSpec(memory_space=pl.ANY),
        scratch_shapes=[pltpu.SMEM((2 * IDX_TILE,), jnp.int32),
                        pltpu.VMEM((MOE_RB * nc, LANES), F32),
                        pltpu.VMEM((3, DISP_TT * nc, LANES), F32),
                        pltpu.SemaphoreType.DMA((2,)),
                        pltpu.SemaphoreType.DMA((2,)),
                        pltpu.SemaphoreType.DMA(()),
                        pltpu.SemaphoreType.DMA((3,))],
    )
    return pl.pallas_call(
        functools.partial(_moe_dispatch_kernel, n_tiles=n_tiles, nc=nc, n_blocks=n_blocks),
        grid_spec=grid_spec,
        out_shape=jax.ShapeDtypeStruct((n_blocks * MOE_RB * nc, LANES), F32),
        compiler_params=_params(("arbitrary",)),
        name="moe_dispatch",
    )(lastblk, nused, pos, h)


def _moe_expert_kernel(be_ref, nused_ref, x_ref, wg_ref, wu_ref, wd_ref, o_ref, wgb, wub, wdb):
    i = pl.program_id(0)
    nused = nused_ref[0]
    nc = x_ref.shape[0] // MOE_RB
    new_expert = jnp.logical_or(i == 0, be_ref[i] != be_ref[jnp.maximum(i - 1, 0)])

    @pl.when(jnp.logical_and(i < nused, new_expert))
    def _():
        wgb[...] = wg_ref[...].astype(BF16)
        wub[...] = wu_ref[...].astype(BF16)
        wdb[...] = wd_ref[...].astype(BF16)

    @pl.when(i < nused)
    def _():
        xb = _load_token_tiles(x_ref, 0, MOE_RB, nc).astype(BF16)
        gate = jnp.dot(xb, wgb[...], preferred_element_type=F32)
        up = jnp.dot(xb, wub[...], preferred_element_type=F32)
        hid = (gate * _sigmoid(gate) * up).astype(BF16)
        _store_token_tiles(o_ref, jnp.dot(hid, wdb[...], preferred_element_type=F32))

    @pl.when(i >= nused)
    def _():
        o_ref[...] = jnp.zeros_like(o_ref)


def _moe_experts(block_e, nused, xg, wg, wu, wd, layer):
    nb = block_e.shape[0]
    d = wg.shape[2]
    nc = d // LANES
    de = wg.shape[3]
    grid_spec = pltpu.PrefetchScalarGridSpec(
        num_scalar_prefetch=2,
        grid=(nb,),
        in_specs=[pl.BlockSpec((MOE_RB * nc, LANES), lambda i, be, nu: (jnp.minimum(i, nu[0] - 1), 0)),
                  pl.BlockSpec((None, None, d, de), lambda i, be, nu: (layer, be[i], 0, 0)),
                  pl.BlockSpec((None, None, d, de), lambda i, be, nu: (layer, be[i], 0, 0)),
                  pl.BlockSpec((None, None, de, d), lambda i, be, nu: (layer, be[i], 0, 0))],
        out_specs=pl.BlockSpec((MOE_RB * nc, LANES), lambda i, be, nu: (i, 0)),
        scratch_shapes=[pltpu.VMEM((d, de), BF16), pltpu.VMEM((d, de), BF16), pltpu.VMEM((de, d), BF16)],
    )
    return pl.pallas_call(
        _moe_expert_kernel,
        grid_spec=grid_spec,
        out_shape=jax.ShapeDtypeStruct((nb * MOE_RB * nc, LANES), F32),
        compiler_params=_params(("arbitrary",)),
        name="moe_experts",
    )(block_e, nused, xg, wg, wu, wd)


def _moe_combine_kernel(pos_hbm, og_hbm, gates_ref, y0_ref, x_ref, mod_ref, g_ref, o_ref,
                        idx_smem, buf, idx_sem, row_sem, *, n_tiles):
    i = pl.program_id(0)
    slot = i % 2
    nxt = 1 - slot
    n_rows = TOP_K * DISP_TT
    nc = buf.shape[1] // n_rows

    def idx_copy(t, s):
        return pltpu.make_async_copy(pos_hbm.at[pl.ds(pl.multiple_of(t * IDX_TILE, IDX_TILE), IDX_TILE)],
                                     idx_smem.at[pl.ds(pl.multiple_of(s * IDX_TILE, IDX_TILE), IDX_TILE)],
                                     idx_sem.at[s])

    def issue_rows(s):
        def body(r, carry):
            base = s * IDX_TILE + r
            for k in range(TOP_K):
                p = idx_smem[base + k * DISP_TT]
                pltpu.make_async_copy(og_hbm.at[pl.ds(pl.multiple_of(p * nc, nc), nc), :],
                                      buf.at[s, pl.ds(pl.multiple_of((k * DISP_TT + r) * nc, nc), nc), :],
                                      row_sem.at[s]).start(priority=k % 2)
            return carry
        lax.fori_loop(0, DISP_TT, body, 0, unroll=2)

    def wait_rows(s):
        pltpu.make_async_copy(og_hbm.at[pl.ds(0, n_rows * nc), :], buf.at[s], row_sem.at[s]).wait()

    @pl.when(i == 0)
    def _():
        idx_copy(0, 0).start()
        idx_copy(0, 0).wait()
        issue_rows(0)

        @pl.when(n_tiles > 1)
        def _():
            idx_copy(1, 1).start()

    @pl.when(i + 1 < n_tiles)
    def _():
        idx_copy(i + 1, nxt).wait()
        issue_rows(nxt)

    wait_rows(slot)

    @pl.when(i + 2 < n_tiles)
    def _():
        idx_copy(i + 2, slot).start()

    gates = gates_ref[...].T
    y = y0_ref[...]
    for k in range(TOP_K):
        y = y + gates[:, k:k + 1] * _load_token_tiles(buf, k * DISP_TT, DISP_TT, nc, lead=slot)
    ms = jnp.mean(y * y, axis=-1, keepdims=True)
    yn = y * lax.rsqrt(ms + RMS_EPS) * g_ref[...]
    o_ref[...] = x_ref[...] + mod_ref[0, 5:6, :] * yn


def _moe_combine(pos, og, gates, y0, xt, mod, g, n_tok, seq, batch):
    d = xt.shape[1]
    n_tiles = n_tok // DISP_TT
    row_spec = lambda w: pl.BlockSpec((DISP_TT, w), lambda i: (i, 0))
    return pl.pallas_call(
        functools.partial(_moe_combine_kernel, n_tiles=n_tiles),
        grid=(n_tiles,),
        in_specs=[pl.BlockSpec(memory_space=pl.ANY),
                  pl.BlockSpec(memory_space=pl.ANY),
                  pl.BlockSpec((TOP_K, DISP_TT), lambda i: (0, i)), row_spec(d), row_spec(d),
                  pl.BlockSpec((1, ADA_CHUNKS, d), _mod_index(DISP_TT, seq, batch)),
                  pl.BlockSpec((1, d), lambda i: (0, 0))],
        out_specs=row_spec(d),
        out_shape=jax.ShapeDtypeStruct(xt.shape, F32),
        input_output_aliases={4: 0},
        scratch_shapes=[pltpu.SMEM((2 * IDX_TILE,), jnp.int32),
                        pltpu.VMEM((2, TOP_K * DISP_TT * (d // LANES), LANES), F32),
                        pltpu.SemaphoreType.DMA((2,)),
                        pltpu.SemaphoreType.DMA((2,))],
        compiler_params=_params(("arbitrary",)),
        name="moe_combine",
    )(pos, og, gates, y0, xt, mod, g.reshape(1, d))


def _dispatch_plan(counts, n_assign):
    e = counts.shape[0]
    padded = (counts + MOE_RB - 1) // MOE_RB * MOE_RB
    pend = jnp.cumsum(padded)
    pstart = pend - padded
    nb = -(-(n_assign + e * (MOE_RB - 1)) // MOE_RB)
    first_row = jnp.arange(nb, dtype=jnp.int32) * MOE_RB
    block_e = jnp.minimum(jnp.sum((pend[None, :] <= first_row[:, None]).astype(jnp.int32), axis=1), e - 1)
    nused = (pend[-1] // MOE_RB).astype(jnp.int32).reshape(1)
    lastblk = jnp.where(counts > 0, pend // MOE_RB - 1, -1)
    return pstart.astype(jnp.int32), lastblk.astype(jnp.int32), block_e.astype(jnp.int32), nused, nb


def _moe_layer(xt, mod, g_pre, g_post, router_w, router_b, w_gate, w_up, w_down, ws_gate, ws_up, ws_down,
               n_tok, seq, batch, layer):
    e = router_w.shape[1]
    rwt = router_w.astype(F32).T
    rwt_hi, rwt_lo = _split(rwt)
    h, y0, eidx, gates, rank, cnt = _moe_pre(xt, mod, g_pre, rwt_hi, rwt_lo, router_b.astype(F32).reshape(e, 1),
                                             ws_gate.astype(BF16), ws_up.astype(BF16), ws_down.astype(BF16),
                                             n_tok, seq, batch)
    pstart, lastblk, block_e, nused, nb = _dispatch_plan(cnt[:, 0], n_tok * TOP_K)
    pos = _moe_pos(eidx, rank, pstart.astype(F32).reshape(e, 1))
    pos = pos.reshape(TOP_K, n_tok // DISP_TT, DISP_TT).transpose(1, 0, 2).reshape(-1)
    xg = _moe_dispatch(lastblk, nused, pos, h, nb)
    og = _moe_experts(block_e, nused, xg, w_gate, w_up, w_down, layer)
    return _moe_combine(pos, og, gates, y0, xt, mod, g_post, n_tok, seq, batch)


def kernel(x, c, ctx, c_ctx, ada_w, ada_b, g_pre_mix, g_post_mix, g_pre_ffn, g_post_ffn, ab_w_in, ab_w_out, gla_dw_f, gla_db_f, gla_dw_b, gla_db_b, gla_norm_g, conv_w, conv_b, conv_ln_g, conv_ln_b, na_w_in, na_w_out, na_rpb, moe_router_w, moe_router_b, moe_w_gate, moe_w_up, moe_w_down, moe_ws_gate, moe_ws_up, moe_ws_down):
    batch, seq, d = x.shape
    ctx_len = ctx.shape[1]
    depth = ada_w.shape[0]
    n_x = batch * seq
    n_all = n_x + batch * ctx_len
    assert seq % ROW_TILE == 0 and (batch * ctx_len) % ROW_TILE == 0 and n_x % ctx_len == 0

    xt = jnp.concatenate([x.reshape(n_x, d), ctx.reshape(batch * ctx_len, d)], axis=0)

    cc = jnp.zeros((ADA_ROWS, d), F32).at[:batch].set(c).at[batch].set(c_ctx)
    mods = _ada_mod(cc, ada_w, ada_b)[:, :batch + 1].reshape(depth, batch + 1, ADA_CHUNKS, d)

    for layer in range(depth):
        last = layer == depth - 1
        i = layer // 2
        mod = mods[layer]
        if layer % 2 == 0:
            w_in = ab_w_in[i]
            cuts = np.cumsum([0, GLA_QK, GLA_QK, GLA_V, GLA_V, GLA_LOWRANK, GLA_LOWRANK, CONV_CH, CONV_CH])
            seg = lambda j: w_in[:, cuts[j]:cuts[j + 1]]
            lr_pad = jnp.zeros((d, LANES - 2 * GLA_LOWRANK), w_in.dtype)
            w_cat = jnp.concatenate([seg(0), seg(1), seg(2), seg(3), seg(6), seg(7), seg(4), seg(5), lr_pad],
                                    axis=1).astype(BF16)
            qk, v, g, glu, lr = _prenorm_proj(xt, mod, g_pre_mix[layer], w_cat,
                                              (2 * GLA_QK, GLA_V, GLA_V, 2 * CONV_CH, LANES), 0, 1, seq, batch)
            dwcat = jnp.zeros((LANES, 2 * GLA_QK), F32)
            dwcat = dwcat.at[:GLA_LOWRANK, :GLA_QK].set(gla_dw_f[i])
            dwcat = dwcat.at[GLA_LOWRANK:2 * GLA_LOWRANK, GLA_QK:].set(gla_dw_b[i]).astype(BF16)
            dbcat = jnp.concatenate([gla_db_f[i], gla_db_b[i]]).reshape(1, 2 * GLA_QK).astype(F32)
            a_out = _gla(qk, v, g, lr, dwcat, dbcat, gla_norm_g[i].reshape(1, GLA_V).astype(F32),
                         seq, ctx_len, batch)
            cw = conv_w[i].astype(F32)
            cvec = lambda t: t.reshape(1, CONV_CH).astype(F32)
            cv_x = _conv_module(glu, cw, cvec(conv_b[i]), cvec(conv_ln_g[i]), cvec(conv_ln_b[i]), seq, 0, batch)
            cv_c = _conv_module(glu, cw, cvec(conv_b[i]), cvec(conv_ln_g[i]), cvec(conv_ln_b[i]), ctx_len,
                                n_x // ctx_len, batch)
            cv = jnp.concatenate([cv_x, cv_c], axis=0)
            n_rows = n_x if last else n_all
            xt = _outproj([a_out, cv], ab_w_out[i].astype(BF16), xt, mod, g_post_mix[layer], 2, n_rows, seq, batch)
        else:
            (z,) = _prenorm_proj(xt, mod, g_pre_mix[layer], na_w_in[i].astype(BF16),
                                 (3 * NA_HEADS * NA_DH,), 0, 1, seq, batch)
            bias = _na_bias_table(na_rpb[i], seq // GRID_W)
            o = _na_attention(z, bias, seq, ctx_len, batch)
            if last:
                xt = _outproj([o], na_w_out[i].astype(BF16), xt, mod, g_post_mix[layer], 2, n_x, seq, batch)
            else:
                raise NotImplementedError("context update after a neighbourhood-attention layer")
        n_tok = n_x if last else n_all
        xt = _moe_layer(xt, mod, g_pre_ffn[layer], g_post_ffn[layer], moe_router_w[layer], moe_router_b[layer],
                        moe_w_gate, moe_w_up, moe_w_down, moe_ws_gate[layer],
                        moe_ws_up[layer], moe_ws_down[layer], n_tok, seq, batch, layer)
    return xt[:n_x].reshape(batch, seq, d)
```

```python
import functools

import numpy as np
import jax
import jax.numpy as jnp
from jax import lax
from jax.experimental import pallas as pl
from jax.experimental.pallas import tpu as pltpu

F32 = jnp.float32
BF16 = jnp.bfloat16

GRID_W = 64
RMS_EPS = 1e-6
LN_EPS = 1e-5
ADA_CHUNKS = 6
GLA_HEADS = 4
GLA_DK = 64
GLA_DV = 128
GLA_LOWRANK = 16
GLA_GATE_NORM = 16.0
GLA_CHUNK = 64
ROPE_BASE = 10000.0
GLA_QK = GLA_HEADS * GLA_DK
GLA_V = GLA_HEADS * GLA_DV
CONV_CH = 512
CONV_WIDTH = 31
NA_HEADS = 16
NA_DH = 64
NA_WIN_ROWS = 8
NA_WIN_COLS = 16
N_EXPERTS = 256
TOP_K = 8
N_GROUPS = 8
TOPK_GROUPS = 4
ROUTE_SCALE = 2.5

LANES = 128
ROW_TILE = 512
ADA_ROWS = 24
ADA_TN = 1536
CONV_RC = 128
CONV_HALO = 16
GLA_PAIR = 2
NA_ROW_UNROLL = 4
MOE_RB = 512
DISP_TT = 128
IDX_TILE = TOP_K * DISP_TT
VMEM_LIMIT = 56 * 1024 * 1024

_NT = (((1,), (1,)), ((), ()))
_TN = (((0,), (0,)), ((), ()))


def _params(sem, **kw):
    return pltpu.CompilerParams(dimension_semantics=sem, vmem_limit_bytes=VMEM_LIMIT, **kw)


def _sigmoid(x):
    return 1.0 / (1.0 + jnp.exp(-x))


def _split(a):
    hi = a.astype(BF16)
    lo = (a - hi.astype(F32)).astype(BF16)
    return hi, lo


def _dot3(a, b_hi, b_lo, dims=(((1,), (0,)), ((), ()))):
    a_hi, a_lo = _split(a)
    d = lambda x, y: lax.dot_general(x, y, dims, preferred_element_type=F32)
    return d(a_hi, b_hi) + (d(a_hi, b_lo) + d(a_lo, b_hi))


def _prenorm_mod(x, g, mod_ref, shift_i, scale_i):
    ms = jnp.mean(x * x, axis=-1, keepdims=True)
    h = x * lax.rsqrt(ms + RMS_EPS) * g
    return h * (1.0 + mod_ref[0, scale_i:scale_i + 1, :]) + mod_ref[0, shift_i:shift_i + 1, :]


def _mod_index(tile_rows, seq, batch):
    per_batch = seq // tile_rows
    return lambda i, *_: (jnp.minimum(i // per_batch, batch), 0, 0)


def _ada_kernel(c_ref, w_ref, b_ref, o_ref):
    c = c_ref[...]
    s = c * _sigmoid(c)
    w_hi, w_lo = _split(w_ref[...])
    o_ref[...] = _dot3(s, w_hi, w_lo) + b_ref[...]


def _ada_mod(cc, ada_w, ada_b):
    depth, d, n = ada_w.shape
    return pl.pallas_call(
        _ada_kernel,
        grid=(depth, n // ADA_TN),
        in_specs=[pl.BlockSpec((ADA_ROWS, d), lambda l, j: (0, 0)),
                  pl.BlockSpec((None, d, ADA_TN), lambda l, j: (l, 0, j)),
                  pl.BlockSpec((None, 1, ADA_TN), lambda l, j: (l, 0, j))],
        out_specs=pl.BlockSpec((None, ADA_ROWS, ADA_TN), lambda l, j: (l, 0, j)),
        out_shape=jax.ShapeDtypeStruct((depth, ADA_ROWS, n), F32),
        compiler_params=_params(("arbitrary", "arbitrary")),
        name="ada_mod",
    )(cc, ada_w, ada_b.reshape(depth, 1, n))


def _prenorm_proj_kernel(x_ref, mod_ref, g_ref, w_ref, *o_refs, shift_i, scale_i):
    h = _prenorm_mod(x_ref[...], g_ref[...], mod_ref, shift_i, scale_i).astype(BF16)
    off = 0
    for o_ref in o_refs:
        n = o_ref.shape[-1]
        o_ref[...] = jnp.dot(h, w_ref[:, off:off + n], preferred_element_type=F32).astype(o_ref.dtype)
        off += n


def _prenorm_proj(xt, mod, g, w, splits, shift_i, scale_i, seq, batch):
    rows, d = xt.shape
    n = w.shape[1]
    return pl.pallas_call(
        functools.partial(_prenorm_proj_kernel, shift_i=shift_i, scale_i=scale_i),
        grid=(rows // ROW_TILE,),
        in_specs=[pl.BlockSpec((ROW_TILE, d), lambda i: (i, 0)),
                  pl.BlockSpec((1, ADA_CHUNKS, d), _mod_index(ROW_TILE, seq, batch)),
                  pl.BlockSpec((1, d), lambda i: (0, 0)),
                  pl.BlockSpec((d, n), lambda i: (0, 0))],
        out_specs=[pl.BlockSpec((ROW_TILE, s), lambda i: (i, 0)) for s in splits],
        out_shape=[jax.ShapeDtypeStruct((rows, s), BF16) for s in splits],
        compiler_params=_params(("arbitrary",)),
        name="prenorm_proj",
    )(xt, mod, g.reshape(1, d), w)


def _gla_kernel(qkx_ref, vx_ref, gx_ref, lrx_ref, qkc_ref, vc_ref, gc_ref, lrc_ref,
                dw_ref, db_ref, cos_ref, sin_ref, gg_ref, ox_ref, oc_ref,
                sf_ref, sb_ref, accx_ref, accc_ref, *, n_x, n_c):
    L = GLA_CHUNK
    row = lax.broadcasted_iota(jnp.int32, (L, L), 0)
    col = lax.broadcasted_iota(jnp.int32, (L, L), 1)
    lower = row >= col
    upper = col >= row
    tri_f = jnp.where(lower, 1.0, 0.0).astype(BF16)
    tri_b = jnp.where(upper, 1.0, 0.0).astype(BF16)
    lane = lax.broadcasted_iota(jnp.int32, (1, GLA_QK), 1)
    head_of_lane = lane // GLA_DK
    first_half = (lane % (GLA_DK // 2)) < (GLA_DK // 4)
    bd_mask = (lax.broadcasted_iota(jnp.int32, (GLA_V, GLA_QK), 0) // GLA_DV
               == lax.broadcasted_iota(jnp.int32, (GLA_V, GLA_QK), 1) // GLA_DK)

    def swap_pairs(t):
        nf = GLA_DK // 4
        return jnp.where(first_half, pltpu.roll(t, GLA_QK - nf, 1), pltpu.roll(t, nf, 1))

    def chunk(fwd, c, qk_ref, v_ref, lr_ref, latent):
        rows = pl.ds(pl.multiple_of(c * L, L), L)
        q = qk_ref[rows, 0:GLA_QK].astype(F32) * (GLA_DK ** -0.5)
        k = qk_ref[rows, GLA_QK:2 * GLA_QK].astype(F32)
        if latent:
            cs = cos_ref[rows, :]
            sn = sin_ref[rows, :]
            q = q * cs + swap_pairs(q) * sn
            k = k * cs + swap_pairs(k) * sn
        dsl = slice(0, GLA_QK) if fwd else slice(GLA_QK, 2 * GLA_QK)
        pre = jnp.dot(lr_ref[rows, :], dw_ref[:, dsl], preferred_element_type=F32) + db_ref[:, dsl]
        yield
        logd = (jnp.minimum(pre, 0.0) - jnp.log(1.0 + jnp.exp(-jnp.abs(pre)))) * (1.0 / GLA_GATE_NORM)
        d_hi, d_lo = _split(logd)
        tri = tri_f if fwd else tri_b
        b = (jnp.dot(tri, d_hi, preferred_element_type=F32)
             + jnp.dot(tri, d_lo, preferred_element_type=F32))
        yield
        mid = L // 2 - 1 if fwd else L // 2
        end = L - 1 if fwd else 0
        b_mid = b[mid:mid + 1, :]
        b_end = b[end:end + 1, :]
        qe = q * jnp.exp(b - b_mid)
        ke = (k * jnp.exp(b_mid - b)).astype(BF16)
        q_in = (q * jnp.exp(b)).astype(BF16)
        k_out = (k * jnp.exp(b_end - b)).astype(BF16)
        dec = jnp.exp(b_end)
        vv = v_ref[rows, :]
        mask = lower if fwd else upper
        scores = []
        for h in range(GLA_HEADS):
            qm = jnp.where(head_of_lane == h, qe, 0.0).astype(BF16)
            scores.append(lax.dot_general(qm, ke, _NT, preferred_element_type=F32))
        yield
        outs = []
        for h in range(GLA_HEADS):
            a = jnp.where(mask, scores[h], 0.0).astype(BF16)
            outs.append(jnp.dot(a, vv[:, h * GLA_DV:(h + 1) * GLA_DV], preferred_element_type=F32))
        o_intra = jnp.concatenate(outs, axis=-1)
        yield

        def last():
            s_ref = sf_ref if fwd else sb_ref
            st = s_ref[...]
            o_inter = lax.dot_general(q_in, st.astype(BF16), _NT, preferred_element_type=F32)
            upd = lax.dot_general(vv, k_out, _TN, preferred_element_type=F32)
            s_ref[...] = jnp.where(bd_mask, st * dec + upd, 0.0)
            return o_intra + o_inter

        return last

    def run_chunks(specs):
        gens = [chunk(*sp) for sp in specs]
        lasts = [None] * len(gens)
        live = list(range(len(gens)))
        while live:
            for gi in list(live):
                try:
                    next(gens[gi])
                except StopIteration as stop:
                    lasts[gi] = stop.value
                    live.remove(gi)
        return [fn() for fn in lasts]

    def finish(o, g_ref, c):
        rows = pl.ds(pl.multiple_of(c * L, L), L)
        g = g_ref[rows, :].astype(F32)
        parts = []
        for h in range(GLA_HEADS):
            oh = o[:, h * GLA_DV:(h + 1) * GLA_DV]
            ms = jnp.mean(oh * oh, axis=-1, keepdims=True)
            parts.append(oh * lax.rsqrt(ms + RMS_EPS))
        on = jnp.concatenate(parts, axis=-1) * gg_ref[...]
        return (on * (g * _sigmoid(g))).astype(BF16)

    def run_seq(n, qk_ref, v_ref, g_ref, lr_ref, acc_ref, o_ref, latent):
        half = n // 2

        def rows_of(c):
            return pl.ds(pl.multiple_of(c * L, L), L)

        def chunks_of(i):
            cs = [(True, i * GLA_PAIR + u) for u in range(GLA_PAIR)]
            cs += [(False, n - 1 - (i * GLA_PAIR + u)) for u in range(GLA_PAIR)]
            return cs

        def first(i, carry):
            cs = chunks_of(i)
            res = run_chunks([(f, c, qk_ref, v_ref, lr_ref, latent) for f, c in cs])
            for (f, c), o in zip(cs, res):
                acc_ref[rows_of(c), :] = o
            return carry

        def second(i, carry):
            cs = chunks_of(i)
            res = run_chunks([(f, c, qk_ref, v_ref, lr_ref, latent) for f, c in cs])
            for (f, c), o in zip(cs, res):
                o_ref[rows_of(c), :] = finish(acc_ref[rows_of(c), :] + o, g_ref, c)
            return carry

        assert half % GLA_PAIR == 0
        lax.fori_loop(0, half // GLA_PAIR, first, 0)
        lax.fori_loop(half // GLA_PAIR, n // GLA_PAIR, second, 0)

    sf_ref[...] = jnp.zeros_like(sf_ref)
    sb_ref[...] = jnp.zeros_like(sb_ref)
    run_seq(n_c, qkc_ref, vc_ref, gc_ref, lrc_ref, accc_ref, oc_ref, False)
    run_seq(n_x, qkx_ref, vx_ref, gx_ref, lrx_ref, accx_ref, ox_ref, True)


def _rope_tables(seq):
    half = GLA_DK // 2
    nf = half // 2
    inv = ROPE_BASE ** (-jnp.arange(nf, dtype=F32) / nf)
    t = jnp.arange(seq)
    rows = (t // GRID_W).astype(F32)
    cols = (t % GRID_W).astype(F32)
    ar = rows[:, None] * inv[None, :]
    ac = cols[:, None] * inv[None, :]
    cos_h = jnp.concatenate([jnp.cos(ar), jnp.cos(ar), jnp.cos(ac), jnp.cos(ac)], axis=-1)
    sin_h = jnp.concatenate([-jnp.sin(ar), jnp.sin(ar), -jnp.sin(ac), jnp.sin(ac)], axis=-1)
    return jnp.tile(cos_h, (1, GLA_HEADS)), jnp.tile(sin_h, (1, GLA_HEADS))


def _gla(qk, v, g, lr, dwcat, dbcat, gla_g, seq, ctx_len, batch):
    rows = qk.shape[0]
    cos_t, sin_t = _rope_tables(seq)
    cb = (batch * seq) // ctx_len
    xs = lambda w: pl.BlockSpec((seq, w), lambda b: (b, 0))
    cs = lambda w: pl.BlockSpec((ctx_len, w), lambda b: (cb + b, 0))
    full = lambda a: pl.BlockSpec(a.shape, lambda b: (0,) * a.ndim)
    out_x, out_c = pl.pallas_call(
        functools.partial(_gla_kernel, n_x=seq // GLA_CHUNK, n_c=ctx_len // GLA_CHUNK),
        grid=(batch,),
        in_specs=[xs(2 * GLA_QK), xs(GLA_V), xs(GLA_V), xs(LANES),
                  cs(2 * GLA_QK), cs(GLA_V), cs(GLA_V), cs(LANES),
                  full(dwcat), full(dbcat), full(cos_t), full(sin_t), full(gla_g)],
        out_specs=[pl.BlockSpec((seq, GLA_V), lambda b: (b, 0)),
                   pl.BlockSpec((ctx_len, GLA_V), lambda b: (b, 0))],
        out_shape=[jax.ShapeDtypeStruct((batch * seq, GLA_V), BF16),
                   jax.ShapeDtypeStruct((batch * ctx_len, GLA_V), BF16)],
        scratch_shapes=[pltpu.VMEM((GLA_V, GLA_QK), F32), pltpu.VMEM((GLA_V, GLA_QK), F32),
                        pltpu.VMEM((seq, GLA_V), F32), pltpu.VMEM((ctx_len, GLA_V), F32)],
        compiler_params=_params(("arbitrary",)),
        name="gla",
    )(qk, v, g, lr, qk, v, g, lr, dwcat, dbcat, cos_t, sin_t, gla_g)
    del rows
    return jnp.concatenate([out_x, out_c], axis=0)


def _conv_kernel(glu_ref, w_ref, cb_ref, lg_ref, lb_ref, o_ref, pad_ref, win_ref, *, t_len):
    ch = CONV_CH
    zeros = jnp.zeros((CONV_HALO, ch), F32)
    pad_ref[0:CONV_HALO, :] = zeros
    pad_ref[CONV_HALO + t_len:2 * CONV_HALO + t_len, :] = zeros
    fill = 128

    def fill_body(i, carry):
        r = pl.multiple_of(i * fill, fill)
        a = glu_ref[pl.ds(r, fill), 0:ch].astype(F32)
        b = glu_ref[pl.ds(r, fill), ch:2 * ch].astype(F32)
        pad_ref[pl.ds(CONV_HALO + r, fill), :] = a * _sigmoid(b)
        return carry

    lax.fori_loop(0, t_len // fill, fill_body, 0)
    first_tap = CONV_HALO - CONV_WIDTH // 2

    def body(i, carry):
        base = pl.multiple_of(i * CONV_RC, CONV_RC)
        win_ref[...] = pad_ref[pl.ds(base, CONV_RC + 2 * CONV_HALO), :]
        acc = jnp.zeros((CONV_RC, ch), F32)
        for j in range(CONV_WIDTH):
            acc = acc + w_ref[j:j + 1, :] * win_ref[first_tap + j:first_tap + j + CONV_RC, :]
        acc = acc + cb_ref[...]
        mu = jnp.mean(acc, axis=-1, keepdims=True)
        cen = acc - mu
        var = jnp.mean(cen * cen, axis=-1, keepdims=True)
        y = cen * lax.rsqrt(var + LN_EPS) * lg_ref[...] + lb_ref[...]
        o_ref[pl.ds(base, CONV_RC), :] = (y * _sigmoid(y)).astype(BF16)
        return carry

    lax.fori_loop(0, t_len // CONV_RC, body, 0)


def _conv_module(glu, conv_w, conv_b, ln_g, ln_b, t_len, first_block, batch):
    full = lambda a: pl.BlockSpec(a.shape, lambda b: (0,) * a.ndim)
    return pl.pallas_call(
        functools.partial(_conv_kernel, t_len=t_len),
        grid=(batch,),
        in_specs=[pl.BlockSpec((t_len, 2 * CONV_CH), lambda b: (first_block + b, 0)),
                  full(conv_w), full(conv_b), full(ln_g), full(ln_b)],
        out_specs=pl.BlockSpec((t_len, CONV_CH), lambda b: (b, 0)),
        out_shape=jax.ShapeDtypeStruct((batch * t_len, CONV_CH), BF16),
        scratch_shapes=[pltpu.VMEM((t_len + 2 * CONV_HALO, CONV_CH), F32),
                        pltpu.VMEM((CONV_RC + 2 * CONV_HALO, CONV_CH), F32)],
        compiler_params=_params(("arbitrary",)),
        name="conv_module",
    )(glu, conv_w, conv_b, ln_g, ln_b)


def _outproj_kernel(*refs, n_lhs, gate_i):
    lhs_refs = refs[:n_lhs]
    w_ref, x_ref, mod_ref, g_ref, o_ref = refs[n_lhs:]
    off = 0
    y = None
    for l_ref in lhs_refs:
        kk = l_ref.shape[-1]
        t = jnp.dot(l_ref[...], w_ref[off:off + kk, :], preferred_element_type=F32)
        y = t if y is None else y + t
        off += kk
    ms = jnp.mean(y * y, axis=-1, keepdims=True)
    yn = y * lax.rsqrt(ms + RMS_EPS) * g_ref[...]
    o_ref[...] = x_ref[...] + mod_ref[0, gate_i:gate_i + 1, :] * yn


def _outproj(lhs_list, w, xt, mod, g, gate_i, n_rows, seq, batch):
    d = xt.shape[1]
    n_lhs = len(lhs_list)
    in_specs = [pl.BlockSpec((ROW_TILE, a.shape[1]), lambda i: (i, 0)) for a in lhs_list]
    in_specs += [pl.BlockSpec(w.shape, lambda i: (0, 0)),
                 pl.BlockSpec((ROW_TILE, d), lambda i: (i, 0)),
                 pl.BlockSpec((1, ADA_CHUNKS, d), _mod_index(ROW_TILE, seq, batch)),
                 pl.BlockSpec((1, d), lambda i: (0, 0))]
    return pl.pallas_call(
        functools.partial(_outproj_kernel, n_lhs=n_lhs, gate_i=gate_i),
        grid=(n_rows // ROW_TILE,),
        in_specs=in_specs,
        out_specs=pl.BlockSpec((ROW_TILE, d), lambda i: (i, 0)),
        out_shape=jax.ShapeDtypeStruct(xt.shape, F32),
        input_output_aliases={n_lhs + 1: 0},
        compiler_params=_params(("arbitrary",)),
        name="outproj",
    )(*lhs_list, w, xt, mod, g.reshape(1, d))


def _na_kernel(q_ref, k_ref, v_ref, kc_ref, vc_ref, bias_ref, o_ref, *, n_rows):
    wr = NA_WIN_ROWS
    lane = lax.broadcasted_iota(jnp.int32, (1, LANES), 1)
    head0 = lane < NA_DH
    kc = kc_ref[...]
    vc = vc_ref[...]

    def body(it, carry):
        rs = [it * NA_ROW_UNROLL + u for u in range(NA_ROW_UNROLL)]
        r0s = [jnp.clip(r - wr // 2, 0, n_rows - wr) for r in rs]
        win = [pl.ds(pl.multiple_of(r0 * GRID_W, GRID_W), wr * GRID_W) for r0 in r0s]
        scores = []
        for r, r0, w in zip(rs, r0s, win):
            q2 = q_ref[pl.ds(pl.multiple_of(r * GRID_W, GRID_W), GRID_W), :] * (NA_DH ** -0.5)
            zero = jnp.zeros_like(q2)
            qs = jnp.concatenate([jnp.where(head0, q2, zero), jnp.where(head0, zero, q2)], axis=0)
            bias = bias_ref[:, r - r0].reshape(2 * GRID_W, wr * GRID_W)
            s_w = lax.dot_general(qs, k_ref[w, :], _NT, preferred_element_type=F32) + bias
            s_c = lax.dot_general(qs, kc, _NT, preferred_element_type=F32)
            scores.append((s_w, s_c))
        probs = []
        for s_w, s_c in scores:
            m = jnp.maximum(jnp.max(s_w, axis=-1, keepdims=True), jnp.max(s_c, axis=-1, keepdims=True))
            p_w = jnp.exp(s_w - m)
            p_c = jnp.exp(s_c - m)
            den = jnp.sum(p_w, axis=-1, keepdims=True) + jnp.sum(p_c, axis=-1, keepdims=True)
            probs.append((p_w.astype(BF16), p_c.astype(BF16), den))
        for r, w, (p_w, p_c, den) in zip(rs, win, probs):
            o = (jnp.dot(p_w, v_ref[w, :], preferred_element_type=F32)
                 + jnp.dot(p_c, vc, preferred_element_type=F32)) / den
            o_ref[pl.ds(pl.multiple_of(r * GRID_W, GRID_W), GRID_W), :] = (
                jnp.where(head0, o[:GRID_W], o[GRID_W:]).astype(BF16))
        return carry

    assert n_rows % NA_ROW_UNROLL == 0
    lax.fori_loop(0, n_rows // NA_ROW_UNROLL, body, 0)


def _na_bias_table(rpb, n_rows):
    wr = min(NA_WIN_ROWS, n_rows)
    qc = np.arange(GRID_W)
    cstart = np.clip(qc - NA_WIN_COLS // 2, 0, GRID_W - NA_WIN_COLS)
    kcol = np.arange(GRID_W)
    ok = (kcol[None, :] >= cstart[:, None]) & (kcol[None, :] < cstart[:, None] + NA_WIN_COLS)
    cidx = np.clip(kcol[None, :] - qc[:, None] + NA_WIN_COLS - 1, 0, 2 * NA_WIN_COLS - 2)
    delta = np.arange(wr)
    j = np.arange(wr)
    ridx = j[None, :] - delta[:, None] + NA_WIN_ROWS - 1
    t = rpb[:, ridx][:, :, :, cidx]
    t = jnp.where(jnp.asarray(ok)[None, None, None], t.astype(F32), -jnp.inf)
    t = t.transpose(0, 1, 3, 2, 4)
    return t.reshape(rpb.shape[0], wr, GRID_W, wr * GRID_W)


def _na_attention(z, bias, seq, ctx_len, batch):
    n_pairs = NA_HEADS * NA_DH // LANES
    n_rows = seq // GRID_W
    cb = (batch * seq) // ctx_len
    return pl.pallas_call(
        functools.partial(_na_kernel, n_rows=n_rows),
        grid=(n_pairs, batch),
        in_specs=[pl.BlockSpec((seq, LANES), lambda p, b: (b, p)),
                  pl.BlockSpec((seq, LANES), lambda p, b: (b, n_pairs + p)),
                  pl.BlockSpec((seq, LANES), lambda p, b: (b, 2 * n_pairs + p)),
                  pl.BlockSpec((ctx_len, LANES), lambda p, b: (cb + b, n_pairs + p)),
                  pl.BlockSpec((ctx_len, LANES), lambda p, b: (cb + b, 2 * n_pairs + p)),
                  pl.BlockSpec((2,) + bias.shape[1:], lambda p, b: (p, 0, 0, 0))],
        out_specs=pl.BlockSpec((seq, LANES), lambda p, b: (b, p)),
        out_shape=jax.ShapeDtypeStruct((batch * seq, NA_HEADS * NA_DH), BF16),
        compiler_params=_params(("arbitrary", "arbitrary")),
        name="na_attention",
    )(z, z, z, z, z, bias)


def _store_token_tiles(ref, val, row0=0):
    n, d = val.shape
    nc = d // LANES
    for k in range(nc):
        ref[pl.ds(row0 * nc + k, n, stride=nc), :] = val[:, k * LANES:(k + 1) * LANES]


def _load_token_tiles(ref, row0, n, nc, lead=None):
    chunks = []
    for k in range(nc):
        idx = pl.ds(row0 * nc + k, n, stride=nc)
        chunks.append(ref[idx, :] if lead is None else ref[lead, idx, :])
    return jnp.concatenate(chunks, axis=-1)


def _first_argmax(x, iota_f, n):
    m = jnp.max(x, axis=0, keepdims=True)
    idx = jnp.min(jnp.where(x == m, iota_f, float(n)), axis=0, keepdims=True)
    return m, idx


def _moe_pre_kernel(x_ref, mod_ref, g_ref, rwh_ref, rwl_ref, rb_ref, wsg_ref, wsu_ref, wsd_ref,
                    h_ref, y0_ref, eidx_ref, gates_ref, rank_ref, cnt_ref, base_ref):
    tm = x_ref.shape[0]
    n_e = rwh_ref.shape[0]

    @pl.when(pl.program_id(0) == 0)
    def _():
        base_ref[...] = jnp.zeros_like(base_ref)

    h = _prenorm_mod(x_ref[...], g_ref[...], mod_ref, 3, 4)
    _store_token_tiles(h_ref, h)
    hb = h.astype(BF16)
    gate = jnp.dot(hb, wsg_ref[...], preferred_element_type=F32)
    up = jnp.dot(hb, wsu_ref[...], preferred_element_type=F32)
    hid = (gate * _sigmoid(gate) * up).astype(BF16)
    y0_ref[...] = jnp.dot(hid, wsd_ref[...], preferred_element_type=F32)

    h_lo = (h - hb.astype(F32)).astype(BF16)
    d = lambda a, b: lax.dot_general(a, b, _NT, preferred_element_type=F32)
    logits = d(rwh_ref[...], hb) + (d(rwh_ref[...], h_lo) + d(rwl_ref[...], hb))
    scores = _sigmoid(logits)
    biased = scores + rb_ref[...]
    neg = -jnp.inf
    gsz = n_e // N_GROUPS
    io_g = lax.broadcasted_iota(jnp.int32, (gsz, tm), 0).astype(F32)
    grp_rows = []
    for gi in range(N_GROUPS):
        xg = biased[gi * gsz:(gi + 1) * gsz]
        m1, i1 = _first_argmax(xg, io_g, gsz)
        m2 = jnp.max(jnp.where(io_g == i1, neg, xg), axis=0, keepdims=True)
        grp_rows.append(m1 + m2)
    cur = jnp.concatenate(grp_rows, axis=0)
    io_n = lax.broadcasted_iota(jnp.int32, (N_GROUPS, tm), 0).astype(F32)
    keep = jnp.zeros((N_GROUPS, tm), F32)
    for _ in range(TOPK_GROUPS):
        _, ii = _first_argmax(cur, io_n, N_GROUPS)
        sel = io_n == ii
        keep = jnp.where(sel, 1.0, keep)
        cur = jnp.where(sel, neg, cur)
    cur = jnp.concatenate([jnp.where(keep[gi:gi + 1] > 0.0, biased[gi * gsz:(gi + 1) * gsz], neg)
                           for gi in range(N_GROUPS)], axis=0)
    io_e = lax.broadcasted_iota(jnp.int32, (n_e, tm), 0).astype(F32)
    idxs, gvals = [], []
    chosen = jnp.zeros((n_e, tm), F32)
    for _ in range(TOP_K):
        _, ii = _first_argmax(cur, io_e, n_e)
        sel = io_e == ii
        idxs.append(ii)
        gvals.append(jnp.sum(jnp.where(sel, scores, 0.0), axis=0, keepdims=True))
        chosen = jnp.where(sel, 1.0, chosen)
        cur = jnp.where(sel, neg, cur)
    gv = jnp.concatenate(gvals, axis=0)
    gates_ref[...] = gv / jnp.sum(gv, axis=0, keepdims=True) * ROUTE_SCALE
    eidx_ref[...] = jnp.concatenate(idxs, axis=0).astype(jnp.int32)

    earlier = (lax.broadcasted_iota(jnp.int32, (tm, tm), 0) < lax.broadcasted_iota(jnp.int32, (tm, tm), 1))
    before = jnp.dot(chosen.astype(BF16), jnp.where(earlier, 1.0, 0.0).astype(BF16), preferred_element_type=F32)
    before = before + base_ref[:, 0:1]
    ranks = [jnp.sum(jnp.where(io_e == ii, before, 0.0), axis=0, keepdims=True) for ii in idxs]
    rank_ref[...] = jnp.concatenate(ranks, axis=0).astype(jnp.int32)
    base_ref[...] = base_ref[...] + jnp.sum(chosen, axis=1, keepdims=True)
    cnt_ref[...] = base_ref[...].astype(jnp.int32)


def _moe_pre(xt, mod, g, rwt_hi, rwt_lo, rb, wsg, wsu, wsd, n_tok, seq, batch):
    d = xt.shape[1]
    e = rwt_hi.shape[0]
    full = lambda a: pl.BlockSpec(a.shape, lambda i: (0,) * a.ndim)
    tok_major = lambda: pl.BlockSpec((TOP_K, ROW_TILE), lambda i: (0, i))
    return pl.pallas_call(
        _moe_pre_kernel,
        grid=(n_tok // ROW_TILE,),
        in_specs=[pl.BlockSpec((ROW_TILE, d), lambda i: (i, 0)),
                  pl.BlockSpec((1, ADA_CHUNKS, d), _mod_index(ROW_TILE, seq, batch)),
                  pl.BlockSpec((1, d), lambda i: (0, 0)),
                  full(rwt_hi), full(rwt_lo), full(rb), full(wsg), full(wsu), full(wsd)],
        out_specs=[pl.BlockSpec((ROW_TILE * (d // LANES), LANES), lambda i: (i, 0)),
                   pl.BlockSpec((ROW_TILE, d), lambda i: (i, 0)),
                   tok_major(), tok_major(), tok_major(),
                   pl.BlockSpec((e, LANES), lambda i: (0, 0))],
        out_shape=[jax.ShapeDtypeStruct((n_tok * (d // LANES), LANES), F32),
                   jax.ShapeDtypeStruct((n_tok, d), F32),
                   jax.ShapeDtypeStruct((TOP_K, n_tok), jnp.int32),
                   jax.ShapeDtypeStruct((TOP_K, n_tok), F32),
                   jax.ShapeDtypeStruct((TOP_K, n_tok), jnp.int32),
                   jax.ShapeDtypeStruct((e, LANES), jnp.int32)],
        scratch_shapes=[pltpu.VMEM((e, LANES), F32)],
        compiler_params=_params(("arbitrary",)),
        name="moe_pre",
    )(xt, mod, g.reshape(1, d), rwt_hi, rwt_lo, rb, wsg, wsu, wsd)


def _moe_pos_kernel(eidx_ref, rank_ref, pstart_ref, pos_ref):
    n_e = pstart_ref.shape[0]
    tp = eidx_ref.shape[1]
    io_e = lax.broadcasted_iota(jnp.int32, (n_e, tp), 0)
    ps = pstart_ref[...]
    e = eidx_ref[...]
    rows = [jnp.sum(jnp.where(io_e == e[k:k + 1, :], ps, 0.0), axis=0, keepdims=True) for k in range(TOP_K)]
    pos_ref[...] = rank_ref[...] + jnp.concatenate(rows, axis=0).astype(jnp.int32)


def _moe_pos(eidx, rank, pstart):
    n_tok = eidx.shape[1]
    blk = lambda: pl.BlockSpec((TOP_K, ROW_TILE), lambda i: (0, i))
    return pl.pallas_call(
        _moe_pos_kernel,
        grid=(n_tok // ROW_TILE,),
        in_specs=[blk(), blk(), pl.BlockSpec(pstart.shape, lambda i: (0, 0))],
        out_specs=blk(),
        out_shape=jax.ShapeDtypeStruct(eidx.shape, jnp.int32),
        compiler_params=_params(("arbitrary",)),
        name="moe_pos",
    )(eidx, rank, pstart)


def _moe_dispatch_kernel(lastblk_ref, nused_ref, pos_hbm, h_hbm, xg_hbm, idx_smem, zbuf, hbuf, idx_sem, row_sem, z_sem,
                         h_sem, *, n_tiles, nc, n_blocks):
    i = pl.program_id(0)
    slot = i % 2
    n_e = lastblk_ref.shape[0]
    rows_per_step = DISP_TT * TOP_K

    def h_copy(t, s):
        return pltpu.make_async_copy(h_hbm.at[pl.ds(pl.multiple_of(t * (DISP_TT * nc), DISP_TT * nc), DISP_TT * nc), :],
                                     hbuf.at[s], h_sem.at[s])

    def idx_copy(t, s):
        return pltpu.make_async_copy(pos_hbm.at[pl.ds(pl.multiple_of(t * IDX_TILE, IDX_TILE), IDX_TILE)],
                                     idx_smem.at[pl.ds(pl.multiple_of(s * IDX_TILE, IDX_TILE), IDX_TILE)],
                                     idx_sem.at[s])

    def block_copy(b):
        return pltpu.make_async_copy(zbuf, xg_hbm.at[pl.ds(pl.multiple_of(b * (MOE_RB * nc), MOE_RB * nc),
                                                           MOE_RB * nc), :], z_sem)

    def wait_rows(s):
        pltpu.make_async_copy(h_hbm.at[pl.ds(0, rows_per_step * nc), :], xg_hbm.at[pl.ds(0, rows_per_step * nc), :],
                              row_sem.at[s]).wait()

    @pl.when(i == 0)
    def _():
        idx_copy(0, 0).start()
        zbuf[...] = jnp.zeros_like(zbuf)

        def for_zero_blocks(fn):
            def per_expert(e, carry):
                @pl.when(lastblk_ref[e] >= 0)
                def _():
                    fn(block_copy(lastblk_ref[e]))
                return carry

            def per_block(b, carry):
                fn(block_copy(b))
                return carry

            lax.fori_loop(0, n_e, per_expert, 0)
            lax.fori_loop(nused_ref[0], n_blocks, per_block, 0)

        for_zero_blocks(lambda cp: cp.start())
        for_zero_blocks(lambda cp: cp.wait())

        h_copy(0, 0).start()

    idx_copy(i, slot).wait()
    hslot = i % 3
    h_copy(i, hslot).wait()

    @pl.when(i + 1 < n_tiles)
    def _():
        idx_copy(i + 1, 1 - slot).start()
        h_copy(i + 1, (i + 1) % 3).start()

    def body(r, carry):
        src = hbuf.at[hslot, pl.ds(pl.multiple_of(r * nc, nc), nc), :]
        base = slot * IDX_TILE + r
        for k in range(TOP_K):
            p = idx_smem[base + k * DISP_TT]
            pltpu.make_async_copy(src, xg_hbm.at[pl.ds(pl.multiple_of(p * nc, nc), nc), :],
                                  row_sem.at[slot]).start(priority=k % 2)
        return carry

    lax.fori_loop(0, DISP_TT, body, 0, unroll=2)

    @pl.when(i > 0)
    def _():
        wait_rows(1 - slot)

    @pl.when(i == n_tiles - 1)
    def _():
        wait_rows(slot)


def _moe_dispatch(lastblk, nused, pos, h, n_blocks):
    n_tok = pos.shape[0] // TOP_K
    nc = h.shape[0] // n_tok
    n_tiles = n_tok // DISP_TT
    grid_spec = pltpu.PrefetchScalarGridSpec(
        num_scalar_prefetch=2,
        grid=(n_tiles,),
        in_specs=[pl.BlockSpec(memory_space=pl.ANY), pl.BlockSpec(memory_space=pl.ANY)],
        out_specs=pl.BlockSpec(memory_space=pl.ANY),
        scratch_shapes=[pltpu.SMEM((2 * IDX_TILE,), jnp.int32),
                        pltpu.VMEM((MOE_RB * nc, LANES), F32),
                        pltpu.VMEM((3, DISP_TT * nc, LANES), F32),
                        pltpu.SemaphoreType.DMA((2,)),
                        pltpu.SemaphoreType.DMA((2,)),
                        pltpu.SemaphoreType.DMA(()),
                        pltpu.SemaphoreType.DMA((3,))],
    )
    return pl.pallas_call(
        functools.partial(_moe_dispatch_kernel, n_tiles=n_tiles, nc=nc, n_blocks=n_blocks),
        grid_spec=grid_spec,
        out_shape=jax.ShapeDtypeStruct((n_blocks * MOE_RB * nc, LANES), F32),
        compiler_params=_params(("arbitrary",)),
        name="moe_dispatch",
    )(lastblk, nused, pos, h)


def _moe_expert_kernel(be_ref, nused_ref, x_ref, wg_ref, wu_ref, wd_ref, o_ref, wgb, wub, wdb):
    i = pl.program_id(0)
    nused = nused_ref[0]
    nc = x_ref.shape[0] // MOE_RB
    new_expert = jnp.logical_or(i == 0, be_ref[i] != be_ref[jnp.maximum(i - 1, 0)])

    @pl.when(jnp.logical_and(i < nused, new_expert))
    def _():
        wgb[...] = wg_ref[...].astype(BF16)
        wub[...] = wu_ref[...].astype(BF16)
        wdb[...] = wd_ref[...].astype(BF16)

    @pl.when(i < nused)
    def _():
        xb = _load_token_tiles(x_ref, 0, MOE_RB, nc).astype(BF16)
        gate = jnp.dot(xb, wgb[...], preferred_element_type=F32)
        up = jnp.dot(xb, wub[...], preferred_element_type=F32)
        hid = (gate * _sigmoid(gate) * up).astype(BF16)
        _store_token_tiles(o_ref, jnp.dot(hid, wdb[...], preferred_element_type=F32))

    @pl.when(i >= nused)
    def _():
        o_ref[...] = jnp.zeros_like(o_ref)


def _moe_experts(block_e, nused, xg, wg, wu, wd, layer):
    nb = block_e.shape[0]
    d = wg.shape[2]
    nc = d // LANES
    de = wg.shape[3]
    grid_spec = pltpu.PrefetchScalarGridSpec(
        num_scalar_prefetch=2,
        grid=(nb,),
        in_specs=[pl.BlockSpec((MOE_RB * nc, LANES), lambda i, be, nu: (jnp.minimum(i, nu[0] - 1), 0)),
                  pl.BlockSpec((None, None, d, de), lambda i, be, nu: (layer, be[i], 0, 0)),
                  pl.BlockSpec((None, None, d, de), lambda i, be, nu: (layer, be[i], 0, 0)),
                  pl.BlockSpec((None, None, de, d), lambda i, be, nu: (layer, be[i], 0, 0))],
        out_specs=pl.BlockSpec((MOE_RB * nc, LANES), lambda i, be, nu: (i, 0)),
        scratch_shapes=[pltpu.VMEM((d, de), BF16), pltpu.VMEM((d, de), BF16), pltpu.VMEM((de, d), BF16)],
    )
    return pl.pallas_call(
        _moe_expert_kernel,
        grid_spec=grid_spec,
        out_shape=jax.ShapeDtypeStruct((nb * MOE_RB * nc, LANES), F32),
        compiler_params=_params(("arbitrary",)),
        name="moe_experts",
    )(block_e, nused, xg, wg, wu, wd)


def _moe_combine_kernel(pos_hbm, og_hbm, gates_ref, y0_ref, x_ref, mod_ref, g_ref, o_ref,
                        idx_smem, buf, idx_sem, row_sem, *, n_tiles):
    i = pl.program_id(0)
    slot = i % 2
    nxt = 1 - slot
    n_rows = TOP_K * DISP_TT
    nc = buf.shape[1] // n_rows

    def idx_copy(t, s):
        return pltpu.make_async_copy(pos_hbm.at[pl.ds(pl.multiple_of(t * IDX_TILE, IDX_TILE), IDX_TILE)],
                                     idx_smem.at[pl.ds(pl.multiple_of(s * IDX_TILE, IDX_TILE), IDX_TILE)],
                                     idx_sem.at[s])

    def issue_rows(s):
        def body(r, carry):
            base = s * IDX_TILE + r
            for k in range(TOP_K):
                p = idx_smem[base + k * DISP_TT]
                pltpu.make_async_copy(og_hbm.at[pl.ds(pl.multiple_of(p * nc, nc), nc), :],
                                      buf.at[s, pl.ds(pl.multiple_of((k * DISP_TT + r) * nc, nc), nc), :],
                                      row_sem.at[s]).start(priority=k % 2)
            return carry
        lax.fori_loop(0, DISP_TT, body, 0, unroll=2)

    def wait_rows(s):
        pltpu.make_async_copy(og_hbm.at[pl.ds(0, n_rows * nc), :], buf.at[s], row_sem.at[s]).wait()

    @pl.when(i == 0)
    def _():
        idx_copy(0, 0).start()
        idx_copy(0, 0).wait()
        issue_rows(0)

        @pl.when(n_tiles > 1)
        def _():
            idx_copy(1, 1).start()

    @pl.when(i + 1 < n_tiles)
    def _():
        idx_copy(i + 1, nxt).wait()
        issue_rows(nxt)

    wait_rows(slot)

    @pl.when(i + 2 < n_tiles)
    def _():
        idx_copy(i + 2, slot).start()

    gates = gates_ref[...].T
    y = y0_ref[...]
    for k in range(TOP_K):
        y = y + gates[:, k:k + 1] * _load_token_tiles(buf, k * DISP_TT, DISP_TT, nc, lead=slot)
    ms = jnp.mean(y * y, axis=-1, keepdims=True)
    yn = y * lax.rsqrt(ms + RMS_EPS) * g_ref[...]
    o_ref[...] = x_ref[...] + mod_ref[0, 5:6, :] * yn


def _moe_combine(pos, og, gates, y0, xt, mod, g, n_tok, seq, batch):
    d = xt.shape[1]
    n_tiles = n_tok // DISP_TT
    row_spec = lambda w: pl.BlockSpec((DISP_TT, w), lambda i: (i, 0))
    return pl.pallas_call(
        functools.partial(_moe_combine_kernel, n_tiles=n_tiles),
        grid=(n_tiles,),
        in_specs=[pl.BlockSpec(memory_space=pl.ANY),
                  pl.BlockSpec(memory_space=pl.ANY),
                  pl.BlockSpec((TOP_K, DISP_TT), lambda i: (0, i)), row_spec(d), row_spec(d),
                  pl.BlockSpec((1, ADA_CHUNKS, d), _mod_index(DISP_TT, seq, batch)),
                  pl.BlockSpec((1, d), lambda i: (0, 0))],
        out_specs=row_spec(d),
        out_shape=jax.ShapeDtypeStruct((n_tok, d), F32),
        input_output_aliases={4: 0} if n_tok == xt.shape[0] else {},
        scratch_shapes=[pltpu.SMEM((2 * IDX_TILE,), jnp.int32),
                        pltpu.VMEM((2, TOP_K * DISP_TT * (d // LANES), LANES), F32),
                        pltpu.SemaphoreType.DMA((2,)),
                        pltpu.SemaphoreType.DMA((2,))],
        compiler_params=_params(("arbitrary",)),
        name="moe_combine",
    )(pos, og, gates, y0, xt, mod, g.reshape(1, d))


def _dispatch_plan(counts, n_assign):
    e = counts.shape[0]
    padded = (counts + MOE_RB - 1) // MOE_RB * MOE_RB
    pend = jnp.cumsum(padded)
    pstart = pend - padded
    nb = -(-(n_assign + e * (MOE_RB - 1)) // MOE_RB)
    first_row = jnp.arange(nb, dtype=jnp.int32) * MOE_RB
    block_e = jnp.minimum(jnp.sum((pend[None, :] <= first_row[:, None]).astype(jnp.int32), axis=1), e - 1)
    nused = (pend[-1] // MOE_RB).astype(jnp.int32).reshape(1)
    lastblk = jnp.where(counts > 0, pend // MOE_RB - 1, -1)
    return pstart.astype(jnp.int32), lastblk.astype(jnp.int32), block_e.astype(jnp.int32), nused, nb


def _moe_layer(xt, mod, g_pre, g_post, router_w, router_b, w_gate, w_up, w_down, ws_gate, ws_up, ws_down,
               n_tok, seq, batch, layer):
    e = router_w.shape[1]
    rwt = router_w.astype(F32).T
    rwt_hi, rwt_lo = _split(rwt)
    h, y0, eidx, gates, rank, cnt = _moe_pre(xt, mod, g_pre, rwt_hi, rwt_lo, router_b.astype(F32).reshape(e, 1),
                                             ws_gate.astype(BF16), ws_up.astype(BF16), ws_down.astype(BF16),
                                             n_tok, seq, batch)
    pstart, lastblk, block_e, nused, nb = _dispatch_plan(cnt[:, 0], n_tok * TOP_K)
    pos = _moe_pos(eidx, rank, pstart.astype(F32).reshape(e, 1))
    pos = pos.reshape(TOP_K, n_tok // DISP_TT, DISP_TT).transpose(1, 0, 2).reshape(-1)
    xg = _moe_dispatch(lastblk, nused, pos, h, nb)
    og = _moe_experts(block_e, nused, xg, w_gate, w_up, w_down, layer)
    return _moe_combine(pos, og, gates, y0, xt, mod, g_post, n_tok, seq, batch)


def kernel(x, c, ctx, c_ctx, ada_w, ada_b, g_pre_mix, g_post_mix, g_pre_ffn, g_post_ffn, ab_w_in, ab_w_out, gla_dw_f, gla_db_f, gla_dw_b, gla_db_b, gla_norm_g, conv_w, conv_b, conv_ln_g, conv_ln_b, na_w_in, na_w_out, na_rpb, moe_router_w, moe_router_b, moe_w_gate, moe_w_up, moe_w_down, moe_ws_gate, moe_ws_up, moe_ws_down):
    batch, seq, d = x.shape
    ctx_len = ctx.shape[1]
    depth = ada_w.shape[0]
    n_x = batch * seq
    n_all = n_x + batch * ctx_len
    assert seq % ROW_TILE == 0 and (batch * ctx_len) % ROW_TILE == 0 and n_x % ctx_len == 0

    xt = jnp.concatenate([x.reshape(n_x, d), ctx.reshape(batch * ctx_len, d)], axis=0)

    cc = jnp.zeros((ADA_ROWS, d), F32).at[:batch].set(c).at[batch].set(c_ctx)
    mods = _ada_mod(cc, ada_w, ada_b)[:, :batch + 1].reshape(depth, batch + 1, ADA_CHUNKS, d)

    for layer in range(depth):
        last = layer == depth - 1
        i = layer // 2
        mod = mods[layer]
        if layer % 2 == 0:
            w_in = ab_w_in[i]
            cuts = np.cumsum([0, GLA_QK, GLA_QK, GLA_V, GLA_V, GLA_LOWRANK, GLA_LOWRANK, CONV_CH, CONV_CH])
            seg = lambda j: w_in[:, cuts[j]:cuts[j + 1]]
            lr_pad = jnp.zeros((d, LANES - 2 * GLA_LOWRANK), w_in.dtype)
            w_cat = jnp.concatenate([seg(0), seg(1), seg(2), seg(3), seg(6), seg(7), seg(4), seg(5), lr_pad],
                                    axis=1).astype(BF16)
            qk, v, g, glu, lr = _prenorm_proj(xt, mod, g_pre_mix[layer], w_cat,
                                              (2 * GLA_QK, GLA_V, GLA_V, 2 * CONV_CH, LANES), 0, 1, seq, batch)
            dwcat = jnp.zeros((LANES, 2 * GLA_QK), F32)
            dwcat = dwcat.at[:GLA_LOWRANK, :GLA_QK].set(gla_dw_f[i])
            dwcat = dwcat.at[GLA_LOWRANK:2 * GLA_LOWRANK, GLA_QK:].set(gla_dw_b[i]).astype(BF16)
            dbcat = jnp.concatenate([gla_db_f[i], gla_db_b[i]]).reshape(1, 2 * GLA_QK).astype(F32)
            a_out = _gla(qk, v, g, lr, dwcat, dbcat, gla_norm_g[i].reshape(1, GLA_V).astype(F32),
                         seq, ctx_len, batch)
            cw = conv_w[i].astype(F32)
            cvec = lambda t: t.reshape(1, CONV_CH).astype(F32)
            cv_x = _conv_module(glu, cw, cvec(conv_b[i]), cvec(conv_ln_g[i]), cvec(conv_ln_b[i]), seq, 0, batch)
            cv_c = _conv_module(glu, cw, cvec(conv_b[i]), cvec(conv_ln_g[i]), cvec(conv_ln_b[i]), ctx_len,
                                n_x // ctx_len, batch)
            cv = jnp.concatenate([cv_x, cv_c], axis=0)
            n_rows = n_x if last else n_all
            xt = _outproj([a_out, cv], ab_w_out[i].astype(BF16), xt, mod, g_post_mix[layer], 2, n_rows, seq, batch)
        else:
            (z,) = _prenorm_proj(xt, mod, g_pre_mix[layer], na_w_in[i].astype(BF16),
                                 (3 * NA_HEADS * NA_DH,), 0, 1, seq, batch)
            bias = _na_bias_table(na_rpb[i], seq // GRID_W)
            o = _na_attention(z, bias, seq, ctx_len, batch)
            if last:
                xt = _outproj([o], na_w_out[i].astype(BF16), xt, mod, g_post_mix[layer], 2, n_x, seq, batch)
            else:
                raise NotImplementedError("context update after a neighbourhood-attention layer")
        n_tok = n_x if last else n_all
        xt = _moe_layer(xt, mod, g_pre_ffn[layer], g_post_ffn[layer], moe_router_w[layer], moe_router_b[layer],
                        moe_w_gate, moe_w_up, moe_w_down, moe_ws_gate[layer],
                        moe_ws_up[layer], moe_ws_down[layer], n_tok, seq, batch, layer)
    return xt[:n_x].reshape(batch, seq, d)
```

```python
import functools

import numpy as np
import jax
import jax.numpy as jnp
from jax import lax
from jax.experimental import pallas as pl
from jax.experimental.pallas import tpu as pltpu

F32 = jnp.float32
BF16 = jnp.bfloat16

GRID_W = 64
RMS_EPS = 1e-6
LN_EPS = 1e-5
ADA_CHUNKS = 6
GLA_HEADS = 4
GLA_DK = 64
GLA_DV = 128
GLA_LOWRANK = 16
GLA_GATE_NORM = 16.0
GLA_CHUNK = 64
ROPE_BASE = 10000.0
GLA_QK = GLA_HEADS * GLA_DK
GLA_V = GLA_HEADS * GLA_DV
CONV_CH = 512
CONV_WIDTH = 31
NA_HEADS = 16
NA_DH = 64
NA_WIN_ROWS = 8
NA_WIN_COLS = 16
N_EXPERTS = 256
TOP_K = 8
N_GROUPS = 8
TOPK_GROUPS = 4
ROUTE_SCALE = 2.5

LANES = 128
ROW_TILE = 512
ADA_ROWS = 24
ADA_TN = 1536
CONV_RC = 128
CONV_HALO = 16
GLA_PAIR = 2
NA_ROW_UNROLL = 4
MOE_RB = 384
DISP_TT = 128
IDX_TILE = TOP_K * DISP_TT
VMEM_LIMIT = 56 * 1024 * 1024

_NT = (((1,), (1,)), ((), ()))
_TN = (((0,), (0,)), ((), ()))


def _params(sem, **kw):
    return pltpu.CompilerParams(dimension_semantics=sem, vmem_limit_bytes=VMEM_LIMIT, **kw)


def _sigmoid(x):
    return 1.0 / (1.0 + jnp.exp(-x))


def _split(a):
    hi = a.astype(BF16)
    lo = (a - hi.astype(F32)).astype(BF16)
    return hi, lo


def _dot3(a, b_hi, b_lo, dims=(((1,), (0,)), ((), ()))):
    a_hi, a_lo = _split(a)
    d = lambda x, y: lax.dot_general(x, y, dims, preferred_element_type=F32)
    return d(a_hi, b_hi) + (d(a_hi, b_lo) + d(a_lo, b_hi))


def _prenorm_mod(x, g, mod_ref, shift_i, scale_i):
    ms = jnp.mean(x * x, axis=-1, keepdims=True)
    h = x * lax.rsqrt(ms + RMS_EPS) * g
    return h * (1.0 + mod_ref[0, scale_i:scale_i + 1, :]) + mod_ref[0, shift_i:shift_i + 1, :]


def _mod_index(tile_rows, seq, batch):
    per_batch = seq // tile_rows
    return lambda i, *_: (jnp.minimum(i // per_batch, batch), 0, 0)


def _ada_kernel(c_ref, w_ref, b_ref, o_ref):
    c = c_ref[...]
    s = c * _sigmoid(c)
    w_hi, w_lo = _split(w_ref[...])
    o_ref[...] = _dot3(s, w_hi, w_lo) + b_ref[...]


def _ada_mod(cc, ada_w, ada_b):
    depth, d, n = ada_w.shape
    return pl.pallas_call(
        _ada_kernel,
        grid=(depth, n // ADA_TN),
        in_specs=[pl.BlockSpec((ADA_ROWS, d), lambda l, j: (0, 0)),
                  pl.BlockSpec((None, d, ADA_TN), lambda l, j: (l, 0, j)),
                  pl.BlockSpec((None, 1, ADA_TN), lambda l, j: (l, 0, j))],
        out_specs=pl.BlockSpec((None, ADA_ROWS, ADA_TN), lambda l, j: (l, 0, j)),
        out_shape=jax.ShapeDtypeStruct((depth, ADA_ROWS, n), F32),
        compiler_params=_params(("arbitrary", "arbitrary")),
        name="ada_mod",
    )(cc, ada_w, ada_b.reshape(depth, 1, n))


def _prenorm_proj_kernel(x_ref, mod_ref, g_ref, w_ref, *o_refs, shift_i, scale_i):
    h = _prenorm_mod(x_ref[...], g_ref[...], mod_ref, shift_i, scale_i).astype(BF16)
    off = 0
    for o_ref in o_refs:
        n = o_ref.shape[-1]
        o_ref[...] = jnp.dot(h, w_ref[:, off:off + n], preferred_element_type=F32).astype(o_ref.dtype)
        off += n


def _prenorm_proj(xt, mod, g, w, splits, shift_i, scale_i, seq, batch):
    rows, d = xt.shape
    n = w.shape[1]
    return pl.pallas_call(
        functools.partial(_prenorm_proj_kernel, shift_i=shift_i, scale_i=scale_i),
        grid=(rows // ROW_TILE,),
        in_specs=[pl.BlockSpec((ROW_TILE, d), lambda i: (i, 0)),
                  pl.BlockSpec((1, ADA_CHUNKS, d), _mod_index(ROW_TILE, seq, batch)),
                  pl.BlockSpec((1, d), lambda i: (0, 0)),
                  pl.BlockSpec((d, n), lambda i: (0, 0))],
        out_specs=[pl.BlockSpec((ROW_TILE, s), lambda i: (i, 0)) for s in splits],
        out_shape=[jax.ShapeDtypeStruct((rows, s), BF16) for s in splits],
        compiler_params=_params(("arbitrary",)),
        name="prenorm_proj",
    )(xt, mod, g.reshape(1, d), w)


def _gla_kernel(qkx_ref, vx_ref, gx_ref, lrx_ref, qkc_ref, vc_ref, gc_ref, lrc_ref,
                dw_ref, db_ref, cos_ref, sin_ref, gg_ref, ox_ref, oc_ref,
                sf_ref, sb_ref, accx_ref, accc_ref, *, n_x, n_c):
    L = GLA_CHUNK
    row = lax.broadcasted_iota(jnp.int32, (L, L), 0)
    col = lax.broadcasted_iota(jnp.int32, (L, L), 1)
    lower = row >= col
    upper = col >= row
    tri_f = jnp.where(lower, 1.0, 0.0).astype(BF16)
    tri_b = jnp.where(upper, 1.0, 0.0).astype(BF16)
    lane = lax.broadcasted_iota(jnp.int32, (1, GLA_QK), 1)
    head_of_lane = lane // GLA_DK
    first_half = (lane % (GLA_DK // 2)) < (GLA_DK // 4)
    bd_mask = (lax.broadcasted_iota(jnp.int32, (GLA_V, GLA_QK), 0) // GLA_DV
               == lax.broadcasted_iota(jnp.int32, (GLA_V, GLA_QK), 1) // GLA_DK)

    def swap_pairs(t):
        nf = GLA_DK // 4
        return jnp.where(first_half, pltpu.roll(t, GLA_QK - nf, 1), pltpu.roll(t, nf, 1))

    def chunk(fwd, c, qk_ref, v_ref, lr_ref, latent):
        rows = pl.ds(pl.multiple_of(c * L, L), L)
        q = qk_ref[rows, 0:GLA_QK].astype(F32) * (GLA_DK ** -0.5)
        k = qk_ref[rows, GLA_QK:2 * GLA_QK].astype(F32)
        if latent:
            cs = cos_ref[rows, :]
            sn = sin_ref[rows, :]
            q = q * cs + swap_pairs(q) * sn
            k = k * cs + swap_pairs(k) * sn
        dsl = slice(0, GLA_QK) if fwd else slice(GLA_QK, 2 * GLA_QK)
        pre = jnp.dot(lr_ref[rows, :], dw_ref[:, dsl], preferred_element_type=F32) + db_ref[:, dsl]
        yield
        logd = (jnp.minimum(pre, 0.0) - jnp.log(1.0 + jnp.exp(-jnp.abs(pre)))) * (1.0 / GLA_GATE_NORM)
        d_hi, d_lo = _split(logd)
        tri = tri_f if fwd else tri_b
        b = (jnp.dot(tri, d_hi, preferred_element_type=F32)
             + jnp.dot(tri, d_lo, preferred_element_type=F32))
        yield
        mid = L // 2 - 1 if fwd else L // 2
        end = L - 1 if fwd else 0
        b_mid = b[mid:mid + 1, :]
        b_end = b[end:end + 1, :]
        qe = q * jnp.exp(b - b_mid)
        ke = (k * jnp.exp(b_mid - b)).astype(BF16)
        q_in = (q * jnp.exp(b)).astype(BF16)
        k_out = (k * jnp.exp(b_end - b)).astype(BF16)
        dec = jnp.exp(b_end)
        vv = v_ref[rows, :]
        mask = lower if fwd else upper
        scores = []
        for h in range(GLA_HEADS):
            qm = jnp.where(head_of_lane == h, qe, 0.0).astype(BF16)
            scores.append(lax.dot_general(qm, ke, _NT, preferred_element_type=F32))
        yield
        outs = []
        for h in range(GLA_HEADS):
            a = jnp.where(mask, scores[h], 0.0).astype(BF16)
            outs.append(jnp.dot(a, vv[:, h * GLA_DV:(h + 1) * GLA_DV], preferred_element_type=F32))
        o_intra = jnp.concatenate(outs, axis=-1)
        yield

        def last():
            s_ref = sf_ref if fwd else sb_ref
            st = s_ref[...]
            o_inter = lax.dot_general(q_in, st.astype(BF16), _NT, preferred_element_type=F32)
            upd = lax.dot_general(vv, k_out, _TN, preferred_element_type=F32)
            s_ref[...] = jnp.where(bd_mask, st * dec + upd, 0.0)
            return o_intra + o_inter

        return last

    def run_chunks(specs):
        gens = [chunk(*sp) for sp in specs]
        lasts = [None] * len(gens)
        live = list(range(len(gens)))
        while live:
            for gi in list(live):
                try:
                    next(gens[gi])
                except StopIteration as stop:
                    lasts[gi] = stop.value
                    live.remove(gi)
        return [fn() for fn in lasts]

    def finish(o, g_ref, c):
        rows = pl.ds(pl.multiple_of(c * L, L), L)
        g = g_ref[rows, :].astype(F32)
        parts = []
        for h in range(GLA_HEADS):
            oh = o[:, h * GLA_DV:(h + 1) * GLA_DV]
            ms = jnp.mean(oh * oh, axis=-1, keepdims=True)
            parts.append(oh * lax.rsqrt(ms + RMS_EPS))
        on = jnp.concatenate(parts, axis=-1) * gg_ref[...]
        return (on * (g * _sigmoid(g))).astype(BF16)

    def run_seq(n, qk_ref, v_ref, g_ref, lr_ref, acc_ref, o_ref, latent):
        half = n // 2

        def rows_of(c):
            return pl.ds(pl.multiple_of(c * L, L), L)

        def chunks_of(i):
            cs = [(True, i * GLA_PAIR + u) for u in range(GLA_PAIR)]
            cs += [(False, n - 1 - (i * GLA_PAIR + u)) for u in range(GLA_PAIR)]
            return cs

        def first(i, carry):
            cs = chunks_of(i)
            res = run_chunks([(f, c, qk_ref, v_ref, lr_ref, latent) for f, c in cs])
            for (f, c), o in zip(cs, res):
                acc_ref[rows_of(c), :] = o
            return carry

        def second(i, carry):
            cs = chunks_of(i)
            res = run_chunks([(f, c, qk_ref, v_ref, lr_ref, latent) for f, c in cs])
            for (f, c), o in zip(cs, res):
                o_ref[rows_of(c), :] = finish(acc_ref[rows_of(c), :] + o, g_ref, c)
            return carry

        assert half % GLA_PAIR == 0
        lax.fori_loop(0, half // GLA_PAIR, first, 0)
        lax.fori_loop(half // GLA_PAIR, n // GLA_PAIR, second, 0)

    sf_ref[...] = jnp.zeros_like(sf_ref)
    sb_ref[...] = jnp.zeros_like(sb_ref)
    run_seq(n_c, qkc_ref, vc_ref, gc_ref, lrc_ref, accc_ref, oc_ref, False)
    run_seq(n_x, qkx_ref, vx_ref, gx_ref, lrx_ref, accx_ref, ox_ref, True)


def _rope_tables(seq):
    half = GLA_DK // 2
    nf = half // 2
    inv = ROPE_BASE ** (-jnp.arange(nf, dtype=F32) / nf)
    t = jnp.arange(seq)
    rows = (t // GRID_W).astype(F32)
    cols = (t % GRID_W).astype(F32)
    ar = rows[:, None] * inv[None, :]
    ac = cols[:, None] * inv[None, :]
    cos_h = jnp.concatenate([jnp.cos(ar), jnp.cos(ar), jnp.cos(ac), jnp.cos(ac)], axis=-1)
    sin_h = jnp.concatenate([-jnp.sin(ar), jnp.sin(ar), -jnp.sin(ac), jnp.sin(ac)], axis=-1)
    return jnp.tile(cos_h, (1, GLA_HEADS)), jnp.tile(sin_h, (1, GLA_HEADS))


def _gla(qk, v, g, lr, dwcat, dbcat, gla_g, seq, ctx_len, batch):
    rows = qk.shape[0]
    cos_t, sin_t = _rope_tables(seq)
    cb = (batch * seq) // ctx_len
    xs = lambda w: pl.BlockSpec((seq, w), lambda b: (b, 0))
    cs = lambda w: pl.BlockSpec((ctx_len, w), lambda b: (cb + b, 0))
    full = lambda a: pl.BlockSpec(a.shape, lambda b: (0,) * a.ndim)
    out_x, out_c = pl.pallas_call(
        functools.partial(_gla_kernel, n_x=seq // GLA_CHUNK, n_c=ctx_len // GLA_CHUNK),
        grid=(batch,),
        in_specs=[xs(2 * GLA_QK), xs(GLA_V), xs(GLA_V), xs(LANES),
                  cs(2 * GLA_QK), cs(GLA_V), cs(GLA_V), cs(LANES),
                  full(dwcat), full(dbcat), full(cos_t), full(sin_t), full(gla_g)],
        out_specs=[pl.BlockSpec((seq, GLA_V), lambda b: (b, 0)),
                   pl.BlockSpec((ctx_len, GLA_V), lambda b: (b, 0))],
        out_shape=[jax.ShapeDtypeStruct((batch * seq, GLA_V), BF16),
                   jax.ShapeDtypeStruct((batch * ctx_len, GLA_V), BF16)],
        scratch_shapes=[pltpu.VMEM((GLA_V, GLA_QK), F32), pltpu.VMEM((GLA_V, GLA_QK), F32),
                        pltpu.VMEM((seq, GLA_V), F32), pltpu.VMEM((ctx_len, GLA_V), F32)],
        compiler_params=_params(("arbitrary",)),
        name="gla",
    )(qk, v, g, lr, qk, v, g, lr, dwcat, dbcat, cos_t, sin_t, gla_g)
    del rows
    return jnp.concatenate([out_x, out_c], axis=0)


def _conv_kernel(glu_ref, w_ref, cb_ref, lg_ref, lb_ref, o_ref, pad_ref, win_ref, *, t_len):
    ch = CONV_CH
    zeros = jnp.zeros((CONV_HALO, ch), F32)
    pad_ref[0:CONV_HALO, :] = zeros
    pad_ref[CONV_HALO + t_len:2 * CONV_HALO + t_len, :] = zeros
    fill = 128

    def fill_body(i, carry):
        r = pl.multiple_of(i * fill, fill)
        a = glu_ref[pl.ds(r, fill), 0:ch].astype(F32)
        b = glu_ref[pl.ds(r, fill), ch:2 * ch].astype(F32)
        pad_ref[pl.ds(CONV_HALO + r, fill), :] = a * _sigmoid(b)
        return carry

    lax.fori_loop(0, t_len // fill, fill_body, 0)
    first_tap = CONV_HALO - CONV_WIDTH // 2

    def body(i, carry):
        base = pl.multiple_of(i * CONV_RC, CONV_RC)
        win_ref[...] = pad_ref[pl.ds(base, CONV_RC + 2 * CONV_HALO), :]
        acc = jnp.zeros((CONV_RC, ch), F32)
        for j in range(CONV_WIDTH):
            acc = acc + w_ref[j:j + 1, :] * win_ref[first_tap + j:first_tap + j + CONV_RC, :]
        acc = acc + cb_ref[...]
        mu = jnp.mean(acc, axis=-1, keepdims=True)
        cen = acc - mu
        var = jnp.mean(cen * cen, axis=-1, keepdims=True)
        y = cen * lax.rsqrt(var + LN_EPS) * lg_ref[...] + lb_ref[...]
        o_ref[pl.ds(base, CONV_RC), :] = (y * _sigmoid(y)).astype(BF16)
        return carry

    lax.fori_loop(0, t_len // CONV_RC, body, 0)


def _conv_module(glu, conv_w, conv_b, ln_g, ln_b, t_len, first_block, batch):
    full = lambda a: pl.BlockSpec(a.shape, lambda b: (0,) * a.ndim)
    return pl.pallas_call(
        functools.partial(_conv_kernel, t_len=t_len),
        grid=(batch,),
        in_specs=[pl.BlockSpec((t_len, 2 * CONV_CH), lambda b: (first_block + b, 0)),
                  full(conv_w), full(conv_b), full(ln_g), full(ln_b)],
        out_specs=pl.BlockSpec((t_len, CONV_CH), lambda b: (b, 0)),
        out_shape=jax.ShapeDtypeStruct((batch * t_len, CONV_CH), BF16),
        scratch_shapes=[pltpu.VMEM((t_len + 2 * CONV_HALO, CONV_CH), F32),
                        pltpu.VMEM((CONV_RC + 2 * CONV_HALO, CONV_CH), F32)],
        compiler_params=_params(("arbitrary",)),
        name="conv_module",
    )(glu, conv_w, conv_b, ln_g, ln_b)


def _outproj_kernel(*refs, n_lhs, gate_i):
    lhs_refs = refs[:n_lhs]
    w_ref, x_ref, mod_ref, g_ref, o_ref = refs[n_lhs:]
    off = 0
    y = None
    for l_ref in lhs_refs:
        kk = l_ref.shape[-1]
        t = jnp.dot(l_ref[...], w_ref[off:off + kk, :], preferred_element_type=F32)
        y = t if y is None else y + t
        off += kk
    ms = jnp.mean(y * y, axis=-1, keepdims=True)
    yn = y * lax.rsqrt(ms + RMS_EPS) * g_ref[...]
    o_ref[...] = x_ref[...] + mod_ref[0, gate_i:gate_i + 1, :] * yn


def _outproj(lhs_list, w, xt, mod, g, gate_i, n_rows, seq, batch):
    d = xt.shape[1]
    n_lhs = len(lhs_list)
    in_specs = [pl.BlockSpec((ROW_TILE, a.shape[1]), lambda i: (i, 0)) for a in lhs_list]
    in_specs += [pl.BlockSpec(w.shape, lambda i: (0, 0)),
                 pl.BlockSpec((ROW_TILE, d), lambda i: (i, 0)),
                 pl.BlockSpec((1, ADA_CHUNKS, d), _mod_index(ROW_TILE, seq, batch)),
                 pl.BlockSpec((1, d), lambda i: (0, 0))]
    return pl.pallas_call(
        functools.partial(_outproj_kernel, n_lhs=n_lhs, gate_i=gate_i),
        grid=(n_rows // ROW_TILE,),
        in_specs=in_specs,
        out_specs=pl.BlockSpec((ROW_TILE, d), lambda i: (i, 0)),
        out_shape=jax.ShapeDtypeStruct(xt.shape, F32),
        input_output_aliases={n_lhs + 1: 0},
        compiler_params=_params(("arbitrary",)),
        name="outproj",
    )(*lhs_list, w, xt, mod, g.reshape(1, d))


def _na_kernel(q_ref, k_ref, v_ref, kc_ref, vc_ref, bias_ref, o_ref, *, n_rows):
    wr = NA_WIN_ROWS
    lane = lax.broadcasted_iota(jnp.int32, (1, LANES), 1)
    head0 = lane < NA_DH
    kc = kc_ref[...]
    vc = vc_ref[...]

    def body(it, carry):
        rs = [it * NA_ROW_UNROLL + u for u in range(NA_ROW_UNROLL)]
        r0s = [jnp.clip(r - wr // 2, 0, n_rows - wr) for r in rs]
        win = [pl.ds(pl.multiple_of(r0 * GRID_W, GRID_W), wr * GRID_W) for r0 in r0s]
        scores = []
        for r, r0, w in zip(rs, r0s, win):
            q2 = q_ref[pl.ds(pl.multiple_of(r * GRID_W, GRID_W), GRID_W), :] * (NA_DH ** -0.5)
            zero = jnp.zeros_like(q2)
            qs = jnp.concatenate([jnp.where(head0, q2, zero), jnp.where(head0, zero, q2)], axis=0)
            bias = bias_ref[:, r - r0].reshape(2 * GRID_W, wr * GRID_W)
            s_w = lax.dot_general(qs, k_ref[w, :], _NT, preferred_element_type=F32) + bias
            s_c = lax.dot_general(qs, kc, _NT, preferred_element_type=F32)
            scores.append((s_w, s_c))
        probs = []
        for s_w, s_c in scores:
            m = jnp.maximum(jnp.max(s_w, axis=-1, keepdims=True), jnp.max(s_c, axis=-1, keepdims=True))
            p_w = jnp.exp(s_w - m)
            p_c = jnp.exp(s_c - m)
            den = jnp.sum(p_w, axis=-1, keepdims=True) + jnp.sum(p_c, axis=-1, keepdims=True)
            probs.append((p_w.astype(BF16), p_c.astype(BF16), den))
        for r, w, (p_w, p_c, den) in zip(rs, win, probs):
            o = (jnp.dot(p_w, v_ref[w, :], preferred_element_type=F32)
                 + jnp.dot(p_c, vc, preferred_element_type=F32)) / den
            o_ref[pl.ds(pl.multiple_of(r * GRID_W, GRID_W), GRID_W), :] = (
                jnp.where(head0, o[:GRID_W], o[GRID_W:]).astype(BF16))
        return carry

    assert n_rows % NA_ROW_UNROLL == 0
    lax.fori_loop(0, n_rows // NA_ROW_UNROLL, body, 0)


def _na_bias_table(rpb, n_rows):
    wr = min(NA_WIN_ROWS, n_rows)
    qc = np.arange(GRID_W)
    cstart = np.clip(qc - NA_WIN_COLS // 2, 0, GRID_W - NA_WIN_COLS)
    kcol = np.arange(GRID_W)
    ok = (kcol[None, :] >= cstart[:, None]) & (kcol[None, :] < cstart[:, None] + NA_WIN_COLS)
    cidx = np.clip(kcol[None, :] - qc[:, None] + NA_WIN_COLS - 1, 0, 2 * NA_WIN_COLS - 2)
    delta = np.arange(wr)
    j = np.arange(wr)
    ridx = j[None, :] - delta[:, None] + NA_WIN_ROWS - 1
    t = rpb[:, ridx][:, :, :, cidx]
    t = jnp.where(jnp.asarray(ok)[None, None, None], t.astype(F32), -jnp.inf)
    t = t.transpose(0, 1, 3, 2, 4)
    return t.reshape(rpb.shape[0], wr, GRID_W, wr * GRID_W)


def _na_attention(z, bias, seq, ctx_len, batch):
    n_pairs = NA_HEADS * NA_DH // LANES
    n_rows = seq // GRID_W
    cb = (batch * seq) // ctx_len
    return pl.pallas_call(
        functools.partial(_na_kernel, n_rows=n_rows),
        grid=(n_pairs, batch),
        in_specs=[pl.BlockSpec((seq, LANES), lambda p, b: (b, p)),
                  pl.BlockSpec((seq, LANES), lambda p, b: (b, n_pairs + p)),
                  pl.BlockSpec((seq, LANES), lambda p, b: (b, 2 * n_pairs + p)),
                  pl.BlockSpec((ctx_len, LANES), lambda p, b: (cb + b, n_pairs + p)),
                  pl.BlockSpec((ctx_len, LANES), lambda p, b: (cb + b, 2 * n_pairs + p)),
                  pl.BlockSpec((2,) + bias.shape[1:], lambda p, b: (p, 0, 0, 0))],
        out_specs=pl.BlockSpec((seq, LANES), lambda p, b: (b, p)),
        out_shape=jax.ShapeDtypeStruct((batch * seq, NA_HEADS * NA_DH), BF16),
        compiler_params=_params(("arbitrary", "arbitrary")),
        name="na_attention",
    )(z, z, z, z, z, bias)


def _store_token_tiles(ref, val, row0=0):
    n, d = val.shape
    nc = d // LANES
    for k in range(nc):
        ref[pl.ds(row0 * nc + k, n, stride=nc), :] = val[:, k * LANES:(k + 1) * LANES]


def _load_token_tiles(ref, row0, n, nc, lead=None):
    chunks = []
    for k in range(nc):
        idx = pl.ds(row0 * nc + k, n, stride=nc)
        chunks.append(ref[idx, :] if lead is None else ref[lead, idx, :])
    return jnp.concatenate(chunks, axis=-1)


def _first_argmax(x, iota_f, n):
    m = jnp.max(x, axis=0, keepdims=True)
    idx = jnp.min(jnp.where(x == m, iota_f, float(n)), axis=0, keepdims=True)
    return m, idx


def _moe_pre_kernel(x_ref, mod_ref, g_ref, rwh_ref, rwl_ref, rb_ref, wsg_ref, wsu_ref, wsd_ref,
                    h_ref, y0_ref, eidx_ref, gates_ref, rank_ref, cnt_ref, base_ref):
    tm = x_ref.shape[0]
    n_e = rwh_ref.shape[0]

    @pl.when(pl.program_id(0) == 0)
    def _():
        base_ref[...] = jnp.zeros_like(base_ref)

    h = _prenorm_mod(x_ref[...], g_ref[...], mod_ref, 3, 4)
    _store_token_tiles(h_ref, h)
    hb = h.astype(BF16)
    gate = jnp.dot(hb, wsg_ref[...], preferred_element_type=F32)
    up = jnp.dot(hb, wsu_ref[...], preferred_element_type=F32)
    hid = (gate * _sigmoid(gate) * up).astype(BF16)
    y0_ref[...] = jnp.dot(hid, wsd_ref[...], preferred_element_type=F32)

    h_lo = (h - hb.astype(F32)).astype(BF16)
    d = lambda a, b: lax.dot_general(a, b, _NT, preferred_element_type=F32)
    logits = d(rwh_ref[...], hb) + (d(rwh_ref[...], h_lo) + d(rwl_ref[...], hb))
    scores = _sigmoid(logits)
    biased = scores + rb_ref[...]
    neg = -jnp.inf
    gsz = n_e // N_GROUPS
    io_g = lax.broadcasted_iota(jnp.int32, (gsz, tm), 0).astype(F32)
    grp_rows = []
    for gi in range(N_GROUPS):
        xg = biased[gi * gsz:(gi + 1) * gsz]
        m1, i1 = _first_argmax(xg, io_g, gsz)
        m2 = jnp.max(jnp.where(io_g == i1, neg, xg), axis=0, keepdims=True)
        grp_rows.append(m1 + m2)
    cur = jnp.concatenate(grp_rows, axis=0)
    io_n = lax.broadcasted_iota(jnp.int32, (N_GROUPS, tm), 0).astype(F32)
    keep = jnp.zeros((N_GROUPS, tm), F32)
    for _ in range(TOPK_GROUPS):
        _, ii = _first_argmax(cur, io_n, N_GROUPS)
        sel = io_n == ii
        keep = jnp.where(sel, 1.0, keep)
        cur = jnp.where(sel, neg, cur)
    cur = jnp.concatenate([jnp.where(keep[gi:gi + 1] > 0.0, biased[gi * gsz:(gi + 1) * gsz], neg)
                           for gi in range(N_GROUPS)], axis=0)
    io_e = lax.broadcasted_iota(jnp.int32, (n_e, tm), 0).astype(F32)
    idxs, gvals = [], []
    chosen = jnp.zeros((n_e, tm), F32)
    for _ in range(TOP_K):
        _, ii = _first_argmax(cur, io_e, n_e)
        sel = io_e == ii
        idxs.append(ii)
        gvals.append(jnp.sum(jnp.where(sel, scores, 0.0), axis=0, keepdims=True))
        chosen = jnp.where(sel, 1.0, chosen)
        cur = jnp.where(sel, neg, cur)
    gv = jnp.concatenate(gvals, axis=0)
    gates_ref[...] = gv / jnp.sum(gv, axis=0, keepdims=True) * ROUTE_SCALE
    eidx_ref[...] = jnp.concatenate(idxs, axis=0).astype(jnp.int32)

    earlier = (lax.broadcasted_iota(jnp.int32, (tm, tm), 0) < lax.broadcasted_iota(jnp.int32, (tm, tm), 1))
    before = jnp.dot(chosen.astype(BF16), jnp.where(earlier, 1.0, 0.0).astype(BF16), preferred_element_type=F32)
    before = before + base_ref[:, 0:1]
    ranks = [jnp.sum(jnp.where(io_e == ii, before, 0.0), axis=0, keepdims=True) for ii in idxs]
    rank_ref[...] = jnp.concatenate(ranks, axis=0).astype(jnp.int32)
    base_ref[...] = base_ref[...] + jnp.sum(chosen, axis=1, keepdims=True)
    cnt_ref[...] = base_ref[...].astype(jnp.int32)


def _moe_pre(xt, mod, g, rwt_hi, rwt_lo, rb, wsg, wsu, wsd, n_tok, seq, batch):
    d = xt.shape[1]
    e = rwt_hi.shape[0]
    full = lambda a: pl.BlockSpec(a.shape, lambda i: (0,) * a.ndim)
    tok_major = lambda: pl.BlockSpec((TOP_K, ROW_TILE), lambda i: (0, i))
    return pl.pallas_call(
        _moe_pre_kernel,
        grid=(n_tok // ROW_TILE,),
        in_specs=[pl.BlockSpec((ROW_TILE, d), lambda i: (i, 0)),
                  pl.BlockSpec((1, ADA_CHUNKS, d), _mod_index(ROW_TILE, seq, batch)),
                  pl.BlockSpec((1, d), lambda i: (0, 0)),
                  full(rwt_hi), full(rwt_lo), full(rb), full(wsg), full(wsu), full(wsd)],
        out_specs=[pl.BlockSpec((ROW_TILE * (d // LANES), LANES), lambda i: (i, 0)),
                   pl.BlockSpec((ROW_TILE, d), lambda i: (i, 0)),
                   tok_major(), tok_major(), tok_major(),
                   pl.BlockSpec((e, LANES), lambda i: (0, 0))],
        out_shape=[jax.ShapeDtypeStruct((n_tok * (d // LANES), LANES), F32),
                   jax.ShapeDtypeStruct((n_tok, d), F32),
                   jax.ShapeDtypeStruct((TOP_K, n_tok), jnp.int32),
                   jax.ShapeDtypeStruct((TOP_K, n_tok), F32),
                   jax.ShapeDtypeStruct((TOP_K, n_tok), jnp.int32),
                   jax.ShapeDtypeStruct((e, LANES), jnp.int32)],
        scratch_shapes=[pltpu.VMEM((e, LANES), F32)],
        compiler_params=_params(("arbitrary",)),
        name="moe_pre",
    )(xt, mod, g.reshape(1, d), rwt_hi, rwt_lo, rb, wsg, wsu, wsd)


def _moe_pos_kernel(eidx_ref, rank_ref, pstart_ref, pos_ref):
    n_e = pstart_ref.shape[0]
    tp = eidx_ref.shape[1]
    io_e = lax.broadcasted_iota(jnp.int32, (n_e, tp), 0)
    ps = pstart_ref[...]
    e = eidx_ref[...]
    rows = [jnp.sum(jnp.where(io_e == e[k:k + 1, :], ps, 0.0), axis=0, keepdims=True) for k in range(TOP_K)]
    pos_ref[...] = rank_ref[...] + jnp.concatenate(rows, axis=0).astype(jnp.int32)


def _moe_pos(eidx, rank, pstart):
    n_tok = eidx.shape[1]
    blk = lambda: pl.BlockSpec((TOP_K, ROW_TILE), lambda i: (0, i))
    return pl.pallas_call(
        _moe_pos_kernel,
        grid=(n_tok // ROW_TILE,),
        in_specs=[blk(), blk(), pl.BlockSpec(pstart.shape, lambda i: (0, 0))],
        out_specs=blk(),
        out_shape=jax.ShapeDtypeStruct(eidx.shape, jnp.int32),
        compiler_params=_params(("arbitrary",)),
        name="moe_pos",
    )(eidx, rank, pstart)


def _moe_dispatch_kernel(lastblk_ref, nused_ref, pos_hbm, h_hbm, xg_hbm, idx_smem, zbuf, hbuf, idx_sem, row_sem, z_sem,
                         h_sem, *, n_tiles, nc, n_blocks):
    i = pl.program_id(0)
    slot = i % 2
    n_e = lastblk_ref.shape[0]
    rows_per_step = DISP_TT * TOP_K

    def h_copy(t, s):
        return pltpu.make_async_copy(h_hbm.at[pl.ds(pl.multiple_of(t * (DISP_TT * nc), DISP_TT * nc), DISP_TT * nc), :],
                                     hbuf.at[s], h_sem.at[s])

    def idx_copy(t, s):
        return pltpu.make_async_copy(pos_hbm.at[pl.ds(pl.multiple_of(t * IDX_TILE, IDX_TILE), IDX_TILE)],
                                     idx_smem.at[pl.ds(pl.multiple_of(s * IDX_TILE, IDX_TILE), IDX_TILE)],
                                     idx_sem.at[s])

    def block_copy(b):
        return pltpu.make_async_copy(zbuf, xg_hbm.at[pl.ds(pl.multiple_of(b * (MOE_RB * nc), MOE_RB * nc),
                                                           MOE_RB * nc), :], z_sem)

    def wait_rows(s):
        pltpu.make_async_copy(h_hbm.at[pl.ds(0, rows_per_step * nc), :], xg_hbm.at[pl.ds(0, rows_per_step * nc), :],
                              row_sem.at[s]).wait()

    @pl.when(i == 0)
    def _():
        idx_copy(0, 0).start()
        zbuf[...] = jnp.zeros_like(zbuf)

        def for_zero_blocks(fn):
            def per_expert(e, carry):
                @pl.when(lastblk_ref[e] >= 0)
                def _():
                    fn(block_copy(lastblk_ref[e]))
                return carry

            def per_block(b, carry):
                fn(block_copy(b))
                return carry

            lax.fori_loop(0, n_e, per_expert, 0)
            lax.fori_loop(nused_ref[0], n_blocks, per_block, 0)

        for_zero_blocks(lambda cp: cp.start())
        for_zero_blocks(lambda cp: cp.wait())

        h_copy(0, 0).start()

    idx_copy(i, slot).wait()
    hslot = i % 3
    h_copy(i, hslot).wait()

    @pl.when(i + 1 < n_tiles)
    def _():
        idx_copy(i + 1, 1 - slot).start()
        h_copy(i + 1, (i + 1) % 3).start()

    def body(r, carry):
        src = hbuf.at[hslot, pl.ds(pl.multiple_of(r * nc, nc), nc), :]
        base = slot * IDX_TILE + r
        for k in range(TOP_K):
            p = idx_smem[base + k * DISP_TT]
            pltpu.make_async_copy(src, xg_hbm.at[pl.ds(pl.multiple_of(p * nc, nc), nc), :],
                                  row_sem.at[slot]).start(priority=k % 2)
        return carry

    lax.fori_loop(0, DISP_TT, body, 0, unroll=2)

    @pl.when(i > 0)
    def _():
        wait_rows(1 - slot)

    @pl.when(i == n_tiles - 1)
    def _():
        wait_rows(slot)


def _moe_dispatch(lastblk, nused, pos, h, n_blocks):
    n_tok = pos.shape[0] // TOP_K
    nc = h.shape[0] // n_tok
    n_tiles = n_tok // DISP_TT
    grid_spec = pltpu.PrefetchScalarGridSpec(
        num_scalar_prefetch=2,
        grid=(n_tiles,),
        in_specs=[pl.BlockSpec(memory_space=pl.ANY), pl.BlockSpec(memory_space=pl.ANY)],
        out_specs=pl.BlockSpec(memory_space=pl.ANY),
        scratch_shapes=[pltpu.SMEM((2 * IDX_TILE,), jnp.int32),
                        pltpu.VMEM((MOE_RB * nc, LANES), F32),
                        pltpu.VMEM((3, DISP_TT * nc, LANES), F32),
                        pltpu.SemaphoreType.DMA((2,)),
                        pltpu.SemaphoreType.DMA((2,)),
                        pltpu.SemaphoreType.DMA(()),
                        pltpu.SemaphoreType.DMA((3,))],
    )
    return pl.pallas_call(
        functools.partial(_moe_dispatch_kernel, n_tiles=n_tiles, nc=nc, n_blocks=n_blocks),
        grid_spec=grid_spec,
        out_shape=jax.ShapeDtypeStruct((n_blocks * MOE_RB * nc, LANES), F32),
        compiler_params=_params(("arbitrary",)),
        name="moe_dispatch",
    )(lastblk, nused, pos, h)


def _moe_expert_kernel(be_ref, nused_ref, x_ref, wg_ref, wu_ref, wd_ref, o_ref, wgb, wub, wdb):
    i = pl.program_id(0)
    nused = nused_ref[0]
    nc = x_ref.shape[0] // MOE_RB
    new_expert = jnp.logical_or(i == 0, be_ref[i] != be_ref[jnp.maximum(i - 1, 0)])

    @pl.when(jnp.logical_and(i < nused, new_expert))
    def _():
        wgb[...] = wg_ref[...].astype(BF16)
        wub[...] = wu_ref[...].astype(BF16)
        wdb[...] = wd_ref[...].astype(BF16)

    @pl.when(i < nused)
    def _():
        xb = _load_token_tiles(x_ref, 0, MOE_RB, nc).astype(BF16)
        gate = jnp.dot(xb, wgb[...], preferred_element_type=F32)
        up = jnp.dot(xb, wub[...], preferred_element_type=F32)
        hid = (gate * _sigmoid(gate) * up).astype(BF16)
        _store_token_tiles(o_ref, jnp.dot(hid, wdb[...], preferred_element_type=F32))

    @pl.when(i >= nused)
    def _():
        o_ref[...] = jnp.zeros_like(o_ref)


def _moe_experts(block_e, nused, xg, wg, wu, wd, layer):
    nb = block_e.shape[0]
    d = wg.shape[2]
    nc = d // LANES
    de = wg.shape[3]
    grid_spec = pltpu.PrefetchScalarGridSpec(
        num_scalar_prefetch=2,
        grid=(nb,),
        in_specs=[pl.BlockSpec((MOE_RB * nc, LANES), lambda i, be, nu: (jnp.minimum(i, nu[0] - 1), 0)),
                  pl.BlockSpec((None, None, d, de), lambda i, be, nu: (layer, be[i], 0, 0)),
                  pl.BlockSpec((None, None, d, de), lambda i, be, nu: (layer, be[i], 0, 0)),
                  pl.BlockSpec((None, None, de, d), lambda i, be, nu: (layer, be[i], 0, 0))],
        out_specs=pl.BlockSpec((MOE_RB * nc, LANES), lambda i, be, nu: (i, 0)),
        scratch_shapes=[pltpu.VMEM((d, de), BF16), pltpu.VMEM((d, de), BF16), pltpu.VMEM((de, d), BF16)],
    )
    return pl.pallas_call(
        _moe_expert_kernel,
        grid_spec=grid_spec,
        out_shape=jax.ShapeDtypeStruct((nb * MOE_RB * nc, LANES), F32),
        compiler_params=_params(("arbitrary",)),
        name="moe_experts",
    )(block_e, nused, xg, wg, wu, wd)


def _moe_combine_kernel(pos_hbm, og_hbm, gates_ref, y0_ref, x_ref, mod_ref, g_ref, o_ref,
                        idx_smem, buf, idx_sem, row_sem, *, n_tiles):
    i = pl.program_id(0)
    slot = i % 2
    nxt = 1 - slot
    n_rows = TOP_K * DISP_TT
    nc = buf.shape[1] // n_rows

    def idx_copy(t, s):
        return pltpu.make_async_copy(pos_hbm.at[pl.ds(pl.multiple_of(t * IDX_TILE, IDX_TILE), IDX_TILE)],
                                     idx_smem.at[pl.ds(pl.multiple_of(s * IDX_TILE, IDX_TILE), IDX_TILE)],
                                     idx_sem.at[s])

    def issue_rows(s):
        def body(r, carry):
            base = s * IDX_TILE + r
            for k in range(TOP_K):
                p = idx_smem[base + k * DISP_TT]
                pltpu.make_async_copy(og_hbm.at[pl.ds(pl.multiple_of(p * nc, nc), nc), :],
                                      buf.at[s, pl.ds(pl.multiple_of((k * DISP_TT + r) * nc, nc), nc), :],
                                      row_sem.at[s]).start(priority=k % 2)
            return carry
        lax.fori_loop(0, DISP_TT, body, 0, unroll=2)

    def wait_rows(s):
        pltpu.make_async_copy(og_hbm.at[pl.ds(0, n_rows * nc), :], buf.at[s], row_sem.at[s]).wait()

    @pl.when(i == 0)
    def _():
        idx_copy(0, 0).start()
        idx_copy(0, 0).wait()
        issue_rows(0)

        @pl.when(n_tiles > 1)
        def _():
            idx_copy(1, 1).start()

    @pl.when(i + 1 < n_tiles)
    def _():
        idx_copy(i + 1, nxt).wait()
        issue_rows(nxt)

    wait_rows(slot)

    @pl.when(i + 2 < n_tiles)
    def _():
        idx_copy(i + 2, slot).start()

    gates = gates_ref[...].T
    y = y0_ref[...]
    for k in range(TOP_K):
        y = y + gates[:, k:k + 1] * _load_token_tiles(buf, k * DISP_TT, DISP_TT, nc, lead=slot)
    ms = jnp.mean(y * y, axis=-1, keepdims=True)
    yn = y * lax.rsqrt(ms + RMS_EPS) * g_ref[...]
    o_ref[...] = x_ref[...] + mod_ref[0, 5:6, :] * yn


def _moe_combine(pos, og, gates, y0, xt, mod, g, n_tok, seq, batch):
    d = xt.shape[1]
    n_tiles = n_tok // DISP_TT
    row_spec = lambda w: pl.BlockSpec((DISP_TT, w), lambda i: (i, 0))
    return pl.pallas_call(
        functools.partial(_moe_combine_kernel, n_tiles=n_tiles),
        grid=(n_tiles,),
        in_specs=[pl.BlockSpec(memory_space=pl.ANY),
                  pl.BlockSpec(memory_space=pl.ANY),
                  pl.BlockSpec((TOP_K, DISP_TT), lambda i: (0, i)), row_spec(d), row_spec(d),
                  pl.BlockSpec((1, ADA_CHUNKS, d), _mod_index(DISP_TT, seq, batch)),
                  pl.BlockSpec((1, d), lambda i: (0, 0))],
        out_specs=row_spec(d),
        out_shape=jax.ShapeDtypeStruct((n_tok, d), F32),
        input_output_aliases={4: 0} if n_tok == xt.shape[0] else {},
        scratch_shapes=[pltpu.SMEM((2 * IDX_TILE,), jnp.int32),
                        pltpu.VMEM((2, TOP_K * DISP_TT * (d // LANES), LANES), F32),
                        pltpu.SemaphoreType.DMA((2,)),
                        pltpu.SemaphoreType.DMA((2,))],
        compiler_params=_params(("arbitrary",)),
        name="moe_combine",
    )(pos, og, gates, y0, xt, mod, g.reshape(1, d))


def _dispatch_plan(counts, n_assign):
    e = counts.shape[0]
    padded = (counts + MOE_RB - 1) // MOE_RB * MOE_RB
    pend = jnp.cumsum(padded)
    pstart = pend - padded
    nb = -(-(n_assign + e * (MOE_RB - 1)) // MOE_RB)
    first_row = jnp.arange(nb, dtype=jnp.int32) * MOE_RB
    block_e = jnp.minimum(jnp.sum((pend[None, :] <= first_row[:, None]).astype(jnp.int32), axis=1), e - 1)
    nused = (pend[-1] // MOE_RB).astype(jnp.int32).reshape(1)
    lastblk = jnp.where(counts > 0, pend // MOE_RB - 1, -1)
    return pstart.astype(jnp.int32), lastblk.astype(jnp.int32), block_e.astype(jnp.int32), nused, nb


def _moe_layer(xt, mod, g_pre, g_post, router_w, router_b, w_gate, w_up, w_down, ws_gate, ws_up, ws_down,
               n_tok, seq, batch, layer):
    e = router_w.shape[1]
    rwt = router_w.astype(F32).T
    rwt_hi, rwt_lo = _split(rwt)
    h, y0, eidx, gates, rank, cnt = _moe_pre(xt, mod, g_pre, rwt_hi, rwt_lo, router_b.astype(F32).reshape(e, 1),
                                             ws_gate.astype(BF16), ws_up.astype(BF16), ws_down.astype(BF16),
                                             n_tok, seq, batch)
    pstart, lastblk, block_e, nused, nb = _dispatch_plan(cnt[:, 0], n_tok * TOP_K)
    pos = _moe_pos(eidx, rank, pstart.astype(F32).reshape(e, 1))
    pos = pos.reshape(TOP_K, n_tok // DISP_TT, DISP_TT).transpose(1, 0, 2).reshape(-1)
    xg = _moe_dispatch(lastblk, nused, pos, h, nb)
    og = _moe_experts(block_e, nused, xg, w_gate, w_up, w_down, layer)
    return _moe_combine(pos, og, gates, y0, xt, mod, g_post, n_tok, seq, batch)


def kernel(x, c, ctx, c_ctx, ada_w, ada_b, g_pre_mix, g_post_mix, g_pre_ffn, g_post_ffn, ab_w_in, ab_w_out, gla_dw_f, gla_db_f, gla_dw_b, gla_db_b, gla_norm_g, conv_w, conv_b, conv_ln_g, conv_ln_b, na_w_in, na_w_out, na_rpb, moe_router_w, moe_router_b, moe_w_gate, moe_w_up, moe_w_down, moe_ws_gate, moe_ws_up, moe_ws_down):
    batch, seq, d = x.shape
    ctx_len = ctx.shape[1]
    depth = ada_w.shape[0]
    n_x = batch * seq
    n_all = n_x + batch * ctx_len
    assert seq % ROW_TILE == 0 and (batch * ctx_len) % ROW_TILE == 0 and n_x % ctx_len == 0

    xt = jnp.concatenate([x.reshape(n_x, d), ctx.reshape(batch * ctx_len, d)], axis=0)

    cc = jnp.zeros((ADA_ROWS, d), F32).at[:batch].set(c).at[batch].set(c_ctx)
    mods = _ada_mod(cc, ada_w, ada_b)[:, :batch + 1].reshape(depth, batch + 1, ADA_CHUNKS, d)

    for layer in range(depth):
        last = layer == depth - 1
        i = layer // 2
        mod = mods[layer]
        if layer % 2 == 0:
            w_in = ab_w_in[i]
            cuts = np.cumsum([0, GLA_QK, GLA_QK, GLA_V, GLA_V, GLA_LOWRANK, GLA_LOWRANK, CONV_CH, CONV_CH])
            seg = lambda j: w_in[:, cuts[j]:cuts[j + 1]]
            lr_pad = jnp.zeros((d, LANES - 2 * GLA_LOWRANK), w_in.dtype)
            w_cat = jnp.concatenate([seg(0), seg(1), seg(2), seg(3), seg(6), seg(7), seg(4), seg(5), lr_pad],
                                    axis=1).astype(BF16)
            qk, v, g, glu, lr = _prenorm_proj(xt, mod, g_pre_mix[layer], w_cat,
                                              (2 * GLA_QK, GLA_V, GLA_V, 2 * CONV_CH, LANES), 0, 1, seq, batch)
            dwcat = jnp.zeros((LANES, 2 * GLA_QK), F32)
            dwcat = dwcat.at[:GLA_LOWRANK, :GLA_QK].set(gla_dw_f[i])
            dwcat = dwcat.at[GLA_LOWRANK:2 * GLA_LOWRANK, GLA_QK:].set(gla_dw_b[i]).astype(BF16)
            dbcat = jnp.concatenate([gla_db_f[i], gla_db_b[i]]).reshape(1, 2 * GLA_QK).astype(F32)
            a_out = _gla(qk, v, g, lr, dwcat, dbcat, gla_norm_g[i].reshape(1, GLA_V).astype(F32),
                         seq, ctx_len, batch)
            cw = conv_w[i].astype(F32)
            cvec = lambda t: t.reshape(1, CONV_CH).astype(F32)
            cv_x = _conv_module(glu, cw, cvec(conv_b[i]), cvec(conv_ln_g[i]), cvec(conv_ln_b[i]), seq, 0, batch)
            cv_c = _conv_module(glu, cw, cvec(conv_b[i]), cvec(conv_ln_g[i]), cvec(conv_ln_b[i]), ctx_len,
                                n_x // ctx_len, batch)
            cv = jnp.concatenate([cv_x, cv_c], axis=0)
            n_rows = n_x if last else n_all
            xt = _outproj([a_out, cv], ab_w_out[i].astype(BF16), xt, mod, g_post_mix[layer], 2, n_rows, seq, batch)
        else:
            (z,) = _prenorm_proj(xt, mod, g_pre_mix[layer], na_w_in[i].astype(BF16),
                                 (3 * NA_HEADS * NA_DH,), 0, 1, seq, batch)
            bias = _na_bias_table(na_rpb[i], seq // GRID_W)
            o = _na_attention(z, bias, seq, ctx_len, batch)
            if last:
                xt = _outproj([o], na_w_out[i].astype(BF16), xt, mod, g_post_mix[layer], 2, n_x, seq, batch)
            else:
                raise NotImplementedError("context update after a neighbourhood-attention layer")
        n_tok = n_x if last else n_all
        xt = _moe_layer(xt, mod, g_pre_ffn[layer], g_post_ffn[layer], moe_router_w[layer], moe_router_b[layer],
                        moe_w_gate, moe_w_up, moe_w_down, moe_ws_gate[layer],
                        moe_ws_up[layer], moe_ws_down[layer], n_tok, seq, batch, layer)
    return xt[:n_x].reshape(batch, seq, d)
```
